```python
import jax, jax.numpy as jnp
from jax import lax
import numpy as np

D_MODEL = 2048
BATCH = 2
SEQ = 4096
DEPTH = 4

GRID_W = 64
CTX_LEN = 256
HEAD_DIM = 128
ROPE_BASE = 10000.0
EPS = 1e-6
NEG_INF = -1e30

RET_HEADS = 8
RET_DK = 128
RET_DV = 128
RET_CHUNK = 128
WIN_HEADS = 8
WIN_KV_HEADS = 2
WIN_RADIUS = 128
WIN_BLOCK = 128
CONV_WIDTH = 1024
CONV_K = 3
MLA_HEADS = 8
MLA_Q_RANK = 512
MLA_KV_RANK = 256
MLA_NOPE = 128
MLA_ROPE = 64
MLA_V = 128
MLA_BLOCK = 128
D_FF = 4 * D_MODEL

EVEN_SPLITS = (RET_HEADS * RET_DK, RET_HEADS * RET_DK, RET_HEADS * RET_DV, RET_HEADS * RET_DV,
               WIN_HEADS * HEAD_DIM, WIN_KV_HEADS * HEAD_DIM, WIN_KV_HEADS * HEAD_DIM)
EVEN_IN = sum(EVEN_SPLITS)
EVEN_MIX = RET_HEADS * RET_DV + WIN_HEADS * HEAD_DIM
ODD_SPLITS = (CONV_WIDTH, CONV_WIDTH, CONV_WIDTH, MLA_Q_RANK, MLA_KV_RANK, MLA_ROPE)
ODD_IN = sum(ODD_SPLITS)
ODD_MIX = CONV_WIDTH + MLA_HEADS * MLA_V
N_EVEN = (DEPTH + 1) // 2
N_ODD = DEPTH // 2

kernel_name = "hybrid_retention_window_conv_mla_dit"


def rms_norm(x, w):
    xf = x.astype(jnp.float32)
    y = xf * lax.rsqrt(jnp.mean(xf * xf, axis=-1, keepdims=True) + EPS)
    return (y * w.astype(jnp.float32)).astype(x.dtype)


def split_cols(h, sizes):
    return jnp.split(h, [int(s) for s in np.cumsum(sizes)[:-1]], axis=-1)


def to_heads(t, n):
    b, tl, _ = t.shape
    return t.reshape(b, tl, n, -1).transpose(0, 2, 1, 3)


def from_heads(t):
    b, h, tl, d = t.shape
    return t.transpose(0, 2, 1, 3).reshape(b, tl, h * d)


def rope_1d(x, pos):
    d = x.shape[-1]
    inv = ROPE_BASE ** (-jnp.arange(0, d, 2, dtype=jnp.float32) / d)
    ang = pos.astype(jnp.float32)[:, None] * inv[None, :]
    cos, sin = jnp.cos(ang), jnp.sin(ang)
    x1, x2 = jnp.split(x.astype(jnp.float32), 2, axis=-1)
    return jnp.concatenate([x1 * cos - x2 * sin, x2 * cos + x1 * sin], axis=-1).astype(x.dtype)


def rope_axial(x, row, col):
    xr, xc = jnp.split(x, 2, axis=-1)
    return jnp.concatenate([rope_1d(xr, row), rope_1d(xc, col)], axis=-1)


def modulation(cvec, w_mod, b_mod):
    return jnp.split(jax.nn.silu(cvec) @ w_mod + b_mod, 6, axis=-1)


def modulate(h, shift, scale):
    return h * (1 + scale[:, None]) + shift[:, None]


def sink_softmax(s, sink):
    sink = jnp.broadcast_to(sink, s.shape[:-1] + (1,))
    return jax.nn.softmax(jnp.concatenate([sink, s], axis=-1), axis=-1)[..., 1:]


def retention_chunkwise(q, k, v, log_gamma, s0, inclusive):
    b, h, tl, dk = q.shape
    dv = v.shape[-1]
    n = tl // RET_CHUNK
    qc = q.astype(jnp.float32).reshape(b, h, n, RET_CHUNK, dk)
    kc = k.astype(jnp.float32).reshape(b, h, n, RET_CHUNK, dk)
    vc = v.astype(jnp.float32).reshape(b, h, n, RET_CHUNK, dv)
    lg = log_gamma.astype(jnp.float32)[:, None]
    i = jnp.arange(RET_CHUNK, dtype=jnp.float32)
    diff = i[:, None] - i[None, :]
    mask = diff >= 0 if inclusive else diff > 0
    dmat = jnp.where(mask[None], jnp.exp(lg[:, :, None] * jnp.maximum(diff, 0.0)[None]), 0.0)
    scores = jnp.einsum('bhnid,bhnjd->bhnij', qc, kc) * dmat[None, :, None]
    o = jnp.einsum('bhnij,bhnje->bhnie', scores, vc)
    k_dec = kc * jnp.exp(lg * (RET_CHUNK - 1 - i))[None, :, None, :, None]
    kv = jnp.einsum('bhnjd,bhnje->bhnde', k_dec, vc)
    g_chunk = jnp.exp(lg[:, 0] * RET_CHUNK)[None, :, None, None]

    def step(s, kv_n):
        return g_chunk * s + kv_n, s

    s_fin, s_prev = lax.scan(step, s0.astype(jnp.float32), jnp.moveaxis(kv, 2, 0))
    s_prev = jnp.moveaxis(s_prev, 0, 2)
    q_dec = qc * jnp.exp(lg * (i + 1))[None, :, None, :, None]
    o = o + jnp.einsum('bhnid,bhnde->bhnie', q_dec, s_prev)
    return o.reshape(b, h, tl, dv), s_fin


def bidir_retention(q, k, v, lg, s0_f, s0_b):
    o_f, s_f = retention_chunkwise(q, k, v, lg[0], s0_f, True)
    flip = lambda t: jnp.flip(t, axis=2)
    o_b, s_b = retention_chunkwise(flip(q), flip(k), flip(v), lg[1], s0_b, False)
    return o_f + flip(o_b), s_f, s_b


def retention_output(o, g, gn_w):
    mu = jnp.mean(o, axis=-1, keepdims=True)
    var = jnp.mean(jnp.square(o - mu), axis=-1, keepdims=True)
    o = from_heads((o - mu) * lax.rsqrt(var + EPS)) * gn_w.astype(jnp.float32)
    return (jax.nn.silu(g.astype(jnp.float32)) * o).astype(g.dtype)


def retention_group(pc, px, log2_decay, gn_w, pos):
    q_c, k_c, v_c, g_c = pc
    q_x, k_x, v_x, g_x = px
    lg = jnp.log1p(-jnp.exp2(log2_decay.astype(jnp.float32)))
    sc = RET_DK ** -0.5
    q_c = to_heads(q_c, RET_HEADS) * sc
    k_c = to_heads(k_c, RET_HEADS)
    v_c = to_heads(v_c, RET_HEADS)
    q_x = rope_1d(to_heads(q_x, RET_HEADS), pos) * sc
    k_x = rope_1d(to_heads(k_x, RET_HEADS), pos)
    v_x = to_heads(v_x, RET_HEADS)
    s0 = jnp.zeros((q_c.shape[0], RET_HEADS, RET_DK, RET_DV), jnp.float32)
    o_c, s_f, s_b = bidir_retention(q_c, k_c, v_c, lg, s0, s0)
    o_x, _, _ = bidir_retention(q_x, k_x, v_x, lg, s_f, s_b)
    return retention_output(o_c, g_c, gn_w), retention_output(o_x, g_x, gn_w)


def window_attention_group(pc, px, sink, row, col):
    q_c, k_c, v_c = pc
    q_x, k_x, v_x = px
    grp = WIN_HEADS // WIN_KV_HEADS
    sc = HEAD_DIM ** -0.5

    def heads_q(t):
        b, tl, _ = t.shape
        return t.reshape(b, tl, WIN_KV_HEADS, grp, HEAD_DIM).transpose(0, 2, 3, 1, 4)

    def merge_q(t):
        b, kh, g, tl, d = t.shape
        return t.transpose(0, 3, 1, 2, 4).reshape(b, tl, kh * g * d)

    sink_kg = sink.astype(jnp.float32).reshape(WIN_KV_HEADS, grp)
    q_c = heads_q(q_c) * sc
    k_c = to_heads(k_c, WIN_KV_HEADS)
    v_c = to_heads(v_c, WIN_KV_HEADS)
    q_x = rope_axial(heads_q(q_x), row, col) * sc
    k_x = rope_axial(to_heads(k_x, WIN_KV_HEADS), row, col)
    v_x = to_heads(v_x, WIN_KV_HEADS)
    s_cc = jnp.einsum('bkgid,bkjd->bkgij', q_c, k_c).astype(jnp.float32)
    p_c = sink_softmax(s_cc, sink_kg[None, :, :, None, None]).astype(v_c.dtype)
    o_c = jnp.einsum('bkgij,bkjd->bkgid', p_c, v_c)
    b, kh, _, tl, d = q_x.shape
    nb = tl // WIN_BLOCK
    qb = q_x.reshape(b, kh, grp, nb, WIN_BLOCK, d)

    def band(t):
        tp = jnp.pad(t, ((0, 0), (0, 0), (WIN_BLOCK, WIN_BLOCK), (0, 0))).reshape(b, kh, nb + 2, WIN_BLOCK, d)
        return jnp.concatenate([tp[:, :, :-2], tp[:, :, 1:-1], tp[:, :, 2:]], axis=3)

    kw, vw = band(k_x), band(v_x)
    qpos = jnp.arange(tl).reshape(nb, WIN_BLOCK)
    kpos = (jnp.arange(nb)[:, None] - 1) * WIN_BLOCK + jnp.arange(3 * WIN_BLOCK)[None, :]
    valid = ((jnp.abs(qpos[:, :, None] - kpos[:, None, :]) <= WIN_RADIUS)
             & (kpos[:, None, :] >= 0) & (kpos[:, None, :] < tl))
    s_win = jnp.where(valid, jnp.einsum('bkgnid,bknjd->bkgnij', qb, kw).astype(jnp.float32), NEG_INF)
    s_ctx = jnp.einsum('bkgnid,bkjd->bkgnij', qb, k_c).astype(jnp.float32)
    n_ctx = k_c.shape[2]
    p = sink_softmax(jnp.concatenate([s_ctx, s_win], axis=-1),
                     sink_kg[None, :, :, None, None, None]).astype(v_x.dtype)
    o_x = (jnp.einsum('bkgnij,bkjd->bkgnid', p[..., :n_ctx], v_c)
           + jnp.einsum('bkgnij,bknjd->bkgnid', p[..., n_ctx:], vw))
    return merge_q(o_c), merge_q(o_x.reshape(b, kh, grp, tl, d))


def short_conv(bg, cg, xv, conv_w):
    u = cg * xv
    up = jnp.pad(u, ((0, 0), (1, 1), (0, 0)))
    z = up[:, :-2] * conv_w[0] + up[:, 1:-1] * conv_w[1] + up[:, 2:] * conv_w[2]
    return bg * z


def mla_group(pc, px, q_norm_w, kv_norm_w, w_uq, w_ukv, row, col):
    def project(cq, ckv, kr):
        q = to_heads(rms_norm(cq, q_norm_w) @ w_uq, MLA_HEADS)
        kv = to_heads(rms_norm(ckv, kv_norm_w) @ w_ukv, MLA_HEADS)
        return q[..., :MLA_NOPE], q[..., MLA_NOPE:], kv[..., :MLA_NOPE], kv[..., MLA_NOPE:], kr[:, None]

    qn_c, qr_c, kn_c, v_c, kr_c = project(*pc)
    qn_x, qr_x, kn_x, v_x, kr_x = project(*px)
    qr_x = rope_axial(qr_x, row, col)
    kr_x = rope_axial(kr_x, row, col)
    sc = (MLA_NOPE + MLA_ROPE) ** -0.5

    def join(qn, qr, kn, kr):
        q = jnp.concatenate([qn, qr], axis=-1) * sc
        k = jnp.concatenate([kn, jnp.broadcast_to(kr, kn.shape[:-1] + (MLA_ROPE,))], axis=-1)
        return q, k

    q_c, k_c = join(qn_c, qr_c, kn_c, kr_c)
    q_x, k_x = join(qn_x, qr_x, kn_x, kr_x)
    p_c = jax.nn.softmax(jnp.einsum('bhid,bhjd->bhij', q_c, k_c).astype(jnp.float32), axis=-1)
    o_c = jnp.einsum('bhij,bhjd->bhid', p_c.astype(v_c.dtype), v_c)
    keys = jnp.concatenate([k_c, k_x], axis=2)
    vals = jnp.concatenate([v_c, v_x], axis=2)
    b, h, tl, dq = q_x.shape
    nb = tl // MLA_BLOCK
    qb = jnp.moveaxis(q_x.reshape(b, h, nb, MLA_BLOCK, dq), 2, 0)

    def attend(qblk):
        p = jax.nn.softmax(jnp.einsum('bhid,bhjd->bhij', qblk, keys).astype(jnp.float32), axis=-1)
        return jnp.einsum('bhij,bhjd->bhid', p.astype(vals.dtype), vals)

    o_x = jnp.moveaxis(lax.map(attend, qb), 0, 2).reshape(b, h, tl, MLA_V)
    return from_heads(o_c), from_heads(o_x)


def even_mixer(n_c, n_x, w_in, log2_decay, gn_w, sink, pos, row, col):
    pc = split_cols(n_c @ w_in, EVEN_SPLITS)
    px = split_cols(n_x @ w_in, EVEN_SPLITS)
    r_c, r_x = retention_group(pc[:4], px[:4], log2_decay, gn_w, pos)
    a_c, a_x = window_attention_group(pc[4:], px[4:], sink, row, col)
    return jnp.concatenate([r_c, a_c], axis=-1), jnp.concatenate([r_x, a_x], axis=-1)


def odd_mixer(n_c, n_x, w_in, conv_w, q_norm_w, kv_norm_w, w_uq, w_ukv, row, col):
    pc = split_cols(n_c @ w_in, ODD_SPLITS)
    px = split_cols(n_x @ w_in, ODD_SPLITS)
    cv_c = short_conv(pc[0], pc[1], pc[2], conv_w)
    cv_x = short_conv(px[0], px[1], px[2], conv_w)
    m_c, m_x = mla_group(pc[3:], px[3:], q_norm_w, kv_norm_w, w_uq, w_ukv, row, col)
    return jnp.concatenate([cv_c, m_c], axis=-1), jnp.concatenate([cv_x, m_x], axis=-1)


def mlp(h, w1, w2):
    return jnp.square(jax.nn.relu(h @ w1)) @ w2


def setup_inputs(seed: int = 0) -> dict:
    key = jax.random.key(seed)
    ks = iter(jax.random.split(key, 32))
    nrm = lambda shape, s: jax.random.normal(next(ks), shape, jnp.float32) * s
    D = D_MODEL
    return {
        "x": nrm((BATCH, SEQ, D), 1.0),
        "c": nrm((BATCH, D), 1.0),
        "ctx": nrm((BATCH, CTX_LEN, D), 1.0),
        "c_ctx": nrm((D,), 1.0),
        "w_mod": nrm((DEPTH, D, 6 * D), 0.5 * D ** -0.5),
        "b_mod": nrm((DEPTH, 6 * D), 0.02),
        "norm1_w": 1.0 + nrm((DEPTH, D), 0.05),
        "norm2_w": 1.0 + nrm((DEPTH, D), 0.05),
        "mlp_w1": nrm((DEPTH, D, D_FF), D ** -0.5),
        "mlp_w2": nrm((DEPTH, D_FF, D), D_FF ** -0.5),
        "ev_w_in": nrm((N_EVEN, D, EVEN_IN), D ** -0.5),
        "ev_ret_log2_decay": -5.0 - jnp.arange(RET_HEADS, dtype=jnp.float32) + nrm((N_EVEN, 2, RET_HEADS), 0.1),
        "ev_ret_gn_w": 1.0 + nrm((N_EVEN, RET_HEADS * RET_DV), 0.05),
        "ev_sink": nrm((N_EVEN, WIN_HEADS), 1.0),
        "ev_w_out": nrm((N_EVEN, EVEN_MIX, D), EVEN_MIX ** -0.5),
        "od_w_in": nrm((N_ODD, D, ODD_IN), D ** -0.5),
        "od_conv_w": nrm((N_ODD, CONV_K, CONV_WIDTH), CONV_K ** -0.5),
        "od_q_norm_w": 1.0 + nrm((N_ODD, MLA_Q_RANK), 0.05),
        "od_kv_norm_w": 1.0 + nrm((N_ODD, MLA_KV_RANK), 0.05),
        "od_w_uq": nrm((N_ODD, MLA_Q_RANK, MLA_HEADS * (MLA_NOPE + MLA_ROPE)), MLA_Q_RANK ** -0.5),
        "od_w_ukv": nrm((N_ODD, MLA_KV_RANK, MLA_HEADS * (MLA_NOPE + MLA_V)), MLA_KV_RANK ** -0.5),
        "od_w_out": nrm((N_ODD, ODD_MIX, D), ODD_MIX ** -0.5),
        "norm_f": 1.0 + nrm((D,), 0.05),
    }


def reference(x, c, ctx, c_ctx, w_mod, b_mod, norm1_w, norm2_w, mlp_w1, mlp_w2,
              ev_w_in, ev_ret_log2_decay, ev_ret_gn_w, ev_sink, ev_w_out,
              od_w_in, od_conv_w, od_q_norm_w, od_kv_norm_w, od_w_uq, od_w_ukv, od_w_out,
              norm_f):
    tl = x.shape[1]
    ROWS = tl // GRID_W
    row = jnp.repeat(jnp.arange(ROWS, dtype=jnp.int32), GRID_W)
    col = jnp.tile(jnp.arange(GRID_W, dtype=jnp.int32), ROWS)
    pos = jnp.arange(tl, dtype=jnp.int32)
    h_c = ctx
    for layer in range(DEPTH):
        j = layer // 2
        sh1, sc1, g1, sh2, sc2, g2 = modulation(c, w_mod[layer], b_mod[layer])
        csh1, csc1, cg1, csh2, csc2, cg2 = modulation(c_ctx[None], w_mod[layer], b_mod[layer])
        n_x = modulate(rms_norm(x, norm1_w[layer]), sh1, sc1)
        n_c = modulate(rms_norm(h_c, norm1_w[layer]), csh1, csc1)
        if layer % 2 == 0:
            m_c, m_x = even_mixer(n_c, n_x, ev_w_in[j], ev_ret_log2_decay[j], ev_ret_gn_w[j], ev_sink[j],
                                  pos, row, col)
            w_out = ev_w_out[j]
        else:
            m_c, m_x = odd_mixer(n_c, n_x, od_w_in[j], od_conv_w[j], od_q_norm_w[j], od_kv_norm_w[j],
                                 od_w_uq[j], od_w_ukv[j], row, col)
            w_out = od_w_out[j]
        x = x + g1[:, None] * (m_x @ w_out)
        x = x + g2[:, None] * mlp(modulate(rms_norm(x, norm2_w[layer]), sh2, sc2), mlp_w1[layer], mlp_w2[layer])
        if layer < DEPTH - 1:
            h_c = h_c + cg1[:, None] * (m_c @ w_out)
            h_c = h_c + cg2[:, None] * mlp(modulate(rms_norm(h_c, norm2_w[layer]), csh2, csc2),
                                           mlp_w1[layer], mlp_w2[layer])
    return rms_norm(x, norm_f)
```

```python
import functools

import jax
import jax.numpy as jnp
from jax import lax
from jax.experimental import pallas as pl
from jax.experimental.pallas import tpu as pltpu

F32 = jnp.float32
BF16 = jnp.bfloat16

D_MODEL = 2048
BATCH = 2
SEQ = 4096
DEPTH = 4
GRID_W = 64
CTX_LEN = 256
HEAD_DIM = 128
ROPE_BASE = 10000.0
EPS = 1e-6
NEG_INF = -1e30

RET_HEADS = 8
RET_CHUNK = 128
WIN_HEADS = 8
WIN_KV_HEADS = 2
WIN_GROUP = WIN_HEADS // WIN_KV_HEADS
WIN_RADIUS = 128
WIN_BLOCK = 128
CONV_WIDTH = 1024
MLA_HEADS = 8
MLA_Q_RANK = 512
MLA_KV_RANK = 256
MLA_NOPE = 128
MLA_ROPE = 64
MLA_V = 128
D_FF = 4 * D_MODEL

EVEN_IN = 4 * RET_HEADS * HEAD_DIM + (WIN_HEADS + 2 * WIN_KV_HEADS) * HEAD_DIM
ODD_IN = 3 * CONV_WIDTH + MLA_Q_RANK + MLA_KV_RANK + MLA_ROPE
ODD_IN_PAD = 4096
MLA_QK_PAD = 256

TB = CTX_LEN + SEQ
M_ROWS = BATCH * TB
LANES = 128
ROW_TILE = 256
MM_TM = 512
VMEM_LIMIT = 60 * 1024 * 1024

NT_DIMS = (((1,), (1,)), ((), ()))
TN_DIMS = (((0,), (0,)), ((), ()))


def _params(sem, vmem=VMEM_LIMIT):
    return pltpu.CompilerParams(dimension_semantics=sem, vmem_limit_bytes=vmem)


def _rope_tables(pos_list, d_rot):
    per = d_rot // len(pos_list)
    half = per // 2
    inv = ROPE_BASE ** (-jnp.arange(0, per, 2, dtype=F32) / per)
    cos_parts, sin_parts = [], []
    for pos in pos_list:
        ang = pos.astype(F32)[:, None] * inv[None, :]
        c, s = jnp.cos(ang), jnp.sin(ang)
        cos_parts += [c, c]
        sin_parts += [-s, s]
    cos = jnp.concatenate(cos_parts, axis=-1)
    sin = jnp.concatenate(sin_parts, axis=-1)
    pad = LANES - d_rot
    if pad:
        cos = jnp.pad(cos, ((0, 0), (0, pad)))
        sin = jnp.pad(sin, ((0, 0), (0, pad)))
    ctx_cos = jnp.pad(jnp.ones((CTX_LEN, d_rot), F32), ((0, 0), (0, pad)))
    ctx_sin = jnp.zeros((CTX_LEN, LANES), F32)
    return jnp.concatenate([ctx_cos, cos], axis=0), jnp.concatenate([ctx_sin, sin], axis=0)


def _rotate(x, cos, sin, half):
    if 2 * half == LANES:
        partner = pltpu.roll(x, half, 1)
    else:
        lane = lax.broadcasted_iota(jnp.int32, (1, LANES), 1)
        first = (lane % (2 * half)) < half
        partner = jnp.where(first, pltpu.roll(x, LANES - half, 1), pltpu.roll(x, half, 1))
    return x * cos + partner * sin


def _modulation_kernel(c_ref, w_ref, b_ref, o_ref):
    s = jax.nn.silu(c_ref[...]).astype(BF16)
    o_ref[...] = jnp.dot(s, w_ref[...].astype(BF16), preferred_element_type=F32) + b_ref[...]


def _modulation(cvec, w_mod, b_mod):
    tn = 1536
    n = 6 * D_MODEL
    return pl.pallas_call(
        _modulation_kernel,
        grid=(DEPTH, n // tn),
        in_specs=[
            pl.BlockSpec((8, D_MODEL), lambda l, j: (0, 0)),
            pl.BlockSpec((None, D_MODEL, tn), lambda l, j: (l, 0, j)),
            pl.BlockSpec((None, 1, tn), lambda l, j: (l, 0, j)),
        ],
        out_specs=pl.BlockSpec((None, 8, tn), lambda l, j: (l, 0, j)),
        out_shape=jax.ShapeDtypeStruct((DEPTH, 8, n), F32),
        compiler_params=_params(("arbitrary", "arbitrary")),
        name="modulation",
    )(cvec, w_mod, b_mod.reshape(DEPTH, 1, n))


def _segment_of_tile(i):
    tiles = TB // ROW_TILE
    return jnp.where(i % tiles == 0, BATCH, i // tiles)


def _norm_mod_kernel(x_ref, w_ref, mod_ref, o_ref, *, shift_idx):
    x = x_ref[...]
    y = x * lax.rsqrt(jnp.mean(x * x, axis=-1, keepdims=True) + EPS) * w_ref[...]
    shift = mod_ref[shift_idx:shift_idx + 1, :]
    scale = mod_ref[shift_idx + 1:shift_idx + 2, :]
    o_ref[...] = (y * (1.0 + scale) + shift).astype(BF16)


def _norm_mod(h, norm_w, mod3, shift_idx):
    return pl.pallas_call(
        functools.partial(_norm_mod_kernel, shift_idx=shift_idx),
        grid=(M_ROWS // ROW_TILE,),
        in_specs=[
            pl.BlockSpec((ROW_TILE, D_MODEL), lambda i: (i, 0)),
            pl.BlockSpec((1, D_MODEL), lambda i: (0, 0)),
            pl.BlockSpec((None, 6, D_MODEL), lambda i: (_segment_of_tile(i), 0, 0)),
        ],
        out_specs=pl.BlockSpec((ROW_TILE, D_MODEL), lambda i: (i, 0)),
        out_shape=jax.ShapeDtypeStruct((M_ROWS, D_MODEL), BF16),
        compiler_params=_params(("arbitrary",)),
        name="norm_mod",
    )(h, norm_w.reshape(1, D_MODEL), mod3)


def _final_norm_kernel(x_ref, w_ref, o_ref):
    x = x_ref[...]
    o_ref[...] = x * lax.rsqrt(jnp.mean(x * x, axis=-1, keepdims=True) + EPS) * w_ref[...]


def _final_norm(h, norm_w):
    tiles = TB // ROW_TILE
    ctx_tiles = CTX_LEN // ROW_TILE
    return pl.pallas_call(
        _final_norm_kernel,
        grid=(BATCH, SEQ // ROW_TILE),
        in_specs=[
            pl.BlockSpec((ROW_TILE, D_MODEL), lambda b, t: (b * tiles + ctx_tiles + t, 0)),
            pl.BlockSpec((1, D_MODEL), lambda b, t: (0, 0)),
        ],
        out_specs=pl.BlockSpec((None, ROW_TILE, D_MODEL), lambda b, t: (b, t, 0)),
        out_shape=jax.ShapeDtypeStruct((BATCH, SEQ, D_MODEL), F32),
        compiler_params=_params(("arbitrary", "arbitrary")),
        name="final_norm",
    )(h, norm_w.reshape(1, D_MODEL))


def _mm_kernel(*refs, k_sizes, epilogue, norm, tm):
    n_a = len(k_sizes)
    a_refs = refs[:n_a]
    w_ref = refs[n_a]
    pos = n_a + 1
    if norm:
        nw_ref = refs[pos]
        pos += 1
    if epilogue == "gated_residual":
        res_ref, gate_ref = refs[pos], refs[pos + 1]
        pos += 2
    o_ref = refs[pos]

    acc = None
    off = 0
    for a_ref, ks in zip(a_refs, k_sizes):
        a = a_ref[...]
        if norm:
            af = a.astype(F32)
            af = af * lax.rsqrt(jnp.mean(af * af, axis=-1, keepdims=True) + EPS) * nw_ref[...]
            a = af.astype(BF16)
        part = jnp.dot(a, w_ref[off:off + ks, :], preferred_element_type=F32)
        acc = part if acc is None else acc + part
        off += ks

    if epilogue == "bf16":
        o_ref[...] = acc.astype(BF16)
    elif epilogue == "relu2":
        r = jnp.maximum(acc, 0.0)
        o_ref[...] = (r * r).astype(BF16)
    else:
        row = pl.program_id(0) * tm + lax.broadcasted_iota(jnp.int32, (tm, 1), 0)
        second = row >= TB
        is_ctx = (row - jnp.where(second, TB, 0)) < CTX_LEN
        gate = jnp.where(is_ctx, gate_ref[2:3, :], jnp.where(second, gate_ref[1:2, :], gate_ref[0:1, :]))
        o_ref[...] = res_ref[...] + gate * acc


def _matmul(a_list, w, *, tn, epilogue="bf16", a_col_blocks=None, k_sizes=None, norm_w=None,
            res=None, gates=None, gate_idx=0, name="matmul"):
    assert BATCH == 2
    kdim, n = w.shape
    if k_sizes is None:
        k_sizes = [a.shape[1] for a in a_list]
    if a_col_blocks is None:
        a_col_blocks = [0] * len(a_list)
    assert sum(k_sizes) == kdim and n % tn == 0
    tm = MM_TM
    in_specs = [pl.BlockSpec((tm, ks), functools.partial(lambda i, j, cb: (i, cb), cb=cb))
                for ks, cb in zip(k_sizes, a_col_blocks)]
    in_specs.append(pl.BlockSpec((kdim, tn), lambda i, j: (0, j)))
    operands = list(a_list) + [w]
    if norm_w is not None:
        assert len(a_list) == 1
        in_specs.append(pl.BlockSpec((1, kdim), lambda i, j: (0, 0)))
        operands.append(norm_w.reshape(1, kdim))
    if epilogue == "gated_residual":
        gate_blk = gate_idx * (n // tn)
        in_specs.append(pl.BlockSpec((tm, tn), lambda i, j: (i, j)))
        in_specs.append(pl.BlockSpec((8, tn), lambda i, j: (0, gate_blk + j)))
        operands += [res, gates]
        out_dtype = F32
    else:
        out_dtype = BF16
    return pl.pallas_call(
        functools.partial(_mm_kernel, k_sizes=tuple(k_sizes), epilogue=epilogue,
                          norm=norm_w is not None, tm=tm),
        grid=(M_ROWS // tm, n // tn),
        in_specs=in_specs,
        out_specs=pl.BlockSpec((tm, tn), lambda i, j: (i, j)),
        out_shape=jax.ShapeDtypeStruct((M_ROWS, n), out_dtype),
        compiler_params=_params(("arbitrary", "arbitrary")),
        name=name,
    )(*operands)


def _retention_kernel(q_ref, k_ref, v_ref, g_ref, cos_ref, sin_ref, ld_ref, gn_ref, o_ref, sb_ref):
    c = RET_CHUNK
    ctx_chunks = CTX_LEN // c
    lat_chunks = SEQ // c
    sc = HEAD_DIM ** -0.5
    lg = jnp.log1p(-jnp.exp2(ld_ref[...]))
    lgf = lg[0:1, 0:1]
    lgb = lg[1:2, 0:1]
    ii = lax.broadcasted_iota(jnp.int32, (c, 1), 0).astype(F32)
    jj = lax.broadcasted_iota(jnp.int32, (1, c), 1).astype(F32)
    diff = ii - jj
    dmat = jnp.where(diff >= 0, jnp.exp(lgf * jnp.maximum(diff, 0.0)), jnp.exp(lgb * jnp.maximum(-diff, 0.0)))
    qf_dec = jnp.exp(lgf * (ii + 1.0))
    qb_dec = jnp.exp(lgb * (c - ii))
    kf_dec = jnp.exp(lgf * (c - 1.0 - ii))
    kb_dec = jnp.exp(lgb * ii)
    gcf = jnp.exp(lgf * c)
    gcb = jnp.exp(lgb * c)

    def rows(n):
        return pl.ds(pl.multiple_of(n * c, c), c)

    def roped(ref, n):
        x = ref[rows(n), :].astype(F32)
        return _rotate(x, cos_ref[rows(n), :], sin_ref[rows(n), :], HEAD_DIM // 2)

    def backward_sweep(n0, cnt, s_init):
        def body(t, s):
            n = n0 + cnt - 1 - t
            sb_ref[n] = s.astype(BF16)
            kd = (roped(k_ref, n) * kb_dec).astype(BF16)
            return gcb * s + lax.dot_general(kd, v_ref[rows(n), :], TN_DIMS, preferred_element_type=F32)
        return lax.fori_loop(0, cnt, body, s_init)

    def forward_sweep(n0, cnt, s_init):
        def body(t, s):
            n = n0 + t
            q = roped(q_ref, n) * sc
            k = roped(k_ref, n)
            v = v_ref[rows(n), :]
            scores = lax.dot_general(q.astype(BF16), k.astype(BF16), NT_DIMS, preferred_element_type=F32) * dmat
            o = jnp.dot(scores.astype(BF16), v, preferred_element_type=F32)
            qd = jnp.concatenate([(q * qf_dec).astype(BF16), (q * qb_dec).astype(BF16)], axis=1)
            st = jnp.concatenate([s.astype(BF16), sb_ref[n]], axis=0)
            o = o + jnp.dot(qd, st, preferred_element_type=F32)
            mu = jnp.mean(o, axis=-1, keepdims=True)
            oc = o - mu
            var = jnp.mean(oc * oc, axis=-1, keepdims=True)
            on = oc * lax.rsqrt(var + EPS) * gn_ref[...]
            g = g_ref[rows(n), :].astype(F32)
            o_ref[rows(n), :] = (jax.nn.silu(g) * on).astype(BF16)
            kd = (k * kf_dec).astype(BF16)
            return gcf * s + lax.dot_general(kd, v, TN_DIMS, preferred_element_type=F32)
        return lax.fori_loop(0, cnt, body, s_init)

    zero = jnp.zeros((c, c), F32)
    s_b = backward_sweep(0, ctx_chunks, zero)
    s_f = forward_sweep(0, ctx_chunks, zero)
    backward_sweep(ctx_chunks, lat_chunks, s_b)
    forward_sweep(ctx_chunks, lat_chunks, s_f)


def _retention(proj, log2_decay, gn_w, cos, sin):
    h = RET_HEADS
    ld = jnp.broadcast_to(log2_decay.T[:, :, None], (h, 2, LANES))
    seq_spec = lambda cb0: pl.BlockSpec((TB, HEAD_DIM), functools.partial(lambda b, hh, cb0: (b, cb0 + hh), cb0=cb0))
    tab_spec = pl.BlockSpec((TB, LANES), lambda b, hh: (0, 0))
    return pl.pallas_call(
        _retention_kernel,
        grid=(BATCH, h),
        in_specs=[seq_spec(0), seq_spec(h), seq_spec(2 * h), seq_spec(3 * h), tab_spec, tab_spec,
                  pl.BlockSpec((None, 2, LANES), lambda b, hh: (hh, 0, 0)),
                  pl.BlockSpec((1, HEAD_DIM), lambda b, hh: (0, hh))],
        out_specs=pl.BlockSpec((TB, HEAD_DIM), lambda b, hh: (b, hh)),
        out_shape=jax.ShapeDtypeStruct((M_ROWS, h * HEAD_DIM), BF16),
        scratch_shapes=[pltpu.VMEM((TB // RET_CHUNK, RET_CHUNK, HEAD_DIM), BF16)],
        compiler_params=_params(("arbitrary", "arbitrary")),
        name="retention",
    )(proj, proj, proj, proj, cos, sin, ld, gn_w.reshape(1, h * HEAD_DIM))


def _window_kernel(sink_ref, q_ref, k_ref, v_ref, cos_ref, sin_ref, o_ref, kr_ref):
    blk = WIN_BLOCK
    ctx_chunks = CTX_LEN // blk
    n_chunks = TB // blk
    kvh = pl.program_id(1)
    n = pl.program_id(2)
    sc = HEAD_DIM ** -0.5
    half = HEAD_DIM // 4

    @pl.when(n == 0)
    def _():
        kr_ref[...] = _rotate(k_ref[...].astype(F32), cos_ref[...], sin_ref[...], half).astype(BF16)

    rq = pl.ds(pl.multiple_of(n * blk, blk), blk)
    cos = cos_ref[rq, :]
    sin = sin_ref[rq, :]
    q = jnp.concatenate(
        [(_rotate(q_ref[:, g * HEAD_DIM:(g + 1) * HEAD_DIM].astype(F32), cos, sin, half) * sc).astype(BF16)
         for g in range(WIN_GROUP)], axis=0)
    rows_q = WIN_GROUP * blk

    cstart = jnp.clip(n - 1, ctx_chunks, n_chunks - 3)
    rw = pl.ds(pl.multiple_of(cstart * blk, blk), 3 * blk)
    s_c = lax.dot_general(q, kr_ref[0:CTX_LEN, :], NT_DIMS, preferred_element_type=F32)
    s_w = lax.dot_general(q, kr_ref[rw, :], NT_DIMS, preferred_element_type=F32)
    qpos = (n - ctx_chunks) * blk + lax.broadcasted_iota(jnp.int32, (rows_q, 1), 0) % blk
    kpos = (cstart - ctx_chunks) * blk + lax.broadcasted_iota(jnp.int32, (1, 3 * blk), 1)
    valid = (jnp.abs(qpos - kpos) <= WIN_RADIUS) & (n >= ctx_chunks)
    s_w = jnp.where(valid, s_w, NEG_INF)

    head = lax.broadcasted_iota(jnp.int32, (rows_q, 1), 0) // blk
    sink = jnp.zeros((rows_q, 1), F32)
    for g in range(WIN_GROUP):
        sink = jnp.where(head == g, sink_ref[kvh * WIN_GROUP + g], sink)

    m = jnp.maximum(jnp.maximum(jnp.max(s_c, axis=-1, keepdims=True), jnp.max(s_w, axis=-1, keepdims=True)), sink)
    p_c = jnp.exp(s_c - m)
    p_w = jnp.exp(s_w - m)
    den = jnp.sum(p_c, axis=-1, keepdims=True) + jnp.sum(p_w, axis=-1, keepdims=True) + jnp.exp(sink - m)
    o = (jnp.dot(p_c.astype(BF16), v_ref[0:CTX_LEN, :], preferred_element_type=F32)
         + jnp.dot(p_w.astype(BF16), v_ref[rw, :], preferred_element_type=F32)) / den
    for g in range(WIN_GROUP):
        o_ref[:, g * HEAD_DIM:(g + 1) * HEAD_DIM] = o[g * blk:(g + 1) * blk, :].astype(BF16)


def _window_attention(proj, sink, cos, sin):
    blk = WIN_BLOCK
    n_chunks = TB // blk
    gw = WIN_GROUP * HEAD_DIM
    q_cb0 = 4 * RET_HEADS * HEAD_DIM // gw
    k_cb0 = (4 * RET_HEADS + WIN_HEADS) * HEAD_DIM // HEAD_DIM
    v_cb0 = k_cb0 + WIN_KV_HEADS
    tab_spec = pl.BlockSpec((TB, LANES), lambda b, kh, n: (0, 0))
    return pl.pallas_call(
        _window_kernel,
        grid=(BATCH, WIN_KV_HEADS, n_chunks),
        in_specs=[
            pl.BlockSpec(memory_space=pltpu.SMEM),
            pl.BlockSpec((blk, gw), lambda b, kh, n: (b * n_chunks + n, q_cb0 + kh)),
            pl.BlockSpec((TB, HEAD_DIM), lambda b, kh, n: (b, k_cb0 + kh)),
            pl.BlockSpec((TB, HEAD_DIM), lambda b, kh, n: (b, v_cb0 + kh)),
            tab_spec, tab_spec,
        ],
        out_specs=pl.BlockSpec((blk, gw), lambda b, kh, n: (b * n_chunks + n, kh)),
        out_shape=jax.ShapeDtypeStruct((M_ROWS, WIN_HEADS * HEAD_DIM), BF16),
        scratch_shapes=[pltpu.VMEM((TB, HEAD_DIM), BF16)],
        compiler_params=_params(("arbitrary", "arbitrary", "arbitrary")),
        name="window_attention",
    )(sink, proj, proj, proj, cos, sin)


def _conv_kernel(b_ref, c_ref, x_ref, w_ref, o_ref):
    u = c_ref[...].astype(F32) * x_ref[...].astype(F32)
    row = lax.broadcasted_iota(jnp.int32, (TB, 1), 0)
    prev = jnp.where((row == 0) | (row == CTX_LEN), 0.0, pltpu.roll(u, 1, 0))
    nxt = jnp.where((row == CTX_LEN - 1) | (row == TB - 1), 0.0, pltpu.roll(u, TB - 1, 0))
    z = prev * w_ref[0:1, :] + u * w_ref[1:2, :] + nxt * w_ref[2:3, :]
    o_ref[...] = (b_ref[...].astype(F32) * z).astype(BF16)


def _short_conv(proj, conv_w):
    tc = 256
    nblk = CONV_WIDTH // tc
    spec = lambda part: pl.BlockSpec((TB, tc), functools.partial(lambda b, j, part: (b, part * nblk + j), part=part))
    return pl.pallas_call(
        _conv_kernel,
        grid=(BATCH, nblk),
        in_specs=[spec(0), spec(1), spec(2), pl.BlockSpec((3, tc), lambda b, j: (0, j))],
        out_specs=pl.BlockSpec((TB, tc), lambda b, j: (b, j)),
        out_shape=jax.ShapeDtypeStruct((M_ROWS, CONV_WIDTH), BF16),
        compiler_params=_params(("arbitrary", "arbitrary")),
        name="short_conv",
    )(proj, proj, proj, conv_w)


def _mla_kernel(q_ref, kn_ref, v_ref, kr_ref, cos_ref, sin_ref, o_ref, kc_ref):
    t = pl.program_id(2)
    sc = (MLA_NOPE + MLA_ROPE) ** -0.5
    half = MLA_ROPE // 4

    @pl.when(t == 0)
    def _():
        kc_ref[:, 0:MLA_NOPE] = kn_ref[...]
        kc_ref[:, MLA_NOPE:MLA_QK_PAD] = _rotate(kr_ref[...].astype(F32), cos_ref[...], sin_ref[...], half).astype(BF16)

    rq = pl.ds(pl.multiple_of(t * ROW_TILE, ROW_TILE), ROW_TILE)
    qn = (q_ref[:, 0:MLA_NOPE].astype(F32) * sc).astype(BF16)
    qr = (_rotate(q_ref[:, MLA_NOPE:MLA_QK_PAD].astype(F32), cos_ref[rq, :], sin_ref[rq, :], half) * sc).astype(BF16)
    q = jnp.concatenate([qn, qr], axis=1)

    def attend(n_keys):
        s = lax.dot_general(q, kc_ref[0:n_keys, :], NT_DIMS, preferred_element_type=F32)
        p = jnp.exp(s - jnp.max(s, axis=-1, keepdims=True))
        den = jnp.sum(p, axis=-1, keepdims=True)
        o_ref[...] = (jnp.dot(p.astype(BF16), v_ref[0:n_keys, :], preferred_element_type=F32) / den).astype(BF16)

    @pl.when(t < CTX_LEN // ROW_TILE)
    def _():
        attend(CTX_LEN)

    @pl.when(t >= CTX_LEN // ROW_TILE)
    def _():
        attend(TB)


def _mla_attention(qcat, kv, proj, cos, sin):
    tiles = TB // ROW_TILE
    kr_cb = (3 * CONV_WIDTH + MLA_Q_RANK + MLA_KV_RANK) // LANES
    tab_spec = pl.BlockSpec((TB, LANES), lambda b, h, t: (0, 0))
    return pl.pallas_call(
        _mla_kernel,
        grid=(BATCH, MLA_HEADS, tiles),
        in_specs=[
            pl.BlockSpec((ROW_TILE, MLA_QK_PAD), lambda b, h, t: (b * tiles + t, h)),
            pl.BlockSpec((TB, MLA_NOPE), lambda b, h, t: (b, 2 * h)),
            pl.BlockSpec((TB, MLA_V), lambda b, h, t: (b, 2 * h + 1)),
            pl.BlockSpec((TB, LANES), lambda b, h, t: (b, kr_cb)),
            tab_spec, tab_spec,
        ],
        out_specs=pl.BlockSpec((ROW_TILE, MLA_V), lambda b, h, t: (b * tiles + t, h)),
        out_shape=jax.ShapeDtypeStruct((M_ROWS, MLA_HEADS * MLA_V), BF16),
        scratch_shapes=[pltpu.VMEM((TB, MLA_QK_PAD), BF16)],
        compiler_params=_params(("arbitrary", "arbitrary", "arbitrary")),
        name="mla_attention",
    )(qcat, kv, kv, proj, cos, sin)


def kernel(x, c, ctx, c_ctx, w_mod, b_mod, norm1_w, norm2_w, mlp_w1, mlp_w2, ev_w_in, ev_ret_log2_decay,
           ev_ret_gn_w, ev_sink, ev_w_out, od_w_in, od_conv_w, od_q_norm_w, od_kv_norm_w, od_w_uq, od_w_ukv,
           od_w_out, norm_f):
    assert x.shape == (BATCH, SEQ, D_MODEL) and ctx.shape == (BATCH, CTX_LEN, D_MODEL)
    rows = SEQ // GRID_W
    row = jnp.repeat(jnp.arange(rows, dtype=jnp.int32), GRID_W)
    col = jnp.tile(jnp.arange(GRID_W, dtype=jnp.int32), rows)
    pos = jnp.arange(SEQ, dtype=jnp.int32)
    ret_cos, ret_sin = _rope_tables([pos], HEAD_DIM)
    win_cos, win_sin = _rope_tables([row, col], HEAD_DIM)
    mla_cos, mla_sin = _rope_tables([row, col], MLA_ROPE)

    cvec = jnp.concatenate([c, c_ctx[None], jnp.zeros((8 - BATCH - 1, D_MODEL), F32)], axis=0)
    mod_all = _modulation(cvec, w_mod, b_mod)

    h = jnp.concatenate([ctx, x], axis=1).reshape(M_ROWS, D_MODEL)

    for layer in range(DEPTH):
        j = layer // 2
        mod2 = mod_all[layer]
        mod3 = mod2.reshape(8, 6, D_MODEL)
        n1 = _norm_mod(h, norm1_w[layer], mod3, 0)
        if layer % 2 == 0:
            proj = _matmul([n1], ev_w_in[j].astype(BF16), tn=EVEN_IN // 4, name="even_in_proj")
            mix_a = _retention(proj, ev_ret_log2_decay[j], ev_ret_gn_w[j], ret_cos, ret_sin)
            mix_b = _window_attention(proj, ev_sink[j], win_cos, win_sin)
            w_out = ev_w_out[j]
        else:
            w_in = jnp.pad(od_w_in[j].astype(BF16), ((0, 0), (0, ODD_IN_PAD - ODD_IN)))
            proj = _matmul([n1], w_in, tn=ODD_IN_PAD // 2, name="odd_in_proj")
            mix_a = _short_conv(proj, od_conv_w[j])
            w_uq = jnp.pad(od_w_uq[j].astype(BF16).reshape(MLA_Q_RANK, MLA_HEADS, MLA_NOPE + MLA_ROPE),
                           ((0, 0), (0, 0), (0, MLA_QK_PAD - MLA_NOPE - MLA_ROPE))
                           ).reshape(MLA_Q_RANK, MLA_HEADS * MLA_QK_PAD)
            qcat = _matmul([proj], w_uq, tn=MLA_HEADS * MLA_QK_PAD, k_sizes=[MLA_Q_RANK],
                           a_col_blocks=[3 * CONV_WIDTH // MLA_Q_RANK], norm_w=od_q_norm_w[j], name="mla_q_up")
            kv = _matmul([proj], od_w_ukv[j].astype(BF16), tn=MLA_HEADS * (MLA_NOPE + MLA_V),
                         k_sizes=[MLA_KV_RANK], a_col_blocks=[(3 * CONV_WIDTH + MLA_Q_RANK) // MLA_KV_RANK],
                         norm_w=od_kv_norm_w[j], name="mla_kv_up")
            mix_b = _mla_attention(qcat, kv, proj, mla_cos, mla_sin)
            w_out = od_w_out[j]
        h = _matmul([mix_a, mix_b], w_out.astype(BF16), tn=D_MODEL, epilogue="gated_residual",
                    res=h, gates=mod2, gate_idx=2, name="mixer_out_proj")
        n2 = _norm_mod(h, norm2_w[layer], mod3, 3)
        ff = _matmul([n2], mlp_w1[layer].astype(BF16), tn=2048, epilogue="relu2", name="mlp_up")
        h = _matmul([ff], mlp_w2[layer].astype(BF16), tn=512, epilogue="gated_residual",
                    res=h, gates=mod2, gate_idx=5, name="mlp_down")
    return _final_norm(h, norm_f)
```

```python
import functools

import jax
import jax.numpy as jnp
from jax import lax
from jax.experimental import pallas as pl
from jax.experimental.pallas import tpu as pltpu

F32 = jnp.float32
BF16 = jnp.bfloat16

D_MODEL = 2048
BATCH = 2
SEQ = 4096
DEPTH = 4
GRID_W = 64
CTX_LEN = 256
HEAD_DIM = 128
ROPE_BASE = 10000.0
EPS = 1e-6
NEG_INF = -1e30

RET_HEADS = 8
RET_CHUNK = 128
RET_UNROLL = 8
WIN_HEADS = 8
WIN_KV_HEADS = 2
WIN_GROUP = WIN_HEADS // WIN_KV_HEADS
WIN_RADIUS = 128
WIN_BLOCK = 128
CONV_WIDTH = 1024
MLA_HEADS = 8
MLA_Q_RANK = 512
MLA_KV_RANK = 256
MLA_NOPE = 128
MLA_ROPE = 64
MLA_V = 128
D_FF = 4 * D_MODEL

EVEN_IN = 4 * RET_HEADS * HEAD_DIM + (WIN_HEADS + 2 * WIN_KV_HEADS) * HEAD_DIM
ODD_IN = 3 * CONV_WIDTH + MLA_Q_RANK + MLA_KV_RANK + MLA_ROPE
ODD_IN_PAD = 4096
MLA_QK_PAD = 256

TB = CTX_LEN + SEQ
M_ROWS = BATCH * TB
LANES = 128
ROW_TILE = 256
MM_TM = 512
MM_TM_BIG = M_ROWS // 8
MM_TM_SMALL = M_ROWS // 16
VMEM_LIMIT = 60 * 1024 * 1024

NT_DIMS = (((1,), (1,)), ((), ()))
TN_DIMS = (((0,), (0,)), ((), ()))


def _params(sem, vmem=VMEM_LIMIT):
    return pltpu.CompilerParams(dimension_semantics=sem, vmem_limit_bytes=vmem)


def _rope_tables(pos_list, d_rot):
    per = d_rot // len(pos_list)
    half = per // 2
    inv = ROPE_BASE ** (-jnp.arange(0, per, 2, dtype=F32) / per)
    cos_parts, sin_parts = [], []
    for pos in pos_list:
        ang = pos.astype(F32)[:, None] * inv[None, :]
        c, s = jnp.cos(ang), jnp.sin(ang)
        cos_parts += [c, c]
        sin_parts += [-s, s]
    cos = jnp.concatenate(cos_parts, axis=-1)
    sin = jnp.concatenate(sin_parts, axis=-1)
    pad = LANES - d_rot
    if pad:
        cos = jnp.pad(cos, ((0, 0), (0, pad)))
        sin = jnp.pad(sin, ((0, 0), (0, pad)))
    ctx_cos = jnp.pad(jnp.ones((CTX_LEN, d_rot), F32), ((0, 0), (0, pad)))
    ctx_sin = jnp.zeros((CTX_LEN, LANES), F32)
    return jnp.concatenate([ctx_cos, cos], axis=0), jnp.concatenate([ctx_sin, sin], axis=0)


def _rotate(x, cos, sin, half):
    if 2 * half == LANES:
        partner = pltpu.roll(x, half, 1)
    else:
        lane = lax.broadcasted_iota(jnp.int32, (1, LANES), 1)
        first = (lane % (2 * half)) < half
        partner = jnp.where(first, pltpu.roll(x, LANES - half, 1), pltpu.roll(x, half, 1))
    return x * cos + partner * sin


def _modulation_kernel(c_ref, w_ref, b_ref, o_ref):
    s = jax.nn.silu(c_ref[...]).astype(BF16)
    o_ref[...] = jnp.dot(s, w_ref[...].astype(BF16), preferred_element_type=F32) + b_ref[...]


def _modulation(cvec, w_mod, b_mod):
    tn = 1536
    n = 6 * D_MODEL
    return pl.pallas_call(
        _modulation_kernel,
        grid=(DEPTH, n // tn),
        in_specs=[
            pl.BlockSpec((8, D_MODEL), lambda l, j: (0, 0)),
            pl.BlockSpec((None, D_MODEL, tn), lambda l, j: (l, 0, j)),
            pl.BlockSpec((None, 1, tn), lambda l, j: (l, 0, j)),
        ],
        out_specs=pl.BlockSpec((None, 8, tn), lambda l, j: (l, 0, j)),
        out_shape=jax.ShapeDtypeStruct((DEPTH, 8, n), F32),
        compiler_params=_params(("arbitrary", "arbitrary")),
        name="modulation",
    )(cvec, w_mod, b_mod.reshape(DEPTH, 1, n))


def _segment_of_tile(i):
    tiles = TB // ROW_TILE
    return jnp.where(i % tiles == 0, BATCH, i // tiles)


def _norm_mod_kernel(x_ref, w_ref, mod_ref, o_ref, *, shift_idx):
    x = x_ref[...]
    y = x * lax.rsqrt(jnp.mean(x * x, axis=-1, keepdims=True) + EPS) * w_ref[...]
    shift = mod_ref[shift_idx:shift_idx + 1, :]
    scale = mod_ref[shift_idx + 1:shift_idx + 2, :]
    o_ref[...] = (y * (1.0 + scale) + shift).astype(BF16)


def _norm_mod(h, norm_w, mod3, shift_idx):
    return pl.pallas_call(
        functools.partial(_norm_mod_kernel, shift_idx=shift_idx),
        grid=(M_ROWS // ROW_TILE,),
        in_specs=[
            pl.BlockSpec((ROW_TILE, D_MODEL), lambda i: (i, 0)),
            pl.BlockSpec((1, D_MODEL), lambda i: (0, 0)),
            pl.BlockSpec((None, 6, D_MODEL), lambda i: (_segment_of_tile(i), 0, 0)),
        ],
        out_specs=pl.BlockSpec((ROW_TILE, D_MODEL), lambda i: (i, 0)),
        out_shape=jax.ShapeDtypeStruct((M_ROWS, D_MODEL), BF16),
        compiler_params=_params(("arbitrary",)),
        name="norm_mod",
    )(h, norm_w.reshape(1, D_MODEL), mod3)


def _final_norm_kernel(x_ref, w_ref, o_ref):
    x = x_ref[...]
    o_ref[...] = x * lax.rsqrt(jnp.mean(x * x, axis=-1, keepdims=True) + EPS) * w_ref[...]


def _final_norm(h, norm_w):
    tiles = TB // ROW_TILE
    ctx_tiles = CTX_LEN // ROW_TILE
    return pl.pallas_call(
        _final_norm_kernel,
        grid=(BATCH, SEQ // ROW_TILE),
        in_specs=[
            pl.BlockSpec((ROW_TILE, D_MODEL), lambda b, t: (b * tiles + ctx_tiles + t, 0)),
            pl.BlockSpec((1, D_MODEL), lambda b, t: (0, 0)),
        ],
        out_specs=pl.BlockSpec((None, ROW_TILE, D_MODEL), lambda b, t: (b, t, 0)),
        out_shape=jax.ShapeDtypeStruct((BATCH, SEQ, D_MODEL), F32),
        compiler_params=_params(("arbitrary", "arbitrary")),
        name="final_norm",
    )(h, norm_w.reshape(1, D_MODEL))


def _mm_kernel(*refs, k_sizes, epilogue, norm, tm):
    n_a = len(k_sizes)
    a_refs = refs[:n_a]
    w_ref = refs[n_a]
    pos = n_a + 1
    if norm:
        nw_ref = refs[pos]
        pos += 1
    if epilogue == "gated_residual":
        res_ref, gate_ref = refs[pos], refs[pos + 1]
        pos += 2
    o_ref = refs[pos]

    acc = None
    off = 0
    for a_ref, ks in zip(a_refs, k_sizes):
        a = a_ref[...]
        if norm:
            af = a.astype(F32)
            af = af * lax.rsqrt(jnp.mean(af * af, axis=-1, keepdims=True) + EPS) * nw_ref[...]
            a = af.astype(BF16)
        part = jnp.dot(a, w_ref[off:off + ks, :], preferred_element_type=F32)
        acc = part if acc is None else acc + part
        off += ks

    if epilogue == "bf16":
        o_ref[...] = acc.astype(BF16)
    elif epilogue == "relu2":
        r = jnp.maximum(acc, 0.0)
        o_ref[...] = (r * r).astype(BF16)
    else:
        row = pl.program_id(0) * tm + lax.broadcasted_iota(jnp.int32, (tm, 1), 0)
        second = row >= TB
        is_ctx = (row - jnp.where(second, TB, 0)) < CTX_LEN
        gate = jnp.where(is_ctx, gate_ref[2:3, :], jnp.where(second, gate_ref[1:2, :], gate_ref[0:1, :]))
        o_ref[...] = res_ref[...] + gate * acc


def _matmul(a_list, w, *, tn, epilogue="bf16", a_col_blocks=None, k_sizes=None, norm_w=None,
            res=None, gates=None, gate_idx=0, name="matmul"):
    assert BATCH == 2
    kdim, n = w.shape
    if k_sizes is None:
        k_sizes = [a.shape[1] for a in a_list]
    if a_col_blocks is None:
        a_col_blocks = [0] * len(a_list)
    assert sum(k_sizes) == kdim and n % tn == 0
    tm = MM_TM
    in_specs = [pl.BlockSpec((tm, ks), functools.partial(lambda i, j, cb: (i, cb), cb=cb))
                for ks, cb in zip(k_sizes, a_col_blocks)]
    in_specs.append(pl.BlockSpec((kdim, tn), lambda i, j: (0, j)))
    operands = list(a_list) + [w]
    if norm_w is not None:
        assert len(a_list) == 1
        in_specs.append(pl.BlockSpec((1, kdim), lambda i, j: (0, 0)))
        operands.append(norm_w.reshape(1, kdim))
    if epilogue == "gated_residual":
        gate_blk = gate_idx * (n // tn)
        in_specs.append(pl.BlockSpec((tm, tn), lambda i, j: (i, j)))
        in_specs.append(pl.BlockSpec((8, tn), lambda i, j: (0, gate_blk + j)))
        operands += [res, gates]
        out_dtype = F32
    else:
        out_dtype = BF16
    return pl.pallas_call(
        functools.partial(_mm_kernel, k_sizes=tuple(k_sizes), epilogue=epilogue,
                          norm=norm_w is not None, tm=tm),
        grid=(M_ROWS // tm, n // tn),
        in_specs=in_specs,
        out_specs=pl.BlockSpec((tm, tn), lambda i, j: (i, j)),
        out_shape=jax.ShapeDtypeStruct((M_ROWS, n), out_dtype),
        compiler_params=_params(("arbitrary", "arbitrary")),
        name=name,
    )(*operands)


def _row_select(tab_ref, row0, tm):
    row = row0 + lax.broadcasted_iota(jnp.int32, (tm, 1), 0)
    second = row >= TB
    is_ctx = (row - jnp.where(second, TB, 0)) < CTX_LEN
    return jnp.where(is_ctx, tab_ref[2:3, :], jnp.where(second, tab_ref[1:2, :], tab_ref[0:1, :]))


def _mm_ws_kernel(*refs, k_sizes, epilogue, norm, tm, valid_cols):
    n_a = len(k_sizes)
    a_refs = refs[:n_a]
    w_ref = refs[n_a]
    pos = n_a + 1
    if norm:
        nw_ref = refs[pos]
        pos += 1
    if epilogue in ("gated_residual", "gated_residual_norm"):
        res_ref, gate_ref = refs[pos], refs[pos + 1]
        pos += 2
    if epilogue == "gated_residual_norm":
        n2w_ref, shift_ref, scale_ref = refs[pos:pos + 3]
        pos += 3
    o_ref = refs[pos]
    pos += 1
    if epilogue == "gated_residual_norm":
        n_ref = refs[pos]
        pos += 1
    wb_ref = refs[pos]
    i = pl.program_id(1)

    @pl.when(i == 0)
    def _():
        w = w_ref[...]
        if valid_cols is not None:
            tn = w.shape[1]
            col = pl.program_id(0) * tn + lax.broadcasted_iota(jnp.int32, (1, tn), 1)
            w = jnp.where(col < valid_cols, w, 0.0)
        wb_ref[...] = w.astype(BF16)

    acc = None
    off = 0
    for a_ref, ks in zip(a_refs, k_sizes):
        a = a_ref[...]
        if norm:
            af = a.astype(F32)
            af = af * lax.rsqrt(jnp.mean(af * af, axis=-1, keepdims=True) + EPS) * nw_ref[...]
            a = af.astype(BF16)
        part = jnp.dot(a, wb_ref[off:off + ks, :], preferred_element_type=F32)
        acc = part if acc is None else acc + part
        off += ks

    if epilogue == "bf16":
        o_ref[...] = acc.astype(BF16)
    elif epilogue == "relu2":
        r = jnp.maximum(acc, 0.0)
        o_ref[...] = (r * r).astype(BF16)
    else:
        h = res_ref[...] + _row_select(gate_ref, i * tm, tm) * acc
        o_ref[...] = h
        if epilogue == "gated_residual_norm":
            y = h * lax.rsqrt(jnp.mean(h * h, axis=-1, keepdims=True) + EPS) * n2w_ref[...]
            n_ref[...] = (y * (1.0 + _row_select(scale_ref, i * tm, tm))
                          + _row_select(shift_ref, i * tm, tm)).astype(BF16)


def _matmul_ws(a_list, w, *, tn, tm, epilogue="bf16", a_col_blocks=None, k_sizes=None, norm_w=None,
               res=None, mod=None, gate_idx=0, next_norm_w=None, n_out=None, name="matmul"):
    assert BATCH == 2
    kdim, n_w = w.shape
    n = n_w if n_out is None else n_out
    if k_sizes is None:
        k_sizes = [a.shape[1] for a in a_list]
    if a_col_blocks is None:
        a_col_blocks = [0] * len(a_list)
    assert sum(k_sizes) == kdim and n % tn == 0 and M_ROWS % tm == 0
    gated = epilogue in ("gated_residual", "gated_residual_norm")
    in_specs = [pl.BlockSpec((tm, ks), functools.partial(lambda j, i, cb: (i, cb), cb=cb))
                for ks, cb in zip(k_sizes, a_col_blocks)]
    single_w = n == tn
    w_mode = dict(pipeline_mode=pl.Buffered(1)) if single_w else {}
    in_specs.append(pl.BlockSpec((kdim, tn), lambda j, i: (0, j), **w_mode))
    operands = list(a_list) + [w]
    if norm_w is not None:
        assert len(a_list) == 1
        in_specs.append(pl.BlockSpec((1, kdim), lambda j, i: (0, 0)))
        operands.append(norm_w.reshape(1, kdim))
    out_specs = pl.BlockSpec((tm, tn), lambda j, i: (i, j))
    out_shape = jax.ShapeDtypeStruct((M_ROWS, n), F32 if gated else BF16)
    if gated:
        blocks_per_vec = D_MODEL // tn
        in_specs.append(pl.BlockSpec((tm, tn), lambda j, i: (i, j)))
        in_specs.append(pl.BlockSpec((8, tn), lambda j, i: (0, gate_idx * blocks_per_vec + j)))
        operands += [res, mod]
    if epilogue == "gated_residual_norm":
        assert tn == n == D_MODEL
        in_specs.append(pl.BlockSpec((1, D_MODEL), lambda j, i: (0, 0)))
        in_specs.append(pl.BlockSpec((8, D_MODEL), lambda j, i: (0, gate_idx + 1)))
        in_specs.append(pl.BlockSpec((8, D_MODEL), lambda j, i: (0, gate_idx + 2)))
        operands += [next_norm_w.reshape(1, D_MODEL), mod, mod]
        out_specs = [out_specs, pl.BlockSpec((tm, tn), lambda j, i: (i, j))]
        out_shape = [out_shape, jax.ShapeDtypeStruct((M_ROWS, n), BF16)]
    return pl.pallas_call(
        functools.partial(_mm_ws_kernel, k_sizes=tuple(k_sizes), epilogue=epilogue,
                          norm=norm_w is not None, tm=tm, valid_cols=None if n == n_w else n_w),
        grid=(n // tn, M_ROWS // tm),
        in_specs=in_specs,
        out_specs=out_specs,
        out_shape=out_shape,
        scratch_shapes=[pltpu.VMEM((kdim, tn), BF16)],
        compiler_params=_params(("arbitrary", "arbitrary")),
        name=name,
    )(*operands)


def _retention_kernel(q_ref, k_ref, v_ref, g_ref, cos_ref, sin_ref, ld_ref, gn_ref, o_ref, kb_ref, kv_ref, st_ref):
    c = RET_CHUNK
    ctx_chunks = CTX_LEN // c
    lat_chunks = SEQ // c
    n_chunks = ctx_chunks + lat_chunks
    sc = HEAD_DIM ** -0.5
    lg = jnp.log1p(-jnp.exp2(ld_ref[...]))
    lgf = lg[0:1, 0:1]
    lgb = lg[1:2, 0:1]
    ii = lax.broadcasted_iota(jnp.int32, (c, 1), 0).astype(F32)
    jj = lax.broadcasted_iota(jnp.int32, (1, c), 1).astype(F32)
    diff = ii - jj
    dmat = jnp.where(diff >= 0, jnp.exp(lgf * jnp.maximum(diff, 0.0)), jnp.exp(lgb * jnp.maximum(-diff, 0.0)))
    qf_dec = jnp.exp(lgf * (ii + 1.0))
    qb_dec = jnp.exp(lgb * (c - ii))
    kf_dec = jnp.exp(lgf * (c - 1.0 - ii))
    kb_dec = jnp.exp(lgb * ii)
    gcf = jnp.exp(lgf * c)
    gcb = jnp.exp(lgb * c)

    def rows(n):
        return pl.ds(pl.multiple_of(n * c, c), c)

    def roped(ref, n):
        x = ref[rows(n), :].astype(F32)
        return _rotate(x, cos_ref[rows(n), :], sin_ref[rows(n), :], HEAD_DIM // 2)

    def increments(n, carry):
        k = roped(k_ref, n)
        kb_ref[rows(n), :] = k.astype(BF16)
        kd = jnp.concatenate([(k * kf_dec).astype(BF16), (k * kb_dec).astype(BF16)], axis=1)
        kv_ref[n] = lax.dot_general(kd, v_ref[rows(n), :], TN_DIMS, preferred_element_type=F32)
        return carry

    def recurrence(n0, cnt, s_f, s_b):
        def body(t, carry):
            s_f, s_b = carry
            nf = n0 + t
            nb = n0 + cnt - 1 - t
            st_ref[nf, 0:c, :] = s_f.astype(BF16)
            st_ref[nb, c:2 * c, :] = s_b.astype(BF16)
            return gcf * s_f + kv_ref[nf, 0:c, :], gcb * s_b + kv_ref[nb, c:2 * c, :]
        return lax.fori_loop(0, cnt, body, (s_f, s_b))

    def outputs(n, carry):
        q = roped(q_ref, n) * sc
        v = v_ref[rows(n), :]
        scores = lax.dot_general(q.astype(BF16), kb_ref[rows(n), :], NT_DIMS, preferred_element_type=F32) * dmat
        qd = jnp.concatenate([(q * qf_dec).astype(BF16), (q * qb_dec).astype(BF16)], axis=1)
        o = (jnp.dot(scores.astype(BF16), v, preferred_element_type=F32)
             + jnp.dot(qd, st_ref[n], preferred_element_type=F32))
        mu = jnp.mean(o, axis=-1, keepdims=True)
        oc = o - mu
        var = jnp.mean(oc * oc, axis=-1, keepdims=True)
        on = oc * lax.rsqrt(var + EPS) * gn_ref[...]
        g = g_ref[rows(n), :].astype(F32)
        o_ref[rows(n), :] = (jax.nn.silu(g) * on).astype(BF16)
        return carry

    lax.fori_loop(0, n_chunks, increments, 0, unroll=RET_UNROLL)
    zero = jnp.zeros((c, c), F32)
    s_f, s_b = recurrence(0, ctx_chunks, zero, zero)
    recurrence(ctx_chunks, lat_chunks, s_f, s_b)
    lax.fori_loop(0, n_chunks, outputs, 0, unroll=RET_UNROLL)


def _retention(proj, log2_decay, gn_w, cos, sin):
    h = RET_HEADS
    ld = jnp.broadcast_to(log2_decay.T[:, :, None], (h, 2, LANES))
    seq_spec = lambda cb0: pl.BlockSpec((TB, HEAD_DIM), functools.partial(lambda b, hh, cb0: (b, cb0 + hh), cb0=cb0))
    tab_spec = pl.BlockSpec((TB, LANES), lambda b, hh: (0, 0))
    return pl.pallas_call(
        _retention_kernel,
        grid=(BATCH, h),
        in_specs=[seq_spec(0), seq_spec(h), seq_spec(2 * h), seq_spec(3 * h), tab_spec, tab_spec,
                  pl.BlockSpec((None, 2, LANES), lambda b, hh: (hh, 0, 0)),
                  pl.BlockSpec((1, HEAD_DIM), lambda b, hh: (0, hh))],
        out_specs=pl.BlockSpec((TB, HEAD_DIM), lambda b, hh: (b, hh)),
        out_shape=jax.ShapeDtypeStruct((M_ROWS, h * HEAD_DIM), BF16),
        scratch_shapes=[pltpu.VMEM((TB, HEAD_DIM), BF16),
                        pltpu.VMEM((TB // RET_CHUNK, 2 * RET_CHUNK, HEAD_DIM), F32),
                        pltpu.VMEM((TB // RET_CHUNK, 2 * RET_CHUNK, HEAD_DIM), BF16)],
        compiler_params=_params(("arbitrary", "arbitrary")),
        name="retention",
    )(proj, proj, proj, proj, cos, sin, ld, gn_w.reshape(1, h * HEAD_DIM))


def _window_kernel(sink_ref, q_ref, k_ref, v_ref, cos_ref, sin_ref, o_ref, kr_ref):
    blk = WIN_BLOCK
    ctx_chunks = CTX_LEN // blk
    n_chunks = TB // blk
    kvh = pl.program_id(1)
    n = pl.program_id(2)
    sc = HEAD_DIM ** -0.5
    half = HEAD_DIM // 4

    @pl.when(n == 0)
    def _():
        kr_ref[...] = _rotate(k_ref[...].astype(F32), cos_ref[...], sin_ref[...], half).astype(BF16)

    rq = pl.ds(pl.multiple_of(n * blk, blk), blk)
    cos = cos_ref[rq, :]
    sin = sin_ref[rq, :]
    q = jnp.concatenate(
        [(_rotate(q_ref[:, g * HEAD_DIM:(g + 1) * HEAD_DIM].astype(F32), cos, sin, half) * sc).astype(BF16)
         for g in range(WIN_GROUP)], axis=0)
    rows_q = WIN_GROUP * blk

    cstart = jnp.clip(n - 1, ctx_chunks, n_chunks - 3)
    rw = pl.ds(pl.multiple_of(cstart * blk, blk), 3 * blk)
    s_c = lax.dot_general(q, kr_ref[0:CTX_LEN, :], NT_DIMS, preferred_element_type=F32)
    s_w = lax.dot_general(q, kr_ref[rw, :], NT_DIMS, preferred_element_type=F32)
    qpos = (n - ctx_chunks) * blk + lax.broadcasted_iota(jnp.int32, (rows_q, 1), 0) % blk
    kpos = (cstart - ctx_chunks) * blk + lax.broadcasted_iota(jnp.int32, (1, 3 * blk), 1)
    valid = (jnp.abs(qpos - kpos) <= WIN_RADIUS) & (n >= ctx_chunks)
    s_w = jnp.where(valid, s_w, NEG_INF)

    head = lax.broadcasted_iota(jnp.int32, (rows_q, 1), 0) // blk
    sink = jnp.zeros((rows_q, 1), F32)
    for g in range(WIN_GROUP):
        sink = jnp.where(head == g, sink_ref[kvh * WIN_GROUP + g], sink)

    m = jnp.maximum(jnp.maximum(jnp.max(s_c, axis=-1, keepdims=True), jnp.max(s_w, axis=-1, keepdims=True)), sink)
    p_c = jnp.exp(s_c - m)
    p_w = jnp.exp(s_w - m)
    den = jnp.sum(p_c, axis=-1, keepdims=True) + jnp.sum(p_w, axis=-1, keepdims=True) + jnp.exp(sink - m)
    o = (jnp.dot(p_c.astype(BF16), v_ref[0:CTX_LEN, :], preferred_element_type=F32)
         + jnp.dot(p_w.astype(BF16), v_ref[rw, :], preferred_element_type=F32)) / den
    for g in range(WIN_GROUP):
        o_ref[:, g * HEAD_DIM:(g + 1) * HEAD_DIM] = o[g * blk:(g + 1) * blk, :].astype(BF16)


def _window_attention(proj, sink, cos, sin):
    blk = WIN_BLOCK
    n_chunks = TB // blk
    gw = WIN_GROUP * HEAD_DIM
    q_cb0 = 4 * RET_HEADS * HEAD_DIM // gw
    k_cb0 = (4 * RET_HEADS + WIN_HEADS) * HEAD_DIM // HEAD_DIM
    v_cb0 = k_cb0 + WIN_KV_HEADS
    tab_spec = pl.BlockSpec((TB, LANES), lambda b, kh, n: (0, 0))
    return pl.pallas_call(
        _window_kernel,
        grid=(BATCH, WIN_KV_HEADS, n_chunks),
        in_specs=[
            pl.BlockSpec(memory_space=pltpu.SMEM),
            pl.BlockSpec((blk, gw), lambda b, kh, n: (b * n_chunks + n, q_cb0 + kh)),
            pl.BlockSpec((TB, HEAD_DIM), lambda b, kh, n: (b, k_cb0 + kh)),
            pl.BlockSpec((TB, HEAD_DIM), lambda b, kh, n: (b, v_cb0 + kh)),
            tab_spec, tab_spec,
        ],
        out_specs=pl.BlockSpec((blk, gw), lambda b, kh, n: (b * n_chunks + n, kh)),
        out_shape=jax.ShapeDtypeStruct((M_ROWS, WIN_HEADS * HEAD_DIM), BF16),
        scratch_shapes=[pltpu.VMEM((TB, HEAD_DIM), BF16)],
        compiler_params=_params(("arbitrary", "arbitrary", "arbitrary")),
        name="window_attention",
    )(sink, proj, proj, proj, cos, sin)


def _conv_kernel(b_ref, c_ref, x_ref, w_ref, o_ref):
    u = c_ref[...].astype(F32) * x_ref[...].astype(F32)
    row = lax.broadcasted_iota(jnp.int32, (TB, 1), 0)
    prev = jnp.where((row == 0) | (row == CTX_LEN), 0.0, pltpu.roll(u, 1, 0))
    nxt = jnp.where((row == CTX_LEN - 1) | (row == TB - 1), 0.0, pltpu.roll(u, TB - 1, 0))
    z = prev * w_ref[0:1, :] + u * w_ref[1:2, :] + nxt * w_ref[2:3, :]
    o_ref[...] = (b_ref[...].astype(F32) * z).astype(BF16)


def _short_conv(proj, conv_w):
    tc = 256
    nblk = CONV_WIDTH // tc
    spec = lambda part: pl.BlockSpec((TB, tc), functools.partial(lambda b, j, part: (b, part * nblk + j), part=part))
    return pl.pallas_call(
        _conv_kernel,
        grid=(BATCH, nblk),
        in_specs=[spec(0), spec(1), spec(2), pl.BlockSpec((3, tc), lambda b, j: (0, j))],
        out_specs=pl.BlockSpec((TB, tc), lambda b, j: (b, j)),
        out_shape=jax.ShapeDtypeStruct((M_ROWS, CONV_WIDTH), BF16),
        compiler_params=_params(("arbitrary", "arbitrary")),
        name="short_conv",
    )(proj, proj, proj, conv_w)


def _mla_kernel(q_ref, kn_ref, v_ref, kr_ref, cos_ref, sin_ref, o_ref, kc_ref):
    t = pl.program_id(2)
    sc = (MLA_NOPE + MLA_ROPE) ** -0.5
    half = MLA_ROPE // 4

    @pl.when(t == 0)
    def _():
        kc_ref[:, 0:MLA_NOPE] = kn_ref[...]
        kc_ref[:, MLA_NOPE:MLA_QK_PAD] = _rotate(kr_ref[...].astype(F32), cos_ref[...], sin_ref[...], half).astype(BF16)

    rq = pl.ds(pl.multiple_of(t * ROW_TILE, ROW_TILE), ROW_TILE)
    qn = (q_ref[:, 0:MLA_NOPE].astype(F32) * sc).astype(BF16)
    qr = (_rotate(q_ref[:, MLA_NOPE:MLA_QK_PAD].astype(F32), cos_ref[rq, :], sin_ref[rq, :], half) * sc).astype(BF16)
    q = jnp.concatenate([qn, qr], axis=1)

    def attend(n_keys):
        s = lax.dot_general(q, kc_ref[0:n_keys, :], NT_DIMS, preferred_element_type=F32)
        p = jnp.exp(s - jnp.max(s, axis=-1, keepdims=True))
        den = jnp.sum(p, axis=-1, keepdims=True)
        o_ref[...] = (jnp.dot(p.astype(BF16), v_ref[0:n_keys, :], preferred_element_type=F32) / den).astype(BF16)

    @pl.when(t < CTX_LEN // ROW_TILE)
    def _():
        attend(CTX_LEN)

    @pl.when(t >= CTX_LEN // ROW_TILE)
    def _():
        attend(TB)


def _mla_attention(qcat, kv, proj, cos, sin):
    tiles = TB // ROW_TILE
    kr_cb = (3 * CONV_WIDTH + MLA_Q_RANK + MLA_KV_RANK) // LANES
    tab_spec = pl.BlockSpec((TB, LANES), lambda b, h, t: (0, 0))
    return pl.pallas_call(
        _mla_kernel,
        grid=(BATCH, MLA_HEADS, tiles),
        in_specs=[
            pl.BlockSpec((ROW_TILE, MLA_QK_PAD), lambda b, h, t: (b * tiles + t, h)),
            pl.BlockSpec((TB, MLA_NOPE), lambda b, h, t: (b, 2 * h)),
            pl.BlockSpec((TB, MLA_V), lambda b, h, t: (b, 2 * h + 1)),
            pl.BlockSpec((TB, LANES), lambda b, h, t: (b, kr_cb)),
            tab_spec, tab_spec,
        ],
        out_specs=pl.BlockSpec((ROW_TILE, MLA_V), lambda b, h, t: (b * tiles + t, h)),
        out_shape=jax.ShapeDtypeStruct((M_ROWS, MLA_HEADS * MLA_V), BF16),
        scratch_shapes=[pltpu.VMEM((TB, MLA_QK_PAD), BF16)],
        compiler_params=_params(("arbitrary", "arbitrary", "arbitrary")),
        name="mla_attention",
    )(qcat, kv, kv, proj, cos, sin)


def kernel(x, c, ctx, c_ctx, w_mod, b_mod, norm1_w, norm2_w, mlp_w1, mlp_w2, ev_w_in, ev_ret_log2_decay,
           ev_ret_gn_w, ev_sink, ev_w_out, od_w_in, od_conv_w, od_q_norm_w, od_kv_norm_w, od_w_uq, od_w_ukv,
           od_w_out, norm_f):
    assert x.shape == (BATCH, SEQ, D_MODEL) and ctx.shape == (BATCH, CTX_LEN, D_MODEL)
    rows = SEQ // GRID_W
    row = jnp.repeat(jnp.arange(rows, dtype=jnp.int32), GRID_W)
    col = jnp.tile(jnp.arange(GRID_W, dtype=jnp.int32), rows)
    pos = jnp.arange(SEQ, dtype=jnp.int32)
    ret_cos, ret_sin = _rope_tables([pos], HEAD_DIM)
    win_cos, win_sin = _rope_tables([row, col], HEAD_DIM)
    mla_cos, mla_sin = _rope_tables([row, col], MLA_ROPE)

    cvec = jnp.concatenate([c, c_ctx[None], jnp.zeros((8 - BATCH - 1, D_MODEL), F32)], axis=0)
    mod_all = _modulation(cvec, w_mod, b_mod)

    h = jnp.concatenate([ctx, x], axis=1).reshape(M_ROWS, D_MODEL)

    for layer in range(DEPTH):
        j = layer // 2
        mod2 = mod_all[layer]
        mod3 = mod2.reshape(8, 6, D_MODEL)
        n1 = _norm_mod(h, norm1_w[layer], mod3, 0)
        if layer % 2 == 0:
            proj = _matmul_ws([n1], ev_w_in[j], tn=EVEN_IN // 4, tm=MM_TM_SMALL, name="even_in_proj")
            mix_a = _retention(proj, ev_ret_log2_decay[j], ev_ret_gn_w[j], ret_cos, ret_sin)
            mix_b = _window_attention(proj, ev_sink[j], win_cos, win_sin)
            w_out = ev_w_out[j]
        else:
            proj = _matmul_ws([n1], od_w_in[j], tn=ODD_IN_PAD // 4, tm=MM_TM_BIG, n_out=ODD_IN_PAD,
                              name="odd_in_proj")
            mix_a = _short_conv(proj, od_conv_w[j])
            w_uq = jnp.pad(od_w_uq[j].reshape(MLA_Q_RANK, MLA_HEADS, MLA_NOPE + MLA_ROPE),
                           ((0, 0), (0, 0), (0, MLA_QK_PAD - MLA_NOPE - MLA_ROPE))
                           ).reshape(MLA_Q_RANK, MLA_HEADS * MLA_QK_PAD)
            qcat = _matmul_ws([proj], w_uq, tn=MLA_HEADS * MLA_QK_PAD, tm=MM_TM_BIG, k_sizes=[MLA_Q_RANK],
                              a_col_blocks=[3 * CONV_WIDTH // MLA_Q_RANK], norm_w=od_q_norm_w[j], name="mla_q_up")
            kv = _matmul_ws([proj], od_w_ukv[j], tn=MLA_HEADS * (MLA_NOPE + MLA_V), tm=MM_TM_BIG,
                            k_sizes=[MLA_KV_RANK], a_col_blocks=[(3 * CONV_WIDTH + MLA_Q_RANK) // MLA_KV_RANK],
                            norm_w=od_kv_norm_w[j], name="mla_kv_up")
            mix_b = _mla_attention(qcat, kv, proj, mla_cos, mla_sin)
            w_out = od_w_out[j]
        h, n2 = _matmul_ws([mix_a, mix_b], w_out, tn=D_MODEL, tm=MM_TM_SMALL, epilogue="gated_residual_norm",
                           res=h, mod=mod2, gate_idx=2, next_norm_w=norm2_w[layer], name="mixer_out_proj")
        ff = _matmul_ws([n2], mlp_w1[layer], tn=1024, tm=MM_TM_BIG, epilogue="relu2", name="mlp_up")
        h = _matmul([ff], mlp_w2[layer].astype(BF16), tn=512, epilogue="gated_residual",
                    res=h, gates=mod2, gate_idx=5, name="mlp_down")
    return _final_norm(h, norm_f)
```

```python
import functools

import jax
import jax.numpy as jnp
from jax import lax
from jax.experimental import pallas as pl
from jax.experimental.pallas import tpu as pltpu

F32 = jnp.float32
BF16 = jnp.bfloat16

D_MODEL = 2048
BATCH = 2
SEQ = 4096
DEPTH = 4
GRID_W = 64
CTX_LEN = 256
HEAD_DIM = 128
ROPE_BASE = 10000.0
EPS = 1e-6
NEG_INF = -1e30

RET_HEADS = 8
RET_CHUNK = 128
RET_UNROLL = 8
WIN_HEADS = 8
WIN_KV_HEADS = 2
WIN_GROUP = WIN_HEADS // WIN_KV_HEADS
WIN_RADIUS = 128
WIN_BLOCK = 128
CONV_WIDTH = 1024
MLA_HEADS = 8
MLA_Q_RANK = 512
MLA_KV_RANK = 256
MLA_NOPE = 128
MLA_ROPE = 64
MLA_V = 128
D_FF = 4 * D_MODEL

EVEN_IN = 4 * RET_HEADS * HEAD_DIM + (WIN_HEADS + 2 * WIN_KV_HEADS) * HEAD_DIM
ODD_IN = 3 * CONV_WIDTH + MLA_Q_RANK + MLA_KV_RANK + MLA_ROPE
ODD_IN_PAD = 4096
MLA_QK_PAD = 256

TB = CTX_LEN + SEQ
M_ROWS = BATCH * TB
LANES = 128
ROW_TILE = 256
MM_TM_BIG = M_ROWS // 8
MM_TM_SMALL = M_ROWS // 16
VMEM_LIMIT = 60 * 1024 * 1024

NT_DIMS = (((1,), (1,)), ((), ()))
TN_DIMS = (((0,), (0,)), ((), ()))


def _params(sem, vmem=VMEM_LIMIT):
    return pltpu.CompilerParams(dimension_semantics=sem, vmem_limit_bytes=vmem)


def _rope_tables(pos_list, d_rot):
    per = d_rot // len(pos_list)
    half = per // 2
    inv = ROPE_BASE ** (-jnp.arange(0, per, 2, dtype=F32) / per)
    cos_parts, sin_parts = [], []
    for pos in pos_list:
        ang = pos.astype(F32)[:, None] * inv[None, :]
        c, s = jnp.cos(ang), jnp.sin(ang)
        cos_parts += [c, c]
        sin_parts += [-s, s]
    cos = jnp.concatenate(cos_parts, axis=-1)
    sin = jnp.concatenate(sin_parts, axis=-1)
    pad = LANES - d_rot
    if pad:
        cos = jnp.pad(cos, ((0, 0), (0, pad)))
        sin = jnp.pad(sin, ((0, 0), (0, pad)))
    ctx_cos = jnp.pad(jnp.ones((CTX_LEN, d_rot), F32), ((0, 0), (0, pad)))
    ctx_sin = jnp.zeros((CTX_LEN, LANES), F32)
    return jnp.concatenate([ctx_cos, cos], axis=0), jnp.concatenate([ctx_sin, sin], axis=0)


def _rotate(x, cos, sin, half):
    if 2 * half == LANES:
        partner = pltpu.roll(x, half, 1)
    else:
        lane = lax.broadcasted_iota(jnp.int32, (1, LANES), 1)
        first = (lane % (2 * half)) < half
        partner = jnp.where(first, pltpu.roll(x, LANES - half, 1), pltpu.roll(x, half, 1))
    return x * cos + partner * sin


def _modulation_kernel(c_ref, w_ref, b_ref, o_ref):
    s = jax.nn.silu(c_ref[...]).astype(BF16)
    o_ref[...] = jnp.dot(s, w_ref[...].astype(BF16), preferred_element_type=F32) + b_ref[...]


def _modulation(cvec, w_mod, b_mod):
    tn = 1536
    n = 6 * D_MODEL
    return pl.pallas_call(
        _modulation_kernel,
        grid=(DEPTH, n // tn),
        in_specs=[
            pl.BlockSpec((8, D_MODEL), lambda l, j: (0, 0)),
            pl.BlockSpec((None, D_MODEL, tn), lambda l, j: (l, 0, j)),
            pl.BlockSpec((None, 1, tn), lambda l, j: (l, 0, j)),
        ],
        out_specs=pl.BlockSpec((None, 8, tn), lambda l, j: (l, 0, j)),
        out_shape=jax.ShapeDtypeStruct((DEPTH, 8, n), F32),
        compiler_params=_params(("arbitrary", "arbitrary")),
        name="modulation",
    )(cvec, w_mod, b_mod.reshape(DEPTH, 1, n))


def _segment_of_tile(i):
    tiles = TB // ROW_TILE
    return jnp.where(i % tiles == 0, BATCH, i // tiles)


def _norm_mod_kernel(x_ref, w_ref, mod_ref, o_ref, *, shift_idx):
    x = x_ref[...]
    y = x * lax.rsqrt(jnp.mean(x * x, axis=-1, keepdims=True) + EPS) * w_ref[...]
    shift = mod_ref[shift_idx:shift_idx + 1, :]
    scale = mod_ref[shift_idx + 1:shift_idx + 2, :]
    o_ref[...] = (y * (1.0 + scale) + shift).astype(BF16)


def _norm_mod(h, norm_w, mod3, shift_idx):
    return pl.pallas_call(
        functools.partial(_norm_mod_kernel, shift_idx=shift_idx),
        grid=(M_ROWS // ROW_TILE,),
        in_specs=[
            pl.BlockSpec((ROW_TILE, D_MODEL), lambda i: (i, 0)),
            pl.BlockSpec((1, D_MODEL), lambda i: (0, 0)),
            pl.BlockSpec((None, 6, D_MODEL), lambda i: (_segment_of_tile(i), 0, 0)),
        ],
        out_specs=pl.BlockSpec((ROW_TILE, D_MODEL), lambda i: (i, 0)),
        out_shape=jax.ShapeDtypeStruct((M_ROWS, D_MODEL), BF16),
        compiler_params=_params(("arbitrary",)),
        name="norm_mod",
    )(h, norm_w.reshape(1, D_MODEL), mod3)


def _final_norm_kernel(x_ref, w_ref, o_ref):
    x = x_ref[...]
    o_ref[...] = x * lax.rsqrt(jnp.mean(x * x, axis=-1, keepdims=True) + EPS) * w_ref[...]


def _final_norm(h, norm_w):
    tiles = TB // ROW_TILE
    ctx_tiles = CTX_LEN // ROW_TILE
    return pl.pallas_call(
        _final_norm_kernel,
        grid=(BATCH, SEQ // ROW_TILE),
        in_specs=[
            pl.BlockSpec((ROW_TILE, D_MODEL), lambda b, t: (b * tiles + ctx_tiles + t, 0)),
            pl.BlockSpec((1, D_MODEL), lambda b, t: (0, 0)),
        ],
        out_specs=pl.BlockSpec((None, ROW_TILE, D_MODEL), lambda b, t: (b, t, 0)),
        out_shape=jax.ShapeDtypeStruct((BATCH, SEQ, D_MODEL), F32),
        compiler_params=_params(("arbitrary", "arbitrary")),
        name="final_norm",
    )(h, norm_w.reshape(1, D_MODEL))


def _row_select(tab_ref, row0, tm):
    row = row0 + lax.broadcasted_iota(jnp.int32, (tm, 1), 0)
    second = row >= TB
    is_ctx = (row - jnp.where(second, TB, 0)) < CTX_LEN
    return jnp.where(is_ctx, tab_ref[2:3, :], jnp.where(second, tab_ref[1:2, :], tab_ref[0:1, :]))


def _mm_ws_kernel(*refs, k_sizes, epilogue, norm, tm, valid_cols):
    n_a = len(k_sizes)
    a_refs = refs[:n_a]
    w_ref = refs[n_a]
    pos = n_a + 1
    if norm:
        nw_ref = refs[pos]
        pos += 1
    if epilogue in ("gated_residual", "gated_residual_norm"):
        res_ref, gate_ref = refs[pos], refs[pos + 1]
        pos += 2
    if epilogue == "gated_residual_norm":
        n2w_ref, shift_ref, scale_ref = refs[pos:pos + 3]
        pos += 3
    o_ref = refs[pos]
    pos += 1
    if epilogue == "gated_residual_norm":
        n_ref = refs[pos]
        pos += 1
    wb_ref = refs[pos]
    i = pl.program_id(1)

    @pl.when(i == 0)
    def _():
        w = w_ref[...]
        if valid_cols is not None:
            tn = w.shape[1]
            col = pl.program_id(0) * tn + lax.broadcasted_iota(jnp.int32, (1, tn), 1)
            w = jnp.where(col < valid_cols, w, 0.0)
        wb_ref[...] = w.astype(BF16)

    acc = None
    off = 0
    for a_ref, ks in zip(a_refs, k_sizes):
        a = a_ref[...]
        if norm:
            af = a.astype(F32)
            af = af * lax.rsqrt(jnp.mean(af * af, axis=-1, keepdims=True) + EPS) * nw_ref[...]
            a = af.astype(BF16)
        part = jnp.dot(a, wb_ref[off:off + ks, :], preferred_element_type=F32)
        acc = part if acc is None else acc + part
        off += ks

    if epilogue == "bf16":
        o_ref[...] = acc.astype(BF16)
    elif epilogue == "relu2":
        r = jnp.maximum(acc, 0.0)
        o_ref[...] = (r * r).astype(BF16)
    else:
        h = res_ref[...] + _row_select(gate_ref, i * tm, tm) * acc
        o_ref[...] = h
        if epilogue == "gated_residual_norm":
            y = h * lax.rsqrt(jnp.mean(h * h, axis=-1, keepdims=True) + EPS) * n2w_ref[...]
            n_ref[...] = (y * (1.0 + _row_select(scale_ref, i * tm, tm))
                          + _row_select(shift_ref, i * tm, tm)).astype(BF16)


def _matmul_ws(a_list, w, layer, *, tn, tm, epilogue="bf16", a_col_blocks=None, k_sizes=None, norm_w=None,
               res=None, mod=None, gate_idx=0, next_norm_w=None, n_out=None, name="matmul"):
    assert BATCH == 2
    _, kdim, n_w = w.shape
    n = n_w if n_out is None else n_out
    if k_sizes is None:
        k_sizes = [a.shape[1] for a in a_list]
    if a_col_blocks is None:
        a_col_blocks = [0] * len(a_list)
    assert sum(k_sizes) == kdim and n % tn == 0 and M_ROWS % tm == 0
    gated = epilogue in ("gated_residual", "gated_residual_norm")
    in_specs = [pl.BlockSpec((tm, ks), functools.partial(lambda j, i, cb: (i, cb), cb=cb))
                for ks, cb in zip(k_sizes, a_col_blocks)]
    single_w = n == tn
    w_mode = dict(pipeline_mode=pl.Buffered(1)) if single_w else {}
    in_specs.append(pl.BlockSpec((None, kdim, tn), lambda j, i: (layer, 0, j), **w_mode))
    operands = list(a_list) + [w]
    if norm_w is not None:
        assert len(a_list) == 1
        in_specs.append(pl.BlockSpec((1, kdim), lambda j, i: (0, 0)))
        operands.append(norm_w.reshape(1, kdim))
    out_specs = pl.BlockSpec((tm, tn), lambda j, i: (i, j))
    out_shape = jax.ShapeDtypeStruct((M_ROWS, n), F32 if gated else BF16)
    if gated:
        blocks_per_vec = D_MODEL // tn
        in_specs.append(pl.BlockSpec((tm, tn), lambda j, i: (i, j)))
        in_specs.append(pl.BlockSpec((8, tn), lambda j, i: (0, gate_idx * blocks_per_vec + j)))
        operands += [res, mod]
    if epilogue == "gated_residual_norm":
        assert tn == n == D_MODEL
        in_specs.append(pl.BlockSpec((1, D_MODEL), lambda j, i: (0, 0)))
        in_specs.append(pl.BlockSpec((8, D_MODEL), lambda j, i: (0, gate_idx + 1)))
        in_specs.append(pl.BlockSpec((8, D_MODEL), lambda j, i: (0, gate_idx + 2)))
        operands += [next_norm_w.reshape(1, D_MODEL), mod, mod]
        out_specs = [out_specs, pl.BlockSpec((tm, tn), lambda j, i: (i, j))]
        out_shape = [out_shape, jax.ShapeDtypeStruct((M_ROWS, n), BF16)]
    return pl.pallas_call(
        functools.partial(_mm_ws_kernel, k_sizes=tuple(k_sizes), epilogue=epilogue,
                          norm=norm_w is not None, tm=tm, valid_cols=None if n == n_w else n_w),
        grid=(n // tn, M_ROWS // tm),
        in_specs=in_specs,
        out_specs=out_specs,
        out_shape=out_shape,
        scratch_shapes=[pltpu.VMEM((kdim, tn), BF16)],
        compiler_params=_params(("arbitrary", "arbitrary")),
        name=name,
    )(*operands)


def _mlp_down_kernel(a_ref, w_ref, res_ref, gate_ref, o_ref, wb_ref, acc_ref, *, tm, nk):
    k = pl.program_id(1)
    i = pl.program_id(2)

    @pl.when(i == 0)
    def _():
        wb_ref[...] = w_ref[...].astype(BF16)

    part = jnp.dot(a_ref[...], wb_ref[...], preferred_element_type=F32)

    @pl.when(k == 0)
    def _():
        acc_ref[i] = part

    @pl.when((k > 0) & (k < nk - 1))
    def _():
        acc_ref[i] += part

    @pl.when(k == nk - 1)
    def _():
        o_ref[...] = res_ref[...] + _row_select(gate_ref, i * tm, tm) * (acc_ref[i] + part)


def _mlp_down(a, w, layer, res, mod, gate_idx):
    _, kdim, n = w.shape
    tm, tn, tk = MM_TM_BIG, 512, 2048
    nk = kdim // tk
    last = nk - 1
    out_map = lambda j, k, i: (jnp.where(k == last, i, 0), j)
    return pl.pallas_call(
        functools.partial(_mlp_down_kernel, tm=tm, nk=nk),
        grid=(n // tn, nk, M_ROWS // tm),
        in_specs=[
            pl.BlockSpec((tm, tk), lambda j, k, i: (i, k)),
            pl.BlockSpec((None, tk, tn), lambda j, k, i: (layer, k, j)),
            pl.BlockSpec((tm, tn), out_map),
            pl.BlockSpec((8, tn), lambda j, k, i: (0, gate_idx * (D_MODEL // tn) + j)),
        ],
        out_specs=pl.BlockSpec((tm, tn), out_map),
        out_shape=jax.ShapeDtypeStruct((M_ROWS, n), F32),
        scratch_shapes=[pltpu.VMEM((tk, tn), BF16), pltpu.VMEM((M_ROWS // tm, tm, tn), F32)],
        compiler_params=_params(("arbitrary", "arbitrary", "arbitrary")),
        name="mlp_down",
    )(a, w, res, mod)


def _retention_kernel(q_ref, k_ref, v_ref, g_ref, cos_ref, sin_ref, ld_ref, gn_ref, o_ref, kb_ref, kv_ref, st_ref):
    c = RET_CHUNK
    ctx_chunks = CTX_LEN // c
    lat_chunks = SEQ // c
    n_chunks = ctx_chunks + lat_chunks
    sc = HEAD_DIM ** -0.5
    lg = jnp.log1p(-jnp.exp2(ld_ref[...]))
    lgf = lg[0:1, 0:1]
    lgb = lg[1:2, 0:1]
    ii = lax.broadcasted_iota(jnp.int32, (c, 1), 0).astype(F32)
    jj = lax.broadcasted_iota(jnp.int32, (1, c), 1).astype(F32)
    diff = ii - jj
    dmat = jnp.where(diff >= 0, jnp.exp(lgf * jnp.maximum(diff, 0.0)), jnp.exp(lgb * jnp.maximum(-diff, 0.0)))
    qf_dec = jnp.exp(lgf * (ii + 1.0))
    qb_dec = jnp.exp(lgb * (c - ii))
    kf_dec = jnp.exp(lgf * (c - 1.0 - ii))
    kb_dec = jnp.exp(lgb * ii)
    gcf = jnp.exp(lgf * c)
    gcb = jnp.exp(lgb * c)

    def rows(n):
        return pl.ds(pl.multiple_of(n * c, c), c)

    def roped(ref, n):
        x = ref[rows(n), :].astype(F32)
        return _rotate(x, cos_ref[rows(n), :], sin_ref[rows(n), :], HEAD_DIM // 2)

    def increments(n, carry):
        k = roped(k_ref, n)
        kb_ref[rows(n), :] = k.astype(BF16)
        kd = jnp.concatenate([(k * kf_dec).astype(BF16), (k * kb_dec).astype(BF16)], axis=1)
        kv_ref[n] = lax.dot_general(kd, v_ref[rows(n), :], TN_DIMS, preferred_element_type=F32)
        return carry

    def recurrence(n0, cnt, s_f, s_b):
        def body(t, carry):
            s_f, s_b = carry
            nf = n0 + t
            nb = n0 + cnt - 1 - t
            st_ref[nf, 0:c, :] = s_f.astype(BF16)
            st_ref[nb, c:2 * c, :] = s_b.astype(BF16)
            return gcf * s_f + kv_ref[nf, 0:c, :], gcb * s_b + kv_ref[nb, c:2 * c, :]
        return lax.fori_loop(0, cnt, body, (s_f, s_b))

    def outputs(n, carry):
        q = roped(q_ref, n) * sc
        v = v_ref[rows(n), :]
        scores = lax.dot_general(q.astype(BF16), kb_ref[rows(n), :], NT_DIMS, preferred_element_type=F32) * dmat
        qd = jnp.concatenate([(q * qf_dec).astype(BF16), (q * qb_dec).astype(BF16)], axis=1)
        o = (jnp.dot(scores.astype(BF16), v, preferred_element_type=F32)
             + jnp.dot(qd, st_ref[n], preferred_element_type=F32))
        mu = jnp.mean(o, axis=-1, keepdims=True)
        oc = o - mu
        var = jnp.mean(oc * oc, axis=-1, keepdims=True)
        on = oc * lax.rsqrt(var + EPS) * gn_ref[...]
        g = g_ref[rows(n), :].astype(F32)
        o_ref[rows(n), :] = (jax.nn.silu(g) * on).astype(BF16)
        return carry

    lax.fori_loop(0, n_chunks, increments, 0, unroll=RET_UNROLL)
    zero = jnp.zeros((c, c), F32)
    s_f, s_b = recurrence(0, ctx_chunks, zero, zero)
    recurrence(ctx_chunks, lat_chunks, s_f, s_b)
    lax.fori_loop(0, n_chunks, outputs, 0, unroll=RET_UNROLL)


def _retention(proj, log2_decay, gn_w, cos, sin):
    h = RET_HEADS
    ld = jnp.broadcast_to(log2_decay.T[:, :, None], (h, 2, LANES))
    seq_spec = lambda cb0: pl.BlockSpec((TB, HEAD_DIM), functools.partial(lambda b, hh, cb0: (b, cb0 + hh), cb0=cb0))
    tab_spec = pl.BlockSpec((TB, LANES), lambda b, hh: (0, 0))
    return pl.pallas_call(
        _retention_kernel,
        grid=(BATCH, h),
        in_specs=[seq_spec(0), seq_spec(h), seq_spec(2 * h), seq_spec(3 * h), tab_spec, tab_spec,
                  pl.BlockSpec((None, 2, LANES), lambda b, hh: (hh, 0, 0)),
                  pl.BlockSpec((1, HEAD_DIM), lambda b, hh: (0, hh))],
        out_specs=pl.BlockSpec((TB, HEAD_DIM), lambda b, hh: (b, hh)),
        out_shape=jax.ShapeDtypeStruct((M_ROWS, h * HEAD_DIM), BF16),
        scratch_shapes=[pltpu.VMEM((TB, HEAD_DIM), BF16),
                        pltpu.VMEM((TB // RET_CHUNK, 2 * RET_CHUNK, HEAD_DIM), F32),
                        pltpu.VMEM((TB // RET_CHUNK, 2 * RET_CHUNK, HEAD_DIM), BF16)],
        compiler_params=_params(("arbitrary", "arbitrary")),
        name="retention",
    )(proj, proj, proj, proj, cos, sin, ld, gn_w.reshape(1, h * HEAD_DIM))


def _window_kernel(sink_ref, q_ref, k_ref, v_ref, cos_ref, sin_ref, o_ref, kr_ref):
    blk = WIN_BLOCK
    ctx_chunks = CTX_LEN // blk
    n_chunks = TB // blk
    kvh = pl.program_id(1)
    sc = HEAD_DIM ** -0.5
    half = HEAD_DIM // 4
    rows_q = WIN_GROUP * blk

    kr_ref[...] = _rotate(k_ref[...].astype(F32), cos_ref[...], sin_ref[...], half).astype(BF16)

    row_iota = lax.broadcasted_iota(jnp.int32, (rows_q, 1), 0)
    head = row_iota // blk
    sink = jnp.zeros((rows_q, 1), F32)
    for g in range(WIN_GROUP):
        sink = jnp.where(head == g, sink_ref[kvh * WIN_GROUP + g], sink)

    def chunk(n, carry):
        rq = pl.ds(pl.multiple_of(n * blk, blk), blk)
        cos = cos_ref[rq, :]
        sin = sin_ref[rq, :]
        q = jnp.concatenate(
            [(_rotate(q_ref[rq, g * HEAD_DIM:(g + 1) * HEAD_DIM].astype(F32), cos, sin, half) * sc).astype(BF16)
             for g in range(WIN_GROUP)], axis=0)

        cstart = jnp.clip(n - 1, ctx_chunks, n_chunks - 3)
        rw = pl.ds(pl.multiple_of(cstart * blk, blk), 3 * blk)
        s_c = lax.dot_general(q, kr_ref[0:CTX_LEN, :], NT_DIMS, preferred_element_type=F32)
        s_w = lax.dot_general(q, kr_ref[rw, :], NT_DIMS, preferred_element_type=F32)
        qpos = (n - ctx_chunks) * blk + row_iota % blk
        kpos = (cstart - ctx_chunks) * blk + lax.broadcasted_iota(jnp.int32, (1, 3 * blk), 1)
        valid = (jnp.abs(qpos - kpos) <= WIN_RADIUS) & (n >= ctx_chunks)
        s_w = jnp.where(valid, s_w, NEG_INF)

        m = jnp.maximum(jnp.maximum(jnp.max(s_c, axis=-1, keepdims=True), jnp.max(s_w, axis=-1, keepdims=True)),
                        sink)
        p_c = jnp.exp(s_c - m)
        p_w = jnp.exp(s_w - m)
        den = jnp.sum(p_c, axis=-1, keepdims=True) + jnp.sum(p_w, axis=-1, keepdims=True) + jnp.exp(sink - m)
        o = (jnp.dot(p_c.astype(BF16), v_ref[0:CTX_LEN, :], preferred_element_type=F32)
             + jnp.dot(p_w.astype(BF16), v_ref[rw, :], preferred_element_type=F32)) / den
        for g in range(WIN_GROUP):
            o_ref[rq, g * HEAD_DIM:(g + 1) * HEAD_DIM] = o[g * blk:(g + 1) * blk, :].astype(BF16)
        return carry

    lax.fori_loop(0, n_chunks, chunk, 0, unroll=2)


def _window_attention(proj, sink, cos, sin):
    gw = WIN_GROUP * HEAD_DIM
    q_cb0 = 4 * RET_HEADS * HEAD_DIM // gw
    k_cb0 = (4 * RET_HEADS + WIN_HEADS) * HEAD_DIM // HEAD_DIM
    v_cb0 = k_cb0 + WIN_KV_HEADS
    tab_spec = pl.BlockSpec((TB, LANES), lambda b, kh: (0, 0))
    return pl.pallas_call(
        _window_kernel,
        grid=(BATCH, WIN_KV_HEADS),
        in_specs=[
            pl.BlockSpec(memory_space=pltpu.SMEM),
            pl.BlockSpec((TB, gw), lambda b, kh: (b, q_cb0 + kh)),
            pl.BlockSpec((TB, HEAD_DIM), lambda b, kh: (b, k_cb0 + kh)),
            pl.BlockSpec((TB, HEAD_DIM), lambda b, kh: (b, v_cb0 + kh)),
            tab_spec, tab_spec,
        ],
        out_specs=pl.BlockSpec((TB, gw), lambda b, kh: (b, kh)),
        out_shape=jax.ShapeDtypeStruct((M_ROWS, WIN_HEADS * HEAD_DIM), BF16),
        scratch_shapes=[pltpu.VMEM((TB, HEAD_DIM), BF16)],
        compiler_params=_params(("arbitrary", "arbitrary")),
        name="window_attention",
    )(sink, proj, proj, proj, cos, sin)


def _conv_kernel(b_ref, c_ref, x_ref, w_ref, o_ref):
    u = c_ref[...].astype(F32) * x_ref[...].astype(F32)
    row = lax.broadcasted_iota(jnp.int32, (TB, 1), 0)
    prev = jnp.where((row == 0) | (row == CTX_LEN), 0.0, pltpu.roll(u, 1, 0))
    nxt = jnp.where((row == CTX_LEN - 1) | (row == TB - 1), 0.0, pltpu.roll(u, TB - 1, 0))
    z = prev * w_ref[0:1, :] + u * w_ref[1:2, :] + nxt * w_ref[2:3, :]
    o_ref[...] = (b_ref[...].astype(F32) * z).astype(BF16)


def _short_conv(proj, conv_w):
    tc = 256
    nblk = CONV_WIDTH // tc
    spec = lambda part: pl.BlockSpec((TB, tc), functools.partial(lambda b, j, part: (b, part * nblk + j), part=part))
    return pl.pallas_call(
        _conv_kernel,
        grid=(BATCH, nblk),
        in_specs=[spec(0), spec(1), spec(2), pl.BlockSpec((3, tc), lambda b, j: (0, j))],
        out_specs=pl.BlockSpec((TB, tc), lambda b, j: (b, j)),
        out_shape=jax.ShapeDtypeStruct((M_ROWS, CONV_WIDTH), BF16),
        compiler_params=_params(("arbitrary", "arbitrary")),
        name="short_conv",
    )(proj, proj, proj, conv_w)


MLA_HEADS_PER_STEP = 2
MLA_Q_TILE = 512
MLA_KEY_CHUNK = 1024


def _mla_kernel(q_ref, kv_ref, kr_ref, cos_ref, sin_ref, o_ref, kc_ref, vx_ref):
    sc = (MLA_NOPE + MLA_ROPE) ** -0.5
    half = MLA_ROPE // 4
    kv_w = MLA_NOPE + MLA_V

    kr = _rotate(kr_ref[...].astype(F32), cos_ref[...], sin_ref[...], half).astype(BF16)
    for hh in range(MLA_HEADS_PER_STEP):
        kc_ref[hh, :, 0:MLA_NOPE] = kv_ref[:, hh * kv_w:hh * kv_w + MLA_NOPE]
        kc_ref[hh, :, MLA_NOPE:MLA_QK_PAD] = kr
        vx_ref[hh, :, 0:MLA_V] = kv_ref[:, hh * kv_w + MLA_NOPE:(hh + 1) * kv_w]
        vx_ref[hh, :, MLA_V:2 * MLA_V] = jnp.ones((TB, MLA_V), BF16)

    def attend(row0, n_rows, key_chunks):
        rq = pl.ds(row0, n_rows)
        cos = cos_ref[rq, :]
        sin = sin_ref[rq, :]
        for hh in range(MLA_HEADS_PER_STEP):
            c0 = hh * MLA_QK_PAD
            qn = (q_ref[rq, c0:c0 + MLA_NOPE].astype(F32) * sc).astype(BF16)
            qr = (_rotate(q_ref[rq, c0 + MLA_NOPE:c0 + MLA_QK_PAD].astype(F32), cos, sin, half) * sc).astype(BF16)
            q = jnp.concatenate([qn, qr], axis=1)
            m = None
            acc = None
            for k0, kn in key_chunks:
                s = lax.dot_general(q, kc_ref[hh, k0:k0 + kn, :], NT_DIMS, preferred_element_type=F32)
                m_chunk = jnp.max(s, axis=-1, keepdims=True)
                m_new = m_chunk if m is None else jnp.maximum(m, m_chunk)
                p = jnp.exp(s - m_new).astype(BF16)
                pv = jnp.dot(p, vx_ref[hh, k0:k0 + kn, :], preferred_element_type=F32)
                acc = pv if acc is None else acc * jnp.exp(m - m_new) + pv
                m = m_new
            o_ref[rq, hh * MLA_V:(hh + 1) * MLA_V] = (acc[:, 0:MLA_V] / acc[:, MLA_V:2 * MLA_V]).astype(BF16)

    ctx_keys = [(0, CTX_LEN)]
    all_keys = ctx_keys + [(CTX_LEN + k * MLA_KEY_CHUNK, MLA_KEY_CHUNK) for k in range(SEQ // MLA_KEY_CHUNK)]
    attend(0, CTX_LEN, ctx_keys)

    def latent_tile(t, carry):
        attend(pl.multiple_of(CTX_LEN + t * MLA_Q_TILE, ROW_TILE), MLA_Q_TILE, all_keys)
        return carry

    lax.fori_loop(0, SEQ // MLA_Q_TILE, latent_tile, 0)


def _mla_attention(qcat, kv, proj, cos, sin):
    hps = MLA_HEADS_PER_STEP
    kr_cb = (3 * CONV_WIDTH + MLA_Q_RANK + MLA_KV_RANK) // LANES
    tab_spec = pl.BlockSpec((TB, LANES), lambda b, h: (0, 0))
    return pl.pallas_call(
        _mla_kernel,
        grid=(BATCH, MLA_HEADS // hps),
        in_specs=[
            pl.BlockSpec((TB, hps * MLA_QK_PAD), lambda b, h: (b, h)),
            pl.BlockSpec((TB, hps * (MLA_NOPE + MLA_V)), lambda b, h: (b, h)),
            pl.BlockSpec((TB, LANES), lambda b, h: (b, kr_cb)),
            tab_spec, tab_spec,
        ],
        out_specs=pl.BlockSpec((TB, hps * MLA_V), lambda b, h: (b, h)),
        out_shape=jax.ShapeDtypeStruct((M_ROWS, MLA_HEADS * MLA_V), BF16),
        scratch_shapes=[pltpu.VMEM((hps, TB, MLA_QK_PAD), BF16), pltpu.VMEM((hps, TB, 2 * MLA_V), BF16)],
        compiler_params=_params(("arbitrary", "arbitrary")),
        name="mla_attention",
    )(qcat, kv, proj, cos, sin)


def kernel(x, c, ctx, c_ctx, w_mod, b_mod, norm1_w, norm2_w, mlp_w1, mlp_w2, ev_w_in, ev_ret_log2_decay,
           ev_ret_gn_w, ev_sink, ev_w_out, od_w_in, od_conv_w, od_q_norm_w, od_kv_norm_w, od_w_uq, od_w_ukv,
           od_w_out, norm_f):
    assert x.shape == (BATCH, SEQ, D_MODEL) and ctx.shape == (BATCH, CTX_LEN, D_MODEL)
    rows = SEQ // GRID_W
    row = jnp.repeat(jnp.arange(rows, dtype=jnp.int32), GRID_W)
    col = jnp.tile(jnp.arange(GRID_W, dtype=jnp.int32), rows)
    pos = jnp.arange(SEQ, dtype=jnp.int32)
    ret_cos, ret_sin = _rope_tables([pos], HEAD_DIM)
    win_cos, win_sin = _rope_tables([row, col], HEAD_DIM)
    mla_cos, mla_sin = _rope_tables([row, col], MLA_ROPE)

    cvec = jnp.concatenate([c, c_ctx[None], jnp.zeros((8 - BATCH - 1, D_MODEL), F32)], axis=0)
    mod_all = _modulation(cvec, w_mod, b_mod)

    h = jnp.concatenate([ctx, x], axis=1).reshape(M_ROWS, D_MODEL)

    n_odd = od_w_uq.shape[0]
    w_uq = jnp.pad(od_w_uq.reshape(n_odd, MLA_Q_RANK, MLA_HEADS, MLA_NOPE + MLA_ROPE),
                   ((0, 0), (0, 0), (0, 0), (0, MLA_QK_PAD - MLA_NOPE - MLA_ROPE))
                   ).reshape(n_odd, MLA_Q_RANK, MLA_HEADS * MLA_QK_PAD)

    for layer in range(DEPTH):
        j = layer // 2
        mod2 = mod_all[layer]
        mod3 = mod2.reshape(8, 6, D_MODEL)
        n1 = _norm_mod(h, norm1_w[layer], mod3, 0)
        if layer % 2 == 0:
            proj = _matmul_ws([n1], ev_w_in, j, tn=EVEN_IN // 4, tm=MM_TM_SMALL, name="even_in_proj")
            mix_a = _retention(proj, ev_ret_log2_decay[j], ev_ret_gn_w[j], ret_cos, ret_sin)
            mix_b = _window_attention(proj, ev_sink[j], win_cos, win_sin)
            w_out = ev_w_out
        else:
            proj = _matmul_ws([n1], od_w_in, j, tn=ODD_IN_PAD // 4, tm=MM_TM_BIG, n_out=ODD_IN_PAD,
                              name="odd_in_proj")
            mix_a = _short_conv(proj, od_conv_w[j])
            qcat = _matmul_ws([proj], w_uq, j, tn=MLA_HEADS * MLA_QK_PAD, tm=MM_TM_BIG, k_sizes=[MLA_Q_RANK],
                              a_col_blocks=[3 * CONV_WIDTH // MLA_Q_RANK], norm_w=od_q_norm_w[j], name="mla_q_up")
            kv = _matmul_ws([proj], od_w_ukv, j, tn=MLA_HEADS * (MLA_NOPE + MLA_V), tm=MM_TM_BIG,
                            k_sizes=[MLA_KV_RANK], a_col_blocks=[(3 * CONV_WIDTH + MLA_Q_RANK) // MLA_KV_RANK],
                            norm_w=od_kv_norm_w[j], name="mla_kv_up")
            mix_b = _mla_attention(qcat, kv, proj, mla_cos, mla_sin)
            w_out = od_w_out
        h, n2 = _matmul_ws([mix_a, mix_b], w_out, j, tn=D_MODEL, tm=MM_TM_SMALL, epilogue="gated_residual_norm",
                           res=h, mod=mod2, gate_idx=2, next_norm_w=norm2_w[layer], name="mixer_out_proj")
        ff = _matmul_ws([n2], mlp_w1, layer, tn=1024, tm=MM_TM_BIG, epilogue="relu2", name="mlp_up")
        h = _mlp_down(ff, mlp_w2, layer, h, mod2, 5)
    return _final_norm(h, norm_f)
```

```python
import functools

import numpy as np
import jax
import jax.numpy as jnp
from jax import lax
from jax.experimental import pallas as pl
from jax.experimental.pallas import tpu as pltpu

F32 = jnp.float32
BF16 = jnp.bfloat16

D_MODEL = 2048
BATCH = 2
SEQ = 4096
DEPTH = 4
GRID_W = 64
CTX_LEN = 256
HEAD_DIM = 128
ROPE_BASE = 10000.0
EPS = 1e-6
NEG_INF = -1e30

RET_HEADS = 8
RET_CHUNK = 128
RET_UNROLL = 8
WIN_HEADS = 8
WIN_KV_HEADS = 2
WIN_GROUP = WIN_HEADS // WIN_KV_HEADS
WIN_RADIUS = 128
WIN_BLOCK = 128
CONV_WIDTH = 1024
MLA_HEADS = 8
MLA_Q_RANK = 512
MLA_KV_RANK = 256
MLA_NOPE = 128
MLA_ROPE = 64
MLA_V = 128
D_FF = 4 * D_MODEL

EVEN_IN = 4 * RET_HEADS * HEAD_DIM + (WIN_HEADS + 2 * WIN_KV_HEADS) * HEAD_DIM
ODD_IN = 3 * CONV_WIDTH + MLA_Q_RANK + MLA_KV_RANK + MLA_ROPE
ODD_IN_PAD = 4096
MLA_QK_PAD = 256

TB = CTX_LEN + SEQ
M_ROWS = BATCH * TB
LANES = 128
ROW_TILE = 256
MM_TM_BIG = M_ROWS // 8
MM_TM_SMALL = M_ROWS // 16
MM_TM_DOWN = 512
VMEM_LIMIT = 60 * 1024 * 1024

NT_DIMS = (((1,), (1,)), ((), ()))
TN_DIMS = (((0,), (0,)), ((), ()))


def _params(sem, vmem=VMEM_LIMIT):
    return pltpu.CompilerParams(dimension_semantics=sem, vmem_limit_bytes=vmem)


def _rope_tables(pos_list, d_rot):
    per = d_rot // len(pos_list)
    inv = ROPE_BASE ** (-np.arange(0, per, 2, dtype=np.float64) / per)
    cos_parts, sin_parts = [], []
    for pos in pos_list:
        ang = pos.astype(np.float64)[:, None] * inv[None, :]
        c, s = np.cos(ang), np.sin(ang)
        cos_parts += [c, c]
        sin_parts += [-s, s]
    pad = ((0, 0), (0, LANES - d_rot))
    cos = np.pad(np.concatenate(cos_parts, axis=-1), pad)
    sin = np.pad(np.concatenate(sin_parts, axis=-1), pad)
    ctx_cos = np.pad(np.ones((CTX_LEN, d_rot)), pad)
    ctx_sin = np.zeros((CTX_LEN, LANES))
    return (jnp.asarray(np.concatenate([ctx_cos, cos], axis=0), F32),
            jnp.asarray(np.concatenate([ctx_sin, sin], axis=0), F32))


def _rotate(x, cos, sin, half):
    if 2 * half == LANES:
        partner = pltpu.roll(x, half, 1)
    else:
        lane = lax.broadcasted_iota(jnp.int32, (1, LANES), 1)
        first = (lane % (2 * half)) < half
        partner = jnp.where(first, pltpu.roll(x, LANES - half, 1), pltpu.roll(x, half, 1))
    return x * cos + partner * sin


def _modulation_kernel(c_ref, w_ref, b_ref, o_ref):
    s = jax.nn.silu(c_ref[...]).astype(BF16)
    o_ref[...] = jnp.dot(s, w_ref[...].astype(BF16), preferred_element_type=F32) + b_ref[...]


def _modulation(cvec, w_mod, b_mod):
    tn = 1536
    n = 6 * D_MODEL
    return pl.pallas_call(
        _modulation_kernel,
        grid=(DEPTH, n // tn),
        in_specs=[
            pl.BlockSpec((8, D_MODEL), lambda l, j: (0, 0)),
            pl.BlockSpec((None, D_MODEL, tn), lambda l, j: (l, 0, j)),
            pl.BlockSpec((None, 1, tn), lambda l, j: (l, 0, j)),
        ],
        out_specs=pl.BlockSpec((None, 8, tn), lambda l, j: (l, 0, j)),
        out_shape=jax.ShapeDtypeStruct((DEPTH, 8, n), F32),
        compiler_params=_params(("arbitrary", "arbitrary")),
        name="modulation",
    )(cvec, w_mod, b_mod.reshape(DEPTH, 1, n))


def _segment_of_tile(i):
    tiles = TB // ROW_TILE
    return jnp.where(i % tiles == 0, BATCH, i // tiles)


def _norm_mod_kernel(x_ref, w_ref, mod_ref, o_ref, *, shift_idx):
    x = x_ref[...]
    y = x * lax.rsqrt(jnp.mean(x * x, axis=-1, keepdims=True) + EPS) * w_ref[...]
    shift = mod_ref[shift_idx:shift_idx + 1, :]
    scale = mod_ref[shift_idx + 1:shift_idx + 2, :]
    o_ref[...] = (y * (1.0 + scale) + shift).astype(BF16)


def _norm_mod(h, norm_w, mod3, shift_idx):
    return pl.pallas_call(
        functools.partial(_norm_mod_kernel, shift_idx=shift_idx),
        grid=(M_ROWS // ROW_TILE,),
        in_specs=[
            pl.BlockSpec((ROW_TILE, D_MODEL), lambda i: (i, 0)),
            pl.BlockSpec((1, D_MODEL), lambda i: (0, 0)),
            pl.BlockSpec((None, 6, D_MODEL), lambda i: (_segment_of_tile(i), 0, 0)),
        ],
        out_specs=pl.BlockSpec((ROW_TILE, D_MODEL), lambda i: (i, 0)),
        out_shape=jax.ShapeDtypeStruct((M_ROWS, D_MODEL), BF16),
        compiler_params=_params(("arbitrary",)),
        name="norm_mod",
    )(h, norm_w.reshape(1, D_MODEL), mod3)


def _final_norm_kernel(x_ref, w_ref, o_ref):
    x = x_ref[...]
    o_ref[...] = x * lax.rsqrt(jnp.mean(x * x, axis=-1, keepdims=True) + EPS) * w_ref[...]


def _final_norm(h, norm_w):
    tiles = TB // ROW_TILE
    ctx_tiles = CTX_LEN // ROW_TILE
    return pl.pallas_call(
        _final_norm_kernel,
        grid=(BATCH, SEQ // ROW_TILE),
        in_specs=[
            pl.BlockSpec((ROW_TILE, D_MODEL), lambda b, t: (b * tiles + ctx_tiles + t, 0)),
            pl.BlockSpec((1, D_MODEL), lambda b, t: (0, 0)),
        ],
        out_specs=pl.BlockSpec((None, ROW_TILE, D_MODEL), lambda b, t: (b, t, 0)),
        out_shape=jax.ShapeDtypeStruct((BATCH, SEQ, D_MODEL), F32),
        compiler_params=_params(("arbitrary", "arbitrary")),
        name="final_norm",
    )(h, norm_w.reshape(1, D_MODEL))


def _row_select(tab_ref, row0, tm):
    row = row0 + lax.broadcasted_iota(jnp.int32, (tm, 1), 0)
    second = row >= TB
    is_ctx = (row - jnp.where(second, TB, 0)) < CTX_LEN
    return jnp.where(is_ctx, tab_ref[2:3, :], jnp.where(second, tab_ref[1:2, :], tab_ref[0:1, :]))


def _mm_ws_kernel(*refs, k_sizes, epilogue, norm, tm, valid_cols, cast, n_sub):
    n_a = len(k_sizes)
    a_refs = refs[:n_a]
    w_ref = refs[n_a]
    pos = n_a + 1
    if norm:
        nw_ref = refs[pos]
        pos += 1
    if epilogue in ("gated_residual", "gated_residual_norm"):
        res_ref, gate_ref = refs[pos], refs[pos + 1]
        pos += 2
    if epilogue == "gated_residual_norm":
        n2w_ref, shift_ref, scale_ref = refs[pos:pos + 3]
        pos += 3
    o_ref = refs[pos]
    pos += 1
    if epilogue == "gated_residual_norm":
        n_ref = refs[pos]
        pos += 1
    i = pl.program_id(1)

    if cast:
        wb_ref = refs[pos]

        @pl.when(i == 0)
        def _():
            w = w_ref[...]
            if valid_cols is not None:
                tn = w.shape[1]
                col = pl.program_id(0) * tn + lax.broadcasted_iota(jnp.int32, (1, tn), 1)
                w = jnp.where(col < valid_cols, w, 0.0)
            wb_ref[...] = w.astype(BF16)
    else:
        wb_ref = w_ref

    ts = tm // n_sub
    for sub in range(n_sub):
        rs = slice(sub * ts, (sub + 1) * ts)
        row0 = i * tm + sub * ts
        acc = None
        off = 0
        for a_ref, ks in zip(a_refs, k_sizes):
            a = a_ref[rs, :]
            if norm:
                af = a.astype(F32)
                af = af * lax.rsqrt(jnp.mean(af * af, axis=-1, keepdims=True) + EPS) * nw_ref[...]
                a = af.astype(BF16)
            part = jnp.dot(a, wb_ref[off:off + ks, :], preferred_element_type=F32)
            acc = part if acc is None else acc + part
            off += ks

        if epilogue == "bf16":
            o_ref[rs, :] = acc.astype(BF16)
        elif epilogue == "relu2":
            r = jnp.maximum(acc, 0.0)
            o_ref[rs, :] = (r * r).astype(BF16)
        else:
            second = row0 >= TB
            within = row0 - jnp.where(second, TB, 0)
            one_segment = (within >= CTX_LEN) & (within + ts <= TB)

            def finish(pick, acc=acc, rs=rs):
                h = res_ref[rs, :] + pick(gate_ref) * acc
                o_ref[rs, :] = h
                if epilogue == "gated_residual_norm":
                    y = h * lax.rsqrt(jnp.mean(h * h, axis=-1, keepdims=True) + EPS)
                    n_ref[rs, :] = (y * (n2w_ref[...] * (1.0 + pick(scale_ref))) + pick(shift_ref)).astype(BF16)

            @pl.when(one_segment)
            def _(second=second, finish=finish):
                seg = second.astype(jnp.int32)
                finish(lambda tab_ref: tab_ref[pl.ds(seg, 1), :])

            @pl.when(jnp.logical_not(one_segment))
            def _(row0=row0, finish=finish):
                finish(lambda tab_ref: _row_select(tab_ref, row0, ts))


def _matmul_ws(a_list, w, layer, *, tn, tm, epilogue="bf16", a_col_blocks=None, k_sizes=None, norm_w=None,
               res=None, mod=None, gate_idx=0, next_norm_w=None, n_out=None, n_sub=1, name="matmul"):
    assert BATCH == 2
    _, kdim, n_w = w.shape
    n = n_w if n_out is None else n_out
    if k_sizes is None:
        k_sizes = [a.shape[1] for a in a_list]
    if a_col_blocks is None:
        a_col_blocks = [0] * len(a_list)
    assert sum(k_sizes) == kdim and n % tn == 0 and M_ROWS % tm == 0
    gated = epilogue in ("gated_residual", "gated_residual_norm")
    in_specs = [pl.BlockSpec((tm, ks), functools.partial(lambda j, i, cb: (i, cb), cb=cb))
                for ks, cb in zip(k_sizes, a_col_blocks)]
    single_w = n == tn
    w_mode = dict(pipeline_mode=pl.Buffered(1)) if single_w else {}
    in_specs.append(pl.BlockSpec((None, kdim, tn), lambda j, i: (layer, 0, j), **w_mode))
    operands = list(a_list) + [w]
    if norm_w is not None:
        assert len(a_list) == 1
        in_specs.append(pl.BlockSpec((1, kdim), lambda j, i: (0, 0)))
        operands.append(norm_w.reshape(1, kdim))
    out_specs = pl.BlockSpec((tm, tn), lambda j, i: (i, j))
    out_shape = jax.ShapeDtypeStruct((M_ROWS, n), F32 if gated else BF16)
    if gated:
        blocks_per_vec = D_MODEL // tn
        in_specs.append(pl.BlockSpec((tm, tn), lambda j, i: (i, j)))
        in_specs.append(pl.BlockSpec((8, tn), lambda j, i: (0, gate_idx * blocks_per_vec + j)))
        operands += [res, mod]
    if epilogue == "gated_residual_norm":
        assert tn == n == D_MODEL
        in_specs.append(pl.BlockSpec((1, D_MODEL), lambda j, i: (0, 0)))
        in_specs.append(pl.BlockSpec((8, D_MODEL), lambda j, i: (0, gate_idx + 1)))
        in_specs.append(pl.BlockSpec((8, D_MODEL), lambda j, i: (0, gate_idx + 2)))
        operands += [next_norm_w.reshape(1, D_MODEL), mod, mod]
        out_specs = [out_specs, pl.BlockSpec((tm, tn), lambda j, i: (i, j))]
        out_shape = [out_shape, jax.ShapeDtypeStruct((M_ROWS, n), BF16)]
    cast = w.dtype != BF16
    return pl.pallas_call(
        functools.partial(_mm_ws_kernel, k_sizes=tuple(k_sizes), epilogue=epilogue, norm=norm_w is not None,
                          tm=tm, valid_cols=None if n == n_w else n_w, cast=cast, n_sub=n_sub),
        grid=(n // tn, M_ROWS // tm),
        in_specs=in_specs,
        out_specs=out_specs,
        out_shape=out_shape,
        scratch_shapes=[pltpu.VMEM((kdim, tn), BF16)] if cast else [],
        compiler_params=_params(("arbitrary", "arbitrary")),
        name=name,
    )(*operands)


def _retention_kernel(q_ref, k_ref, v_ref, g_ref, cos_ref, sin_ref, ld_ref, gn_ref, o_ref, kb_ref, kv_ref, st_ref):
    c = RET_CHUNK
    ctx_chunks = CTX_LEN // c
    lat_chunks = SEQ // c
    n_chunks = ctx_chunks + lat_chunks
    sc = HEAD_DIM ** -0.5
    lg = jnp.log1p(-jnp.exp2(ld_ref[...]))
    lgf = lg[0:1, 0:1]
    lgb = lg[1:2, 0:1]
    ii = lax.broadcasted_iota(jnp.int32, (c, 1), 0).astype(F32)
    jj = lax.broadcasted_iota(jnp.int32, (1, c), 1).astype(F32)
    diff = ii - jj
    dmat = jnp.where(diff >= 0, jnp.exp(lgf * jnp.maximum(diff, 0.0)), jnp.exp(lgb * jnp.maximum(-diff, 0.0)))
    qf_dec = jnp.exp(lgf * (ii + 1.0))
    qb_dec = jnp.exp(lgb * (c - ii))
    kf_dec = jnp.exp(lgf * (c - 1.0 - ii))
    kb_dec = jnp.exp(lgb * ii)
    gcf = jnp.exp(lgf * c)
    gcb = jnp.exp(lgb * c)

    def rows(n):
        return pl.ds(pl.multiple_of(n * c, c), c)

    def roped(ref, n):
        x = ref[rows(n), :].astype(F32)
        return _rotate(x, cos_ref[rows(n), :], sin_ref[rows(n), :], HEAD_DIM // 2)

    def increments(n, carry):
        k = roped(k_ref, n)
        kb_ref[rows(n), :] = k.astype(BF16)
        kd = jnp.concatenate([(k * kf_dec).astype(BF16), (k * kb_dec).astype(BF16)], axis=1)
        kv_ref[n] = lax.dot_general(kd, v_ref[rows(n), :], TN_DIMS, preferred_element_type=F32)
        return carry

    def recurrence(n0, cnt, s_f, s_b):
        def body(t, carry):
            s_f, s_b = carry
            nf = n0 + t
            nb = n0 + cnt - 1 - t
            st_ref[nf, 0:c, :] = s_f.astype(BF16)
            st_ref[nb, c:2 * c, :] = s_b.astype(BF16)
            return gcf * s_f + kv_ref[nf, 0:c, :], gcb * s_b + kv_ref[nb, c:2 * c, :]
        return lax.fori_loop(0, cnt, body, (s_f, s_b))

    def outputs(n, carry):
        q = roped(q_ref, n) * sc
        v = v_ref[rows(n), :]
        scores = lax.dot_general(q.astype(BF16), kb_ref[rows(n), :], NT_DIMS, preferred_element_type=F32) * dmat
        qd = jnp.concatenate([(q * qf_dec).astype(BF16), (q * qb_dec).astype(BF16)], axis=1)
        o = (jnp.dot(scores.astype(BF16), v, preferred_element_type=F32)
             + jnp.dot(qd, st_ref[n], preferred_element_type=F32))
        mu = jnp.mean(o, axis=-1, keepdims=True)
        oc = o - mu
        var = jnp.mean(oc * oc, axis=-1, keepdims=True)
        on = oc * lax.rsqrt(var + EPS) * gn_ref[...]
        g = g_ref[rows(n), :].astype(F32)
        o_ref[rows(n), :] = (jax.nn.silu(g) * on).astype(BF16)
        return carry

    lax.fori_loop(0, n_chunks, increments, 0, unroll=RET_UNROLL)
    zero = jnp.zeros((c, c), F32)
    s_f, s_b = recurrence(0, ctx_chunks, zero, zero)
    recurrence(ctx_chunks, lat_chunks, s_f, s_b)
    lax.fori_loop(0, n_chunks, outputs, 0, unroll=RET_UNROLL)


def _retention(proj, log2_decay, gn_w, cos, sin):
    h = RET_HEADS
    ld = jnp.broadcast_to(log2_decay.T[:, :, None], (h, 2, LANES))
    seq_spec = lambda cb0: pl.BlockSpec((TB, HEAD_DIM), functools.partial(lambda b, hh, cb0: (b, cb0 + hh), cb0=cb0))
    tab_spec = pl.BlockSpec((TB, LANES), lambda b, hh: (0, 0))
    return pl.pallas_call(
        _retention_kernel,
        grid=(BATCH, h),
        in_specs=[seq_spec(0), seq_spec(h), seq_spec(2 * h), seq_spec(3 * h), tab_spec, tab_spec,
                  pl.BlockSpec((None, 2, LANES), lambda b, hh: (hh, 0, 0)),
                  pl.BlockSpec((1, HEAD_DIM), lambda b, hh: (0, hh))],
        out_specs=pl.BlockSpec((TB, HEAD_DIM), lambda b, hh: (b, hh)),
        out_shape=jax.ShapeDtypeStruct((M_ROWS, h * HEAD_DIM), BF16),
        scratch_shapes=[pltpu.VMEM((TB, HEAD_DIM), BF16),
                        pltpu.VMEM((TB // RET_CHUNK, 2 * RET_CHUNK, HEAD_DIM), F32),
                        pltpu.VMEM((TB // RET_CHUNK, 2 * RET_CHUNK, HEAD_DIM), BF16)],
        compiler_params=_params(("arbitrary", "arbitrary")),
        name="retention",
    )(proj, proj, proj, proj, cos, sin, ld, gn_w.reshape(1, h * HEAD_DIM))


def _window_kernel(sink_ref, q_ref, k_ref, v_ref, cos_ref, sin_ref, o_ref, kr_ref, vx_ref):
    blk = WIN_BLOCK
    ctx_chunks = CTX_LEN // blk
    n_chunks = TB // blk
    kvh = pl.program_id(1)
    sc = HEAD_DIM ** -0.5
    half = HEAD_DIM // 4
    rows_q = WIN_GROUP * blk

    kr_ref[...] = _rotate(k_ref[...].astype(F32), cos_ref[...], sin_ref[...], half).astype(BF16)
    vx_ref[:, 0:HEAD_DIM] = v_ref[...]
    vx_ref[:, HEAD_DIM:2 * HEAD_DIM] = jnp.ones((TB, HEAD_DIM), BF16)

    row_iota = lax.broadcasted_iota(jnp.int32, (rows_q, 1), 0)
    head = row_iota // blk
    sink = jnp.zeros((rows_q, 1), F32)
    for g in range(WIN_GROUP):
        sink = jnp.where(head == g, sink_ref[kvh * WIN_GROUP + g], sink)

    def chunk(n, carry):
        rq = pl.ds(pl.multiple_of(n * blk, blk), blk)
        cos = cos_ref[rq, :]
        sin = sin_ref[rq, :]
        q = jnp.concatenate(
            [(_rotate(q_ref[rq, g * HEAD_DIM:(g + 1) * HEAD_DIM].astype(F32), cos, sin, half) * sc).astype(BF16)
             for g in range(WIN_GROUP)], axis=0)

        cstart = jnp.clip(n - 1, ctx_chunks, n_chunks - 3)
        rw = pl.ds(pl.multiple_of(cstart * blk, blk), 3 * blk)
        s_c = lax.dot_general(q, kr_ref[0:CTX_LEN, :], NT_DIMS, preferred_element_type=F32)
        s_w = lax.dot_general(q, kr_ref[rw, :], NT_DIMS, preferred_element_type=F32)
        qpos = (n - ctx_chunks) * blk + lax.broadcasted_iota(jnp.int32, (blk, 1), 0)
        kpos = (cstart - ctx_chunks) * blk + lax.broadcasted_iota(jnp.int32, (1, 3 * blk), 1)
        valid = (jnp.abs(qpos - kpos) <= WIN_RADIUS) & (n >= ctx_chunks)
        s_w = jnp.concatenate([jnp.where(valid, s_w[g * blk:(g + 1) * blk, :], NEG_INF)
                               for g in range(WIN_GROUP)], axis=0)

        m = jnp.maximum(jnp.maximum(jnp.max(s_c, axis=-1, keepdims=True), jnp.max(s_w, axis=-1, keepdims=True)),
                        sink)
        p_c = jnp.exp(s_c - m).astype(BF16)
        p_w = jnp.exp(s_w - m).astype(BF16)
        ox = (jnp.dot(p_c, vx_ref[0:CTX_LEN, :], preferred_element_type=F32)
              + jnp.dot(p_w, vx_ref[rw, :], preferred_element_type=F32))
        o = ox[:, 0:HEAD_DIM] / (ox[:, HEAD_DIM:2 * HEAD_DIM] + jnp.exp(sink - m))
        for g in range(WIN_GROUP):
            o_ref[rq, g * HEAD_DIM:(g + 1) * HEAD_DIM] = o[g * blk:(g + 1) * blk, :].astype(BF16)
        return carry

    lax.fori_loop(0, n_chunks, chunk, 0, unroll=2)


def _window_attention(proj, sink, cos, sin):
    gw = WIN_GROUP * HEAD_DIM
    q_cb0 = 4 * RET_HEADS * HEAD_DIM // gw
    k_cb0 = (4 * RET_HEADS + WIN_HEADS) * HEAD_DIM // HEAD_DIM
    v_cb0 = k_cb0 + WIN_KV_HEADS
    tab_spec = pl.BlockSpec((TB, LANES), lambda b, kh: (0, 0))
    return pl.pallas_call(
        _window_kernel,
        grid=(BATCH, WIN_KV_HEADS),
        in_specs=[
            pl.BlockSpec(memory_space=pltpu.SMEM),
            pl.BlockSpec((TB, gw), lambda b, kh: (b, q_cb0 + kh)),
            pl.BlockSpec((TB, HEAD_DIM), lambda b, kh: (b, k_cb0 + kh)),
            pl.BlockSpec((TB, HEAD_DIM), lambda b, kh: (b, v_cb0 + kh)),
            tab_spec, tab_spec,
        ],
        out_specs=pl.BlockSpec((TB, gw), lambda b, kh: (b, kh)),
        out_shape=jax.ShapeDtypeStruct((M_ROWS, WIN_HEADS * HEAD_DIM), BF16),
        scratch_shapes=[pltpu.VMEM((TB, HEAD_DIM), BF16), pltpu.VMEM((TB, 2 * HEAD_DIM), BF16)],
        compiler_params=_params(("arbitrary", "arbitrary")),
        name="window_attention",
    )(sink, proj, proj, proj, cos, sin)


def _conv_kernel(b_ref, c_ref, x_ref, w_ref, o_ref):
    u = c_ref[...].astype(F32) * x_ref[...].astype(F32)
    row = lax.broadcasted_iota(jnp.int32, (TB, 1), 0)
    prev = jnp.where((row == 0) | (row == CTX_LEN), 0.0, pltpu.roll(u, 1, 0))
    nxt = jnp.where((row == CTX_LEN - 1) | (row == TB - 1), 0.0, pltpu.roll(u, TB - 1, 0))
    z = prev * w_ref[0:1, :] + u * w_ref[1:2, :] + nxt * w_ref[2:3, :]
    o_ref[...] = (b_ref[...].astype(F32) * z).astype(BF16)


def _short_conv(proj, conv_w):
    tc = 256
    nblk = CONV_WIDTH // tc
    spec = lambda part: pl.BlockSpec((TB, tc), functools.partial(lambda b, j, part: (b, part * nblk + j), part=part))
    return pl.pallas_call(
        _conv_kernel,
        grid=(BATCH, nblk),
        in_specs=[spec(0), spec(1), spec(2), pl.BlockSpec((3, tc), lambda b, j: (0, j))],
        out_specs=pl.BlockSpec((TB, tc), lambda b, j: (b, j)),
        out_shape=jax.ShapeDtypeStruct((M_ROWS, CONV_WIDTH), BF16),
        compiler_params=_params(("arbitrary", "arbitrary")),
        name="short_conv",
    )(proj, proj, proj, conv_w)


MLA_HEADS_PER_STEP = 2
MLA_Q_TILE = 512
MLA_KEY_CHUNK = 1024


def _mla_kernel(q_ref, kv_ref, kr_ref, cos_ref, sin_ref, o_ref, kc_ref, vx_ref):
    sc = (MLA_NOPE + MLA_ROPE) ** -0.5
    half = MLA_ROPE // 4
    kv_w = MLA_NOPE + MLA_V

    kr = _rotate(kr_ref[...].astype(F32), cos_ref[...], sin_ref[...], half).astype(BF16)
    for hh in range(MLA_HEADS_PER_STEP):
        kc_ref[hh, :, 0:MLA_NOPE] = kv_ref[:, hh * kv_w:hh * kv_w + MLA_NOPE]
        kc_ref[hh, :, MLA_NOPE:MLA_QK_PAD] = kr
        vx_ref[hh, :, 0:MLA_V] = kv_ref[:, hh * kv_w + MLA_NOPE:(hh + 1) * kv_w]
        vx_ref[hh, :, MLA_V:2 * MLA_V] = jnp.ones((TB, MLA_V), BF16)

    def attend(row0, n_rows, key_chunks):
        rq = pl.ds(row0, n_rows)
        cos = cos_ref[rq, :]
        sin = sin_ref[rq, :]
        for hh in range(MLA_HEADS_PER_STEP):
            c0 = hh * MLA_QK_PAD
            qn = (q_ref[rq, c0:c0 + MLA_NOPE].astype(F32) * sc).astype(BF16)
            qr = (_rotate(q_ref[rq, c0 + MLA_NOPE:c0 + MLA_QK_PAD].astype(F32), cos, sin, half) * sc).astype(BF16)
            q = jnp.concatenate([qn, qr], axis=1)
            m = None
            acc = None
            for k0, kn in key_chunks:
                s = lax.dot_general(q, kc_ref[hh, k0:k0 + kn, :], NT_DIMS, preferred_element_type=F32)
                m_chunk = jnp.max(s, axis=-1, keepdims=True)
                m_new = m_chunk if m is None else jnp.maximum(m, m_chunk)
                p = jnp.exp(s - m_new).astype(BF16)
                pv = jnp.dot(p, vx_ref[hh, k0:k0 + kn, :], preferred_element_type=F32)
                acc = pv if acc is None else acc * jnp.exp(m - m_new) + pv
                m = m_new
            o_ref[rq, hh * MLA_V:(hh + 1) * MLA_V] = (acc[:, 0:MLA_V] / acc[:, MLA_V:2 * MLA_V]).astype(BF16)

    ctx_keys = [(0, CTX_LEN)]
    all_keys = ctx_keys + [(CTX_LEN + k * MLA_KEY_CHUNK, MLA_KEY_CHUNK) for k in range(SEQ // MLA_KEY_CHUNK)]
    attend(0, CTX_LEN, ctx_keys)

    def latent_tile(t, carry):
        attend(pl.multiple_of(CTX_LEN + t * MLA_Q_TILE, ROW_TILE), MLA_Q_TILE, all_keys)
        return carry

    lax.fori_loop(0, SEQ // MLA_Q_TILE, latent_tile, 0)


def _mla_attention(qcat, kv, proj, cos, sin):
    hps = MLA_HEADS_PER_STEP
    kr_cb = (3 * CONV_WIDTH + MLA_Q_RANK + MLA_KV_RANK) // LANES
    tab_spec = pl.BlockSpec((TB, LANES), lambda b, h: (0, 0))
    return pl.pallas_call(
        _mla_kernel,
        grid=(BATCH, MLA_HEADS // hps),
        in_specs=[
            pl.BlockSpec((TB, hps * MLA_QK_PAD), lambda b, h: (b, h)),
            pl.BlockSpec((TB, hps * (MLA_NOPE + MLA_V)), lambda b, h: (b, h)),
            pl.BlockSpec((TB, LANES), lambda b, h: (b, kr_cb)),
            tab_spec, tab_spec,
        ],
        out_specs=pl.BlockSpec((TB, hps * MLA_V), lambda b, h: (b, h)),
        out_shape=jax.ShapeDtypeStruct((M_ROWS, MLA_HEADS * MLA_V), BF16),
        scratch_shapes=[pltpu.VMEM((hps, TB, MLA_QK_PAD), BF16), pltpu.VMEM((hps, TB, 2 * MLA_V), BF16)],
        compiler_params=_params(("arbitrary", "arbitrary")),
        name="mla_attention",
    )(qcat, kv, proj, cos, sin)


def kernel(x, c, ctx, c_ctx, w_mod, b_mod, norm1_w, norm2_w, mlp_w1, mlp_w2, ev_w_in, ev_ret_log2_decay,
           ev_ret_gn_w, ev_sink, ev_w_out, od_w_in, od_conv_w, od_q_norm_w, od_kv_norm_w, od_w_uq, od_w_ukv,
           od_w_out, norm_f):
    assert x.shape == (BATCH, SEQ, D_MODEL) and ctx.shape == (BATCH, CTX_LEN, D_MODEL)
    rows = SEQ // GRID_W
    row = np.repeat(np.arange(rows), GRID_W)
    col = np.tile(np.arange(GRID_W), rows)
    pos = np.arange(SEQ)
    ret_cos, ret_sin = _rope_tables([pos], HEAD_DIM)
    win_cos, win_sin = _rope_tables([row, col], HEAD_DIM)
    mla_cos, mla_sin = _rope_tables([row, col], MLA_ROPE)

    cvec = jnp.concatenate([c, c_ctx[None], jnp.zeros((8 - BATCH - 1, D_MODEL), F32)], axis=0)
    mod_all = _modulation(cvec, w_mod, b_mod)

    h = jnp.concatenate([ctx, x], axis=1).reshape(M_ROWS, D_MODEL)

    n_odd = od_w_uq.shape[0]
    w_uq = jnp.pad(od_w_uq.reshape(n_odd, MLA_Q_RANK, MLA_HEADS, MLA_NOPE + MLA_ROPE),
                   ((0, 0), (0, 0), (0, 0), (0, MLA_QK_PAD - MLA_NOPE - MLA_ROPE))
                   ).reshape(n_odd, MLA_Q_RANK, MLA_HEADS * MLA_QK_PAD)

    w2_bf16 = mlp_w2.astype(BF16)

    for layer in range(DEPTH):
        j = layer // 2
        mod2 = mod_all[layer]
        mod3 = mod2.reshape(8, 6, D_MODEL)
        n1 = _norm_mod(h, norm1_w[layer], mod3, 0)
        if layer % 2 == 0:
            proj = _matmul_ws([n1], ev_w_in, j, tn=EVEN_IN // 4, tm=MM_TM_SMALL, name="even_in_proj")
            mix_a = _retention(proj, ev_ret_log2_decay[j], ev_ret_gn_w[j], ret_cos, ret_sin)
            mix_b = _window_attention(proj, ev_sink[j], win_cos, win_sin)
            w_out = ev_w_out
        else:
            proj = _matmul_ws([n1], od_w_in, j, tn=ODD_IN_PAD // 4, tm=MM_TM_BIG, n_out=ODD_IN_PAD,
                              name="odd_in_proj")
            mix_a = _short_conv(proj, od_conv_w[j])
            qcat = _matmul_ws([proj], w_uq, j, tn=MLA_HEADS * MLA_QK_PAD, tm=MM_TM_BIG, k_sizes=[MLA_Q_RANK],
                              a_col_blocks=[3 * CONV_WIDTH // MLA_Q_RANK], norm_w=od_q_norm_w[j], name="mla_q_up")
            kv = _matmul_ws([proj], od_w_ukv, j, tn=MLA_HEADS * (MLA_NOPE + MLA_V), tm=MM_TM_BIG,
                            k_sizes=[MLA_KV_RANK], a_col_blocks=[(3 * CONV_WIDTH + MLA_Q_RANK) // MLA_KV_RANK],
                            norm_w=od_kv_norm_w[j], name="mla_kv_up")
            mix_b = _mla_attention(qcat, kv, proj, mla_cos, mla_sin)
            w_out = od_w_out
        h, n2 = _matmul_ws([mix_a, mix_b], w_out, j, tn=D_MODEL, tm=MM_TM_SMALL, epilogue="gated_residual_norm",
                           res=h, mod=mod2, gate_idx=2, next_norm_w=norm2_w[layer], n_sub=2, name="mixer_out_proj")
        ff = _matmul_ws([n2], mlp_w1, layer, tn=1024, tm=MM_TM_BIG, epilogue="relu2", name="mlp_up")
        h = _matmul_ws([ff], w2_bf16, layer, tn=512, tm=MM_TM_DOWN, epilogue="gated_residual",
                       res=h, mod=mod2, gate_idx=5, name="mlp_down")
    return _final_norm(h, norm_f)
```

```python
import functools

import numpy as np
import jax
import jax.numpy as jnp
from jax import lax
from jax.experimental import pallas as pl
from jax.experimental.pallas import tpu as pltpu

F32 = jnp.float32
BF16 = jnp.bfloat16

D_MODEL = 2048
BATCH = 2
SEQ = 4096
DEPTH = 4
GRID_W = 64
CTX_LEN = 256
HEAD_DIM = 128
ROPE_BASE = 10000.0
EPS = 1e-6
NEG_INF = -1e30

RET_HEADS = 8
RET_CHUNK = 128
RET_UNROLL = 8
WIN_HEADS = 8
WIN_KV_HEADS = 2
WIN_GROUP = WIN_HEADS // WIN_KV_HEADS
WIN_RADIUS = 128
WIN_BLOCK = 128
CONV_WIDTH = 1024
MLA_HEADS = 8
MLA_Q_RANK = 512
MLA_KV_RANK = 256
MLA_NOPE = 128
MLA_ROPE = 64
MLA_V = 128
D_FF = 4 * D_MODEL

EVEN_IN = 4 * RET_HEADS * HEAD_DIM + (WIN_HEADS + 2 * WIN_KV_HEADS) * HEAD_DIM
ODD_IN = 3 * CONV_WIDTH + MLA_Q_RANK + MLA_KV_RANK + MLA_ROPE
ODD_IN_PAD = 4096
MLA_QK_PAD = 256

TB = CTX_LEN + SEQ
M_ROWS = BATCH * TB
LANES = 128
ROW_TILE = 256
MM_TM_BIG = M_ROWS // 8
MM_TM_SMALL = M_ROWS // 16
MM_TM_DOWN = 512
VMEM_LIMIT = 60 * 1024 * 1024

NT_DIMS = (((1,), (1,)), ((), ()))
TN_DIMS = (((0,), (0,)), ((), ()))


def _params(sem, vmem=VMEM_LIMIT):
    return pltpu.CompilerParams(dimension_semantics=sem, vmem_limit_bytes=vmem)


def _rope_tables(pos_list, d_rot):
    per = d_rot // len(pos_list)
    inv = ROPE_BASE ** (-np.arange(0, per, 2, dtype=np.float64) / per)
    cos_parts, sin_parts = [], []
    for pos in pos_list:
        ang = pos.astype(np.float64)[:, None] * inv[None, :]
        c, s = np.cos(ang), np.sin(ang)
        cos_parts += [c, c]
        sin_parts += [-s, s]
    pad = ((0, 0), (0, LANES - d_rot))
    cos = np.pad(np.concatenate(cos_parts, axis=-1), pad)
    sin = np.pad(np.concatenate(sin_parts, axis=-1), pad)
    ctx_cos = np.pad(np.ones((CTX_LEN, d_rot)), pad)
    ctx_sin = np.zeros((CTX_LEN, LANES))
    return (jnp.asarray(np.concatenate([ctx_cos, cos], axis=0), F32),
            jnp.asarray(np.concatenate([ctx_sin, sin], axis=0), F32))


def _rotate(x, cos, sin, half):
    if 2 * half == LANES:
        partner = pltpu.roll(x, half, 1)
    else:
        lane = lax.broadcasted_iota(jnp.int32, (1, LANES), 1)
        first = (lane % (2 * half)) < half
        partner = jnp.where(first, pltpu.roll(x, LANES - half, 1), pltpu.roll(x, half, 1))
    return x * cos + partner * sin


def _modulation_kernel(c_ref, w_ref, b_ref, o_ref):
    s = jax.nn.silu(c_ref[...]).astype(BF16)
    o_ref[...] = jnp.dot(s, w_ref[...].astype(BF16), preferred_element_type=F32) + b_ref[...]


def _modulation(cvec, w_mod, b_mod):
    tn = 1536
    n = 6 * D_MODEL
    return pl.pallas_call(
        _modulation_kernel,
        grid=(DEPTH, n // tn),
        in_specs=[
            pl.BlockSpec((8, D_MODEL), lambda l, j: (0, 0)),
            pl.BlockSpec((None, D_MODEL, tn), lambda l, j: (l, 0, j)),
            pl.BlockSpec((None, 1, tn), lambda l, j: (l, 0, j)),
        ],
        out_specs=pl.BlockSpec((None, 8, tn), lambda l, j: (l, 0, j)),
        out_shape=jax.ShapeDtypeStruct((DEPTH, 8, n), F32),
        compiler_params=_params(("arbitrary", "arbitrary")),
        name="modulation",
    )(cvec, w_mod, b_mod.reshape(DEPTH, 1, n))


def _segment_of_tile(i):
    tiles = TB // ROW_TILE
    return jnp.where(i % tiles == 0, BATCH, i // tiles)


def _embed_norm_kernel(x_ref, ctx_ref, w_ref, mod_ref, h_ref, n_ref):
    is_ctx = pl.program_id(0) % (TB // ROW_TILE) == 0
    x = jnp.where(is_ctx, ctx_ref[...], x_ref[...])
    h_ref[...] = x
    y = x * lax.rsqrt(jnp.mean(x * x, axis=-1, keepdims=True) + EPS) * w_ref[...]
    n_ref[...] = (y * (1.0 + mod_ref[1:2, :]) + mod_ref[0:1, :]).astype(BF16)


def _embed_norm(x, ctx, norm_w, mod3):
    assert CTX_LEN == ROW_TILE
    tiles = TB // ROW_TILE
    row_spec = pl.BlockSpec((ROW_TILE, D_MODEL), lambda i: (i, 0))
    return pl.pallas_call(
        _embed_norm_kernel,
        grid=(M_ROWS // ROW_TILE,),
        in_specs=[
            pl.BlockSpec((None, ROW_TILE, D_MODEL), lambda i: (i // tiles, jnp.maximum(i % tiles - 1, 0), 0)),
            pl.BlockSpec((None, ROW_TILE, D_MODEL), lambda i: (i // tiles, 0, 0)),
            pl.BlockSpec((1, D_MODEL), lambda i: (0, 0)),
            pl.BlockSpec((None, 6, D_MODEL), lambda i: (_segment_of_tile(i), 0, 0)),
        ],
        out_specs=[row_spec, row_spec],
        out_shape=[jax.ShapeDtypeStruct((M_ROWS, D_MODEL), F32), jax.ShapeDtypeStruct((M_ROWS, D_MODEL), BF16)],
        compiler_params=_params(("arbitrary",)),
        name="embed_norm",
    )(x, ctx, norm_w.reshape(1, D_MODEL), mod3)


def _norm_mod_kernel(x_ref, w_ref, shift_ref, scale_ref, o_ref, *, tm):
    row0 = pl.program_id(0) * tm
    second = row0 >= TB
    within = row0 - jnp.where(second, TB, 0)
    one_segment = (within >= CTX_LEN) & (within + tm <= TB)
    x = x_ref[...]
    y = x * lax.rsqrt(jnp.mean(x * x, axis=-1, keepdims=True) + EPS)

    def finish(pick):
        o_ref[...] = (y * (w_ref[...] * (1.0 + pick(scale_ref))) + pick(shift_ref)).astype(BF16)

    @pl.when(one_segment)
    def _():
        seg = second.astype(jnp.int32)
        finish(lambda tab_ref: tab_ref[pl.ds(seg, 1), :])

    @pl.when(jnp.logical_not(one_segment))
    def _():
        finish(lambda tab_ref: _row_select(tab_ref, row0, tm))


def _norm_mod(h, norm_w, mod, shift_idx):
    tm = MM_TM_SMALL
    return pl.pallas_call(
        functools.partial(_norm_mod_kernel, tm=tm),
        grid=(M_ROWS // tm,),
        in_specs=[
            pl.BlockSpec((tm, D_MODEL), lambda i: (i, 0)),
            pl.BlockSpec((1, D_MODEL), lambda i: (0, 0)),
            pl.BlockSpec((8, D_MODEL), lambda i: (0, shift_idx)),
            pl.BlockSpec((8, D_MODEL), lambda i: (0, shift_idx + 1)),
        ],
        out_specs=pl.BlockSpec((tm, D_MODEL), lambda i: (i, 0)),
        out_shape=jax.ShapeDtypeStruct((M_ROWS, D_MODEL), BF16),
        compiler_params=_params(("arbitrary",)),
        name="norm_mod",
    )(h, norm_w.reshape(1, D_MODEL), mod, mod)


def _final_norm_kernel(x_ref, w_ref, o_ref):
    x = x_ref[...]
    o_ref[...] = x * lax.rsqrt(jnp.mean(x * x, axis=-1, keepdims=True) + EPS) * w_ref[...]


def _final_norm(h, norm_w):
    tiles = TB // ROW_TILE
    ctx_tiles = CTX_LEN // ROW_TILE
    return pl.pallas_call(
        _final_norm_kernel,
        grid=(BATCH, SEQ // ROW_TILE),
        in_specs=[
            pl.BlockSpec((ROW_TILE, D_MODEL), lambda b, t: (b * tiles + ctx_tiles + t, 0)),
            pl.BlockSpec((1, D_MODEL), lambda b, t: (0, 0)),
        ],
        out_specs=pl.BlockSpec((None, ROW_TILE, D_MODEL), lambda b, t: (b, t, 0)),
        out_shape=jax.ShapeDtypeStruct((BATCH, SEQ, D_MODEL), F32),
        compiler_params=_params(("arbitrary", "arbitrary")),
        name="final_norm",
    )(h, norm_w.reshape(1, D_MODEL))


def _row_select(tab_ref, row0, tm):
    row = row0 + lax.broadcasted_iota(jnp.int32, (tm, 1), 0)
    second = row >= TB
    is_ctx = (row - jnp.where(second, TB, 0)) < CTX_LEN
    return jnp.where(is_ctx, tab_ref[2:3, :], jnp.where(second, tab_ref[1:2, :], tab_ref[0:1, :]))


def _mm_ws_kernel(*refs, k_sizes, epilogue, norm, tm, valid_cols, cast, n_sub, side_cast):
    n_a = len(k_sizes)
    a_refs = refs[:n_a]
    w_ref = refs[n_a]
    pos = n_a + 1
    if norm:
        nw_ref = refs[pos]
        pos += 1
    if epilogue in ("gated_residual", "gated_residual_norm"):
        res_ref, gate_ref = refs[pos], refs[pos + 1]
        pos += 2
    if epilogue == "gated_residual_norm":
        n2w_ref, shift_ref, scale_ref = refs[pos:pos + 3]
        pos += 3
    if side_cast:
        side_in_ref = refs[pos]
        pos += 1
    o_ref = refs[pos]
    pos += 1
    if epilogue == "gated_residual_norm":
        n_ref = refs[pos]
        pos += 1
    if side_cast:
        refs[pos][...] = side_in_ref[...].astype(BF16)
        pos += 1
    i = pl.program_id(1)

    if cast:
        wb_ref = refs[pos]

        @pl.when(i == 0)
        def _():
            w = w_ref[...]
            if valid_cols is not None:
                tn = w.shape[1]
                col = pl.program_id(0) * tn + lax.broadcasted_iota(jnp.int32, (1, tn), 1)
                w = jnp.where(col < valid_cols, w, 0.0)
            wb_ref[...] = w.astype(BF16)
    else:
        wb_ref = w_ref

    ts = tm // n_sub
    for sub in range(n_sub):
        rs = slice(sub * ts, (sub + 1) * ts)
        row0 = i * tm + sub * ts
        acc = None
        off = 0
        for a_ref, ks in zip(a_refs, k_sizes):
            a = a_ref[rs, :]
            if norm:
                af = a.astype(F32)
                af = af * lax.rsqrt(jnp.mean(af * af, axis=-1, keepdims=True) + EPS) * nw_ref[...]
                a = af.astype(BF16)
            part = jnp.dot(a, wb_ref[off:off + ks, :], preferred_element_type=F32)
            acc = part if acc is None else acc + part
            off += ks

        if epilogue == "bf16":
            o_ref[rs, :] = acc.astype(BF16)
        elif epilogue == "relu2":
            r = jnp.maximum(acc, 0.0)
            o_ref[rs, :] = (r * r).astype(BF16)
        else:
            second = row0 >= TB
            within = row0 - jnp.where(second, TB, 0)
            one_segment = (within >= CTX_LEN) & (within + ts <= TB)

            def finish(pick, acc=acc, rs=rs):
                h = res_ref[rs, :] + pick(gate_ref) * acc
                o_ref[rs, :] = h
                if epilogue == "gated_residual_norm":
                    y = h * lax.rsqrt(jnp.mean(h * h, axis=-1, keepdims=True) + EPS)
                    n_ref[rs, :] = (y * (n2w_ref[...] * (1.0 + pick(scale_ref))) + pick(shift_ref)).astype(BF16)

            @pl.when(one_segment)
            def _(second=second, finish=finish):
                seg = second.astype(jnp.int32)
                finish(lambda tab_ref: tab_ref[pl.ds(seg, 1), :])

            @pl.when(jnp.logical_not(one_segment))
            def _(row0=row0, finish=finish):
                finish(lambda tab_ref: _row_select(tab_ref, row0, ts))


def _matmul_ws(a_list, w, layer, *, tn, tm, epilogue="bf16", a_col_blocks=None, k_sizes=None, norm_w=None,
               res=None, mod=None, gate_idx=0, next_norm_w=None, n_out=None, n_sub=1, side_cast=None, name="matmul"):
    assert BATCH == 2
    _, kdim, n_w = w.shape
    n = n_w if n_out is None else n_out
    if k_sizes is None:
        k_sizes = [a.shape[1] for a in a_list]
    if a_col_blocks is None:
        a_col_blocks = [0] * len(a_list)
    assert sum(k_sizes) == kdim and n % tn == 0 and M_ROWS % tm == 0
    gated = epilogue in ("gated_residual", "gated_residual_norm")
    in_specs = [pl.BlockSpec((tm, ks), functools.partial(lambda j, i, cb: (i, cb), cb=cb))
                for ks, cb in zip(k_sizes, a_col_blocks)]
    single_w = n == tn
    w_mode = dict(pipeline_mode=pl.Buffered(1)) if single_w else {}
    in_specs.append(pl.BlockSpec((None, kdim, tn), lambda j, i: (layer, 0, j), **w_mode))
    operands = list(a_list) + [w]
    if norm_w is not None:
        assert len(a_list) == 1
        in_specs.append(pl.BlockSpec((1, kdim), lambda j, i: (0, 0)))
        operands.append(norm_w.reshape(1, kdim))
    out_specs = pl.BlockSpec((tm, tn), lambda j, i: (i, j))
    out_shape = jax.ShapeDtypeStruct((M_ROWS, n), F32 if gated else BF16)
    if gated:
        blocks_per_vec = D_MODEL // tn
        in_specs.append(pl.BlockSpec((tm, tn), lambda j, i: (i, j)))
        in_specs.append(pl.BlockSpec((8, tn), lambda j, i: (0, gate_idx * blocks_per_vec + j)))
        operands += [res, mod]
    if epilogue == "gated_residual_norm":
        assert tn == n == D_MODEL
        in_specs.append(pl.BlockSpec((1, D_MODEL), lambda j, i: (0, 0)))
        in_specs.append(pl.BlockSpec((8, D_MODEL), lambda j, i: (0, gate_idx + 1)))
        in_specs.append(pl.BlockSpec((8, D_MODEL), lambda j, i: (0, gate_idx + 2)))
        operands += [next_norm_w.reshape(1, D_MODEL), mod, mod]
        out_specs = [out_specs, pl.BlockSpec((tm, tn), lambda j, i: (i, j))]
        out_shape = [out_shape, jax.ShapeDtypeStruct((M_ROWS, n), BF16)]
    if side_cast is not None:
        side_w, side_layer = side_cast
        _, side_k, side_n = side_w.shape
        row_tiles = M_ROWS // tm
        slab = side_k // ((n // tn) * row_tiles)
        assert slab * (n // tn) * row_tiles == side_k and not isinstance(out_specs, list)
        in_specs.append(pl.BlockSpec((None, slab, side_n), lambda j, i: (side_layer, j * row_tiles + i, 0)))
        operands.append(side_w)
        out_specs = [out_specs, pl.BlockSpec((slab, side_n), lambda j, i: (j * row_tiles + i, 0))]
        out_shape = [out_shape, jax.ShapeDtypeStruct((side_k, side_n), BF16)]
    cast = w.dtype != BF16
    return pl.pallas_call(
        functools.partial(_mm_ws_kernel, k_sizes=tuple(k_sizes), epilogue=epilogue, norm=norm_w is not None,
                          tm=tm, valid_cols=None if n == n_w else n_w, cast=cast, n_sub=n_sub,
                          side_cast=side_cast is not None),
        grid=(n // tn, M_ROWS // tm),
        in_specs=in_specs,
        out_specs=out_specs,
        out_shape=out_shape,
        scratch_shapes=[pltpu.VMEM((kdim, tn), BF16)] if cast else [],
        compiler_params=_params(("arbitrary", "arbitrary")),
        name=name,
    )(*operands)


def _retention_kernel(q_ref, k_ref, v_ref, g_ref, cos_ref, sin_ref, ld_ref, gn_ref, o_ref, kb_ref, kv_ref, st_ref):
    c = RET_CHUNK
    ctx_chunks = CTX_LEN // c
    lat_chunks = SEQ // c
    n_chunks = ctx_chunks + lat_chunks
    sc = HEAD_DIM ** -0.5
    lg = jnp.log1p(-jnp.exp2(ld_ref[...]))
    lgf = lg[0:1, 0:1]
    lgb = lg[1:2, 0:1]
    ii = lax.broadcasted_iota(jnp.int32, (c, 1), 0).astype(F32)
    jj = lax.broadcasted_iota(jnp.int32, (1, c), 1).astype(F32)
    diff = ii - jj
    dmat = jnp.where(diff >= 0, jnp.exp(lgf * jnp.maximum(diff, 0.0)), jnp.exp(lgb * jnp.maximum(-diff, 0.0)))
    qf_dec = jnp.exp(lgf * (ii + 1.0))
    qb_dec = jnp.exp(lgb * (c - ii))
    kf_dec = jnp.exp(lgf * (c - 1.0 - ii))
    kb_dec = jnp.exp(lgb * ii)
    gcf = jnp.exp(lgf * c)
    gcb = jnp.exp(lgb * c)

    def rows(n):
        return pl.ds(pl.multiple_of(n * c, c), c)

    def roped(ref, n):
        x = ref[rows(n), :].astype(F32)
        return _rotate(x, cos_ref[rows(n), :], sin_ref[rows(n), :], HEAD_DIM // 2)

    def increments(n, carry):
        k = roped(k_ref, n)
        kb_ref[rows(n), :] = k.astype(BF16)
        kd = jnp.concatenate([(k * kf_dec).astype(BF16), (k * kb_dec).astype(BF16)], axis=1)
        kv_ref[n] = lax.dot_general(kd, v_ref[rows(n), :], TN_DIMS, preferred_element_type=F32)
        return carry

    def recurrence(n0, cnt, s_f, s_b):
        def body(t, carry):
            s_f, s_b = carry
            nf = n0 + t
            nb = n0 + cnt - 1 - t
            st_ref[nf, 0:c, :] = s_f.astype(BF16)
            st_ref[nb, c:2 * c, :] = s_b.astype(BF16)
            return gcf * s_f + kv_ref[nf, 0:c, :], gcb * s_b + kv_ref[nb, c:2 * c, :]
        return lax.fori_loop(0, cnt, body, (s_f, s_b))

    def outputs(n, carry):
        q = roped(q_ref, n) * sc
        v = v_ref[rows(n), :]
        scores = lax.dot_general(q.astype(BF16), kb_ref[rows(n), :], NT_DIMS, preferred_element_type=F32) * dmat
        qd = jnp.concatenate([(q * qf_dec).astype(BF16), (q * qb_dec).astype(BF16)], axis=1)
        o = (jnp.dot(scores.astype(BF16), v, preferred_element_type=F32)
             + jnp.dot(qd, st_ref[n], preferred_element_type=F32))
        mu = jnp.mean(o, axis=-1, keepdims=True)
        oc = o - mu
        var = jnp.mean(oc * oc, axis=-1, keepdims=True)
        on = oc * lax.rsqrt(var + EPS) * gn_ref[...]
        g = g_ref[rows(n), :].astype(F32)
        o_ref[rows(n), :] = (jax.nn.silu(g) * on).astype(BF16)
        return carry

    lax.fori_loop(0, n_chunks, increments, 0, unroll=RET_UNROLL)
    zero = jnp.zeros((c, c), F32)
    s_f, s_b = recurrence(0, ctx_chunks, zero, zero)
    recurrence(ctx_chunks, lat_chunks, s_f, s_b)
    lax.fori_loop(0, n_chunks, outputs, 0, unroll=RET_UNROLL)


def _retention(proj, log2_decay, gn_w, cos, sin):
    h = RET_HEADS
    ld = jnp.broadcast_to(log2_decay.T[:, :, None], (h, 2, LANES))
    seq_spec = lambda cb0: pl.BlockSpec((TB, HEAD_DIM), functools.partial(lambda b, hh, cb0: (b, cb0 + hh), cb0=cb0))
    tab_spec = pl.BlockSpec((TB, LANES), lambda b, hh: (0, 0))
    return pl.pallas_call(
        _retention_kernel,
        grid=(BATCH, h),
        in_specs=[seq_spec(0), seq_spec(h), seq_spec(2 * h), seq_spec(3 * h), tab_spec, tab_spec,
                  pl.BlockSpec((None, 2, LANES), lambda b, hh: (hh, 0, 0)),
                  pl.BlockSpec((1, HEAD_DIM), lambda b, hh: (0, hh))],
        out_specs=pl.BlockSpec((TB, HEAD_DIM), lambda b, hh: (b, hh)),
        out_shape=jax.ShapeDtypeStruct((M_ROWS, h * HEAD_DIM), BF16),
        scratch_shapes=[pltpu.VMEM((TB, HEAD_DIM), BF16),
                        pltpu.VMEM((TB // RET_CHUNK, 2 * RET_CHUNK, HEAD_DIM), F32),
                        pltpu.VMEM((TB // RET_CHUNK, 2 * RET_CHUNK, HEAD_DIM), BF16)],
        compiler_params=_params(("arbitrary", "arbitrary")),
        name="retention",
    )(proj, proj, proj, proj, cos, sin, ld, gn_w.reshape(1, h * HEAD_DIM))


def _window_kernel(sink_ref, q_ref, k_ref, v_ref, cos_ref, sin_ref, o_ref, kr_ref, vx_ref):
    blk = WIN_BLOCK
    ctx_chunks = CTX_LEN // blk
    n_chunks = TB // blk
    kvh = pl.program_id(1)
    sc = HEAD_DIM ** -0.5
    half = HEAD_DIM // 4
    rows_q = WIN_GROUP * blk

    kr_ref[...] = _rotate(k_ref[...].astype(F32), cos_ref[...], sin_ref[...], half).astype(BF16)
    vx_ref[:, 0:HEAD_DIM] = v_ref[...]
    vx_ref[:, HEAD_DIM:2 * HEAD_DIM] = jnp.ones((TB, HEAD_DIM), BF16)

    row_iota = lax.broadcasted_iota(jnp.int32, (rows_q, 1), 0)
    head = row_iota // blk
    sink = jnp.zeros((rows_q, 1), F32)
    for g in range(WIN_GROUP):
        sink = jnp.where(head == g, sink_ref[kvh * WIN_GROUP + g], sink)

    def chunk(n, carry):
        rq = pl.ds(pl.multiple_of(n * blk, blk), blk)
        cos = cos_ref[rq, :]
        sin = sin_ref[rq, :]
        q = jnp.concatenate(
            [(_rotate(q_ref[rq, g * HEAD_DIM:(g + 1) * HEAD_DIM].astype(F32), cos, sin, half) * sc).astype(BF16)
             for g in range(WIN_GROUP)], axis=0)

        cstart = jnp.clip(n - 1, ctx_chunks, n_chunks - 3)
        rw = pl.ds(pl.multiple_of(cstart * blk, blk), 3 * blk)
        s_c = lax.dot_general(q, kr_ref[0:CTX_LEN, :], NT_DIMS, preferred_element_type=F32)
        s_w = lax.dot_general(q, kr_ref[rw, :], NT_DIMS, preferred_element_type=F32)
        qpos = (n - ctx_chunks) * blk + lax.broadcasted_iota(jnp.int32, (blk, 1), 0)
        kpos = (cstart - ctx_chunks) * blk + lax.broadcasted_iota(jnp.int32, (1, 3 * blk), 1)
        valid = (jnp.abs(qpos - kpos) <= WIN_RADIUS) & (n >= ctx_chunks)
        s_w = jnp.concatenate([jnp.where(valid, s_w[g * blk:(g + 1) * blk, :], NEG_INF)
                               for g in range(WIN_GROUP)], axis=0)

        m = jnp.maximum(jnp.maximum(jnp.max(s_c, axis=-1, keepdims=True), jnp.max(s_w, axis=-1, keepdims=True)),
                        sink)
        p_c = jnp.exp(s_c - m).astype(BF16)
        p_w = jnp.exp(s_w - m).astype(BF16)
        ox = (jnp.dot(p_c, vx_ref[0:CTX_LEN, :], preferred_element_type=F32)
              + jnp.dot(p_w, vx_ref[rw, :], preferred_element_type=F32))
        o = ox[:, 0:HEAD_DIM] / (ox[:, HEAD_DIM:2 * HEAD_DIM] + jnp.exp(sink - m))
        for g in range(WIN_GROUP):
            o_ref[rq, g * HEAD_DIM:(g + 1) * HEAD_DIM] = o[g * blk:(g + 1) * blk, :].astype(BF16)
        return carry

    lax.fori_loop(0, n_chunks, chunk, 0, unroll=2)


def _window_attention(proj, sink, cos, sin):
    gw = WIN_GROUP * HEAD_DIM
    q_cb0 = 4 * RET_HEADS * HEAD_DIM // gw
    k_cb0 = (4 * RET_HEADS + WIN_HEADS) * HEAD_DIM // HEAD_DIM
    v_cb0 = k_cb0 + WIN_KV_HEADS
    tab_spec = pl.BlockSpec((TB, LANES), lambda b, kh: (0, 0))
    return pl.pallas_call(
        _window_kernel,
        grid=(BATCH, WIN_KV_HEADS),
        in_specs=[
            pl.BlockSpec(memory_space=pltpu.SMEM),
            pl.BlockSpec((TB, gw), lambda b, kh: (b, q_cb0 + kh)),
            pl.BlockSpec((TB, HEAD_DIM), lambda b, kh: (b, k_cb0 + kh)),
            pl.BlockSpec((TB, HEAD_DIM), lambda b, kh: (b, v_cb0 + kh)),
            tab_spec, tab_spec,
        ],
        out_specs=pl.BlockSpec((TB, gw), lambda b, kh: (b, kh)),
        out_shape=jax.ShapeDtypeStruct((M_ROWS, WIN_HEADS * HEAD_DIM), BF16),
        scratch_shapes=[pltpu.VMEM((TB, HEAD_DIM), BF16), pltpu.VMEM((TB, 2 * HEAD_DIM), BF16)],
        compiler_params=_params(("arbitrary", "arbitrary")),
        name="window_attention",
    )(sink, proj, proj, proj, cos, sin)


def _conv_kernel(b_ref, c_ref, x_ref, w_ref, o_ref):
    u = c_ref[...].astype(F32) * x_ref[...].astype(F32)
    row = lax.broadcasted_iota(jnp.int32, (TB, 1), 0)
    prev = jnp.where((row == 0) | (row == CTX_LEN), 0.0, pltpu.roll(u, 1, 0))
    nxt = jnp.where((row == CTX_LEN - 1) | (row == TB - 1), 0.0, pltpu.roll(u, TB - 1, 0))
    z = prev * w_ref[0:1, :] + u * w_ref[1:2, :] + nxt * w_ref[2:3, :]
    o_ref[...] = (b_ref[...].astype(F32) * z).astype(BF16)


def _short_conv(proj, conv_w):
    tc = 256
    nblk = CONV_WIDTH // tc
    spec = lambda part: pl.BlockSpec((TB, tc), functools.partial(lambda b, j, part: (b, part * nblk + j), part=part))
    return pl.pallas_call(
        _conv_kernel,
        grid=(BATCH, nblk),
        in_specs=[spec(0), spec(1), spec(2), pl.BlockSpec((3, tc), lambda b, j: (0, j))],
        out_specs=pl.BlockSpec((TB, tc), lambda b, j: (b, j)),
        out_shape=jax.ShapeDtypeStruct((M_ROWS, CONV_WIDTH), BF16),
        compiler_params=_params(("arbitrary", "arbitrary")),
        name="short_conv",
    )(proj, proj, proj, conv_w)


MLA_HEADS_PER_STEP = 2
MLA_Q_TILE = 512
MLA_KEY_CHUNK = 1024


def _mla_kernel(q_ref, kv_ref, kr_ref, cos_ref, sin_ref, o_ref, kc_ref, vx_ref):
    sc = (MLA_NOPE + MLA_ROPE) ** -0.5
    half = MLA_ROPE // 4
    kv_w = MLA_NOPE + MLA_V

    kr = _rotate(kr_ref[...].astype(F32), cos_ref[...], sin_ref[...], half).astype(BF16)
    for hh in range(MLA_HEADS_PER_STEP):
        kc_ref[hh, :, 0:MLA_NOPE] = kv_ref[:, hh * kv_w:hh * kv_w + MLA_NOPE]
        kc_ref[hh, :, MLA_NOPE:MLA_QK_PAD] = kr
        vx_ref[hh, :, 0:MLA_V] = kv_ref[:, hh * kv_w + MLA_NOPE:(hh + 1) * kv_w]
        vx_ref[hh, :, MLA_V:2 * MLA_V] = jnp.ones((TB, MLA_V), BF16)

    def attend(row0, n_rows, key_chunks):
        rq = pl.ds(row0, n_rows)
        cos = cos_ref[rq, :]
        sin = sin_ref[rq, :]
        for hh in range(MLA_HEADS_PER_STEP):
            c0 = hh * MLA_QK_PAD
            qn = (q_ref[rq, c0:c0 + MLA_NOPE].astype(F32) * sc).astype(BF16)
            qr = (_rotate(q_ref[rq, c0 + MLA_NOPE:c0 + MLA_QK_PAD].astype(F32), cos, sin, half) * sc).astype(BF16)
            q = jnp.concatenate([qn, qr], axis=1)
            m = None
            acc = None
            for k0, kn in key_chunks:
                s = lax.dot_general(q, kc_ref[hh, k0:k0 + kn, :], NT_DIMS, preferred_element_type=F32)
                m_chunk = jnp.max(s, axis=-1, keepdims=True)
                m_new = m_chunk if m is None else jnp.maximum(m, m_chunk)
                p = jnp.exp(s - m_new).astype(BF16)
                pv = jnp.dot(p, vx_ref[hh, k0:k0 + kn, :], preferred_element_type=F32)
                acc = pv if acc is None else acc * jnp.exp(m - m_new) + pv
                m = m_new
            o_ref[rq, hh * MLA_V:(hh + 1) * MLA_V] = (acc[:, 0:MLA_V] / acc[:, MLA_V:2 * MLA_V]).astype(BF16)

    ctx_keys = [(0, CTX_LEN)]
    all_keys = ctx_keys + [(CTX_LEN + k * MLA_KEY_CHUNK, MLA_KEY_CHUNK) for k in range(SEQ // MLA_KEY_CHUNK)]
    attend(0, CTX_LEN, ctx_keys)

    def latent_tile(t, carry):
        attend(pl.multiple_of(CTX_LEN + t * MLA_Q_TILE, ROW_TILE), MLA_Q_TILE, all_keys)
        return carry

    lax.fori_loop(0, SEQ // MLA_Q_TILE, latent_tile, 0)


def _mla_attention(qcat, kv, proj, cos, sin):
    hps = MLA_HEADS_PER_STEP
    kr_cb = (3 * CONV_WIDTH + MLA_Q_RANK + MLA_KV_RANK) // LANES
    tab_spec = pl.BlockSpec((TB, LANES), lambda b, h: (0, 0))
    return pl.pallas_call(
        _mla_kernel,
        grid=(BATCH, MLA_HEADS // hps),
        in_specs=[
            pl.BlockSpec((TB, hps * MLA_QK_PAD), lambda b, h: (b, h)),
            pl.BlockSpec((TB, hps * (MLA_NOPE + MLA_V)), lambda b, h: (b, h)),
            pl.BlockSpec((TB, LANES), lambda b, h: (b, kr_cb)),
            tab_spec, tab_spec,
        ],
        out_specs=pl.BlockSpec((TB, hps * MLA_V), lambda b, h: (b, h)),
        out_shape=jax.ShapeDtypeStruct((M_ROWS, MLA_HEADS * MLA_V), BF16),
        scratch_shapes=[pltpu.VMEM((hps, TB, MLA_QK_PAD), BF16), pltpu.VMEM((hps, TB, 2 * MLA_V), BF16)],
        compiler_params=_params(("arbitrary", "arbitrary")),
        name="mla_attention",
    )(qcat, kv, proj, cos, sin)


def kernel(x, c, ctx, c_ctx, w_mod, b_mod, norm1_w, norm2_w, mlp_w1, mlp_w2, ev_w_in, ev_ret_log2_decay,
           ev_ret_gn_w, ev_sink, ev_w_out, od_w_in, od_conv_w, od_q_norm_w, od_kv_norm_w, od_w_uq, od_w_ukv,
           od_w_out, norm_f):
    assert x.shape == (BATCH, SEQ, D_MODEL) and ctx.shape == (BATCH, CTX_LEN, D_MODEL)
    rows = SEQ // GRID_W
    row = np.repeat(np.arange(rows), GRID_W)
    col = np.tile(np.arange(GRID_W), rows)
    pos = np.arange(SEQ)
    ret_cos, ret_sin = _rope_tables([pos], HEAD_DIM)
    win_cos, win_sin = _rope_tables([row, col], HEAD_DIM)
    mla_cos, mla_sin = _rope_tables([row, col], MLA_ROPE)

    cvec = jnp.concatenate([c, c_ctx[None], jnp.zeros((8 - BATCH - 1, D_MODEL), F32)], axis=0)
    mod_all = _modulation(cvec, w_mod, b_mod)

    n_odd = od_w_uq.shape[0]
    w_uq = jnp.pad(od_w_uq.reshape(n_odd, MLA_Q_RANK, MLA_HEADS, MLA_NOPE + MLA_ROPE),
                   ((0, 0), (0, 0), (0, 0), (0, MLA_QK_PAD - MLA_NOPE - MLA_ROPE))
                   ).reshape(n_odd, MLA_Q_RANK, MLA_HEADS * MLA_QK_PAD)

    for layer in range(DEPTH):
        j = layer // 2
        mod2 = mod_all[layer]
        if layer == 0:
            h, n1 = _embed_norm(x, ctx, norm1_w[layer], mod2.reshape(8, 6, D_MODEL))
        else:
            n1 = _norm_mod(h, norm1_w[layer], mod2, 0)
        if layer % 2 == 0:
            proj = _matmul_ws([n1], ev_w_in, j, tn=EVEN_IN // 4, tm=MM_TM_SMALL, name="even_in_proj")
            mix_a = _retention(proj, ev_ret_log2_decay[j], ev_ret_gn_w[j], ret_cos, ret_sin)
            mix_b = _window_attention(proj, ev_sink[j], win_cos, win_sin)
            w_out = ev_w_out
        else:
            proj = _matmul_ws([n1], od_w_in, j, tn=ODD_IN_PAD // 4, tm=MM_TM_BIG, n_out=ODD_IN_PAD,
                              name="odd_in_proj")
            mix_a = _short_conv(proj, od_conv_w[j])
            qcat = _matmul_ws([proj], w_uq, j, tn=MLA_HEADS * MLA_QK_PAD, tm=MM_TM_BIG, k_sizes=[MLA_Q_RANK],
                              a_col_blocks=[3 * CONV_WIDTH // MLA_Q_RANK], norm_w=od_q_norm_w[j], name="mla_q_up")
            kv = _matmul_ws([proj], od_w_ukv, j, tn=MLA_HEADS * (MLA_NOPE + MLA_V), tm=MM_TM_BIG,
                            k_sizes=[MLA_KV_RANK], a_col_blocks=[(3 * CONV_WIDTH + MLA_Q_RANK) // MLA_KV_RANK],
                            norm_w=od_kv_norm_w[j], name="mla_kv_up")
            mix_b = _mla_attention(qcat, kv, proj, mla_cos, mla_sin)
            w_out = od_w_out
        h, n2 = _matmul_ws([mix_a, mix_b], w_out, j, tn=D_MODEL, tm=MM_TM_SMALL, epilogue="gated_residual_norm",
                           res=h, mod=mod2, gate_idx=2, next_norm_w=norm2_w[layer], n_sub=2, name="mixer_out_proj")
        ff, w2_bf16 = _matmul_ws([n2], mlp_w1, layer, tn=1024, tm=MM_TM_BIG, epilogue="relu2",
                                 side_cast=(mlp_w2, layer), name="mlp_up")
        h = _matmul_ws([ff], w2_bf16[None], 0, tn=512, tm=MM_TM_DOWN, epilogue="gated_residual",
                       res=h, mod=mod2, gate_idx=5, name="mlp_down")
    return _final_norm(h, norm_f)
```

```python
import functools

import numpy as np
import jax
import jax.numpy as jnp
from jax import lax
from jax.experimental import pallas as pl
from jax.experimental.pallas import tpu as pltpu

F32 = jnp.float32
BF16 = jnp.bfloat16

D_MODEL = 2048
BATCH = 2
SEQ = 4096
DEPTH = 4
GRID_W = 64
CTX_LEN = 256
HEAD_DIM = 128
ROPE_BASE = 10000.0
EPS = 1e-6
NEG_INF = -1e30

RET_HEADS = 8
RET_CHUNK = 128
RET_UNROLL = 8
WIN_HEADS = 8
WIN_KV_HEADS = 2
WIN_GROUP = WIN_HEADS // WIN_KV_HEADS
WIN_RADIUS = 128
WIN_BLOCK = 128
CONV_WIDTH = 1024
MLA_HEADS = 8
MLA_Q_RANK = 512
MLA_KV_RANK = 256
MLA_NOPE = 128
MLA_ROPE = 64
MLA_V = 128
D_FF = 4 * D_MODEL

EVEN_IN = 4 * RET_HEADS * HEAD_DIM + (WIN_HEADS + 2 * WIN_KV_HEADS) * HEAD_DIM
ODD_IN = 3 * CONV_WIDTH + MLA_Q_RANK + MLA_KV_RANK + MLA_ROPE
ODD_IN_PAD = 3968
MLA_QK_PAD = 256

TB = CTX_LEN + SEQ
M_ROWS = BATCH * TB
LANES = 128
ROW_TILE = 256
MM_TM_BIG = M_ROWS // 8
MM_TM_SMALL = M_ROWS // 16
MOD_COLS = 256
MM_TM_DOWN = 512
VMEM_LIMIT = 60 * 1024 * 1024

NT_DIMS = (((1,), (1,)), ((), ()))
TN_DIMS = (((0,), (0,)), ((), ()))


def _params(sem, vmem=VMEM_LIMIT):
    return pltpu.CompilerParams(dimension_semantics=sem, vmem_limit_bytes=vmem)


def _rope_tables(pos_list, d_rot):
    per = d_rot // len(pos_list)
    inv = ROPE_BASE ** (-np.arange(0, per, 2, dtype=np.float64) / per)
    cos_parts, sin_parts = [], []
    for pos in pos_list:
        ang = pos.astype(np.float64)[:, None] * inv[None, :]
        c, s = np.cos(ang), np.sin(ang)
        cos_parts += [c, c]
        sin_parts += [-s, s]
    pad = ((0, 0), (0, LANES - d_rot))
    cos = np.pad(np.concatenate(cos_parts, axis=-1), pad)
    sin = np.pad(np.concatenate(sin_parts, axis=-1), pad)
    ctx_cos = np.pad(np.ones((CTX_LEN, d_rot)), pad)
    ctx_sin = np.zeros((CTX_LEN, LANES))
    return (jnp.asarray(np.concatenate([ctx_cos, cos], axis=0), F32),
            jnp.asarray(np.concatenate([ctx_sin, sin], axis=0), F32))


def _rotate(x, cos, sin, half):
    if 2 * half == LANES:
        partner = pltpu.roll(x, half, 1)
    else:
        lane = lax.broadcasted_iota(jnp.int32, (1, LANES), 1)
        first = (lane % (2 * half)) < half
        partner = jnp.where(first, pltpu.roll(x, LANES - half, 1), pltpu.roll(x, half, 1))
    return x * cos + partner * sin


def _modulation_kernel(c_ref, w_ref, b_ref, o_ref):
    s = jax.nn.silu(c_ref[...]).astype(BF16)
    o_ref[...] = jnp.dot(s, w_ref[...].astype(BF16), preferred_element_type=F32) + b_ref[...]


def _modulation(cvec, w_mod, b_mod, n_layers):
    tn = 1536
    n = 6 * D_MODEL
    return pl.pallas_call(
        _modulation_kernel,
        grid=(n_layers, n // tn),
        in_specs=[
            pl.BlockSpec((8, D_MODEL), lambda l, j: (0, 0)),
            pl.BlockSpec((None, D_MODEL, tn), lambda l, j: (l, 0, j)),
            pl.BlockSpec((None, 1, tn), lambda l, j: (l, 0, j)),
        ],
        out_specs=pl.BlockSpec((None, 8, tn), lambda l, j: (l, 0, j)),
        out_shape=jax.ShapeDtypeStruct((n_layers, 8, n), F32),
        compiler_params=_params(("arbitrary", "arbitrary")),
        name="modulation",
    )(cvec, w_mod, b_mod.reshape(DEPTH, 1, n))


def _segment_of_tile(i):
    tiles = TB // ROW_TILE
    return jnp.where(i % tiles == 0, BATCH, i // tiles)


def _embed_norm_kernel(x_ref, ctx_ref, w_ref, mod_ref, h_ref, n_ref):
    is_ctx = pl.program_id(0) % (TB // ROW_TILE) == 0
    x = jnp.where(is_ctx, ctx_ref[...], x_ref[...])
    h_ref[...] = x
    y = x * lax.rsqrt(jnp.mean(x * x, axis=-1, keepdims=True) + EPS) * w_ref[...]
    n_ref[...] = (y * (1.0 + mod_ref[1:2, :]) + mod_ref[0:1, :]).astype(BF16)


def _embed_norm(x, ctx, norm_w, mod3):
    assert CTX_LEN == ROW_TILE
    tiles = TB // ROW_TILE
    row_spec = pl.BlockSpec((ROW_TILE, D_MODEL), lambda i: (i, 0))
    return pl.pallas_call(
        _embed_norm_kernel,
        grid=(M_ROWS // ROW_TILE,),
        in_specs=[
            pl.BlockSpec((None, ROW_TILE, D_MODEL), lambda i: (i // tiles, jnp.maximum(i % tiles - 1, 0), 0)),
            pl.BlockSpec((None, ROW_TILE, D_MODEL), lambda i: (i // tiles, 0, 0)),
            pl.BlockSpec((1, D_MODEL), lambda i: (0, 0)),
            pl.BlockSpec((None, 6, D_MODEL), lambda i: (_segment_of_tile(i), 0, 0)),
        ],
        out_specs=[row_spec, row_spec],
        out_shape=[jax.ShapeDtypeStruct((M_ROWS, D_MODEL), F32), jax.ShapeDtypeStruct((M_ROWS, D_MODEL), BF16)],
        compiler_params=_params(("arbitrary",)),
        name="embed_norm",
    )(x, ctx, norm_w.reshape(1, D_MODEL), mod3)


def _norm_mod_kernel(x_ref, w_ref, shift_ref, scale_ref, o_ref, *, tm):
    row0 = pl.program_id(0) * tm
    second = row0 >= TB
    within = row0 - jnp.where(second, TB, 0)
    one_segment = (within >= CTX_LEN) & (within + tm <= TB)
    x = x_ref[...]
    y = x * lax.rsqrt(jnp.mean(x * x, axis=-1, keepdims=True) + EPS)

    def finish(pick):
        o_ref[...] = (y * (w_ref[...] * (1.0 + pick(scale_ref))) + pick(shift_ref)).astype(BF16)

    @pl.when(one_segment)
    def _():
        seg = second.astype(jnp.int32)
        finish(lambda tab_ref: tab_ref[pl.ds(seg, 1), :])

    @pl.when(jnp.logical_not(one_segment))
    def _():
        finish(lambda tab_ref: _row_select(tab_ref, row0, tm))


def _norm_mod(h, norm_w, mod, shift_idx):
    tm = MM_TM_SMALL
    return pl.pallas_call(
        functools.partial(_norm_mod_kernel, tm=tm),
        grid=(M_ROWS // tm,),
        in_specs=[
            pl.BlockSpec((tm, D_MODEL), lambda i: (i, 0)),
            pl.BlockSpec((1, D_MODEL), lambda i: (0, 0)),
            pl.BlockSpec((8, D_MODEL), lambda i: (0, shift_idx)),
            pl.BlockSpec((8, D_MODEL), lambda i: (0, shift_idx + 1)),
        ],
        out_specs=pl.BlockSpec((tm, D_MODEL), lambda i: (i, 0)),
        out_shape=jax.ShapeDtypeStruct((M_ROWS, D_MODEL), BF16),
        compiler_params=_params(("arbitrary",)),
        name="norm_mod",
    )(h, norm_w.reshape(1, D_MODEL), mod, mod)


def _final_norm_kernel(x_ref, w_ref, o_ref):
    x = x_ref[...]
    o_ref[...] = x * lax.rsqrt(jnp.mean(x * x, axis=-1, keepdims=True) + EPS) * w_ref[...]


def _final_norm(h, norm_w):
    tiles = TB // ROW_TILE
    ctx_tiles = CTX_LEN // ROW_TILE
    return pl.pallas_call(
        _final_norm_kernel,
        grid=(BATCH, SEQ // ROW_TILE),
        in_specs=[
            pl.BlockSpec((ROW_TILE, D_MODEL), lambda b, t: (b * tiles + ctx_tiles + t, 0)),
            pl.BlockSpec((1, D_MODEL), lambda b, t: (0, 0)),
        ],
        out_specs=pl.BlockSpec((None, ROW_TILE, D_MODEL), lambda b, t: (b, t, 0)),
        out_shape=jax.ShapeDtypeStruct((BATCH, SEQ, D_MODEL), F32),
        compiler_params=_params(("arbitrary", "arbitrary")),
        name="final_norm",
    )(h, norm_w.reshape(1, D_MODEL))


def _row_select(tab_ref, row0, tm):
    row = row0 + lax.broadcasted_iota(jnp.int32, (tm, 1), 0)
    second = row >= TB
    is_ctx = (row - jnp.where(second, TB, 0)) < CTX_LEN
    return jnp.where(is_ctx, tab_ref[2:3, :], jnp.where(second, tab_ref[1:2, :], tab_ref[0:1, :]))


def _mm_ws_kernel(*refs, k_sizes, epilogue, norm, tm, valid_cols, cast, n_sub, side_cast, side_mod_blocks):
    n_a = len(k_sizes)
    a_refs = refs[:n_a]
    w_ref = refs[n_a]
    pos = n_a + 1
    if norm:
        nw_ref = refs[pos]
        pos += 1
    if epilogue in ("gated_residual", "gated_residual_norm"):
        res_ref, gate_ref = refs[pos], refs[pos + 1]
        pos += 2
    if epilogue == "gated_residual_norm":
        n2w_ref, shift_ref, scale_ref = refs[pos:pos + 3]
        pos += 3
    if side_cast:
        side_in_ref = refs[pos]
        pos += 1
    if side_mod_blocks:
        ct_ref, wmod_ref, bmod_ref = refs[pos:pos + 3]
        pos += 3
    o_ref = refs[pos]
    pos += 1
    if epilogue == "gated_residual_norm":
        n_ref = refs[pos]
        pos += 1
    i = pl.program_id(1)
    if side_cast:
        side_out_ref = refs[pos]
        pos += 1
    if side_mod_blocks:
        mod_ref = refs[pos]
        pos += 1

    if cast:
        wb_ref = refs[pos]

        @pl.when(i == 0)
        def _():
            w = w_ref[...]
            if valid_cols is not None:
                tn = w.shape[1]
                col = pl.program_id(0) * tn + lax.broadcasted_iota(jnp.int32, (1, tn), 1)
                w = jnp.where(col < valid_cols, w, 0.0)
            wb_ref[...] = w.astype(BF16)
    else:
        wb_ref = w_ref

    ts = tm // n_sub
    for sub in range(n_sub):
        rs = slice(sub * ts, (sub + 1) * ts)
        row0 = i * tm + sub * ts
        acc = None
        off = 0
        for a_ref, ks in zip(a_refs, k_sizes):
            a = a_ref[rs, :]
            if norm:
                af = a.astype(F32)
                af = af * lax.rsqrt(jnp.mean(af * af, axis=-1, keepdims=True) + EPS) * nw_ref[...]
                a = af.astype(BF16)
            part = jnp.dot(a, wb_ref[off:off + ks, :], preferred_element_type=F32)
            acc = part if acc is None else acc + part
            off += ks

        if epilogue == "bf16":
            o_ref[rs, :] = acc.astype(BF16)
        elif epilogue == "relu2":
            r = jnp.maximum(acc, 0.0)
            o_ref[rs, :] = (r * r).astype(BF16)
        else:
            second = row0 >= TB
            within = row0 - jnp.where(second, TB, 0)
            one_segment = (within >= CTX_LEN) & (within + ts <= TB)

            def finish(pick, acc=acc, rs=rs):
                h = res_ref[rs, :] + pick(gate_ref) * acc
                o_ref[rs, :] = h
                if epilogue == "gated_residual_norm":
                    y = h * lax.rsqrt(jnp.mean(h * h, axis=-1, keepdims=True) + EPS)
                    n_ref[rs, :] = (y * (n2w_ref[...] * (1.0 + pick(scale_ref))) + pick(shift_ref)).astype(BF16)

            @pl.when(one_segment)
            def _(second=second, finish=finish):
                seg = second.astype(jnp.int32)
                finish(lambda tab_ref: tab_ref[pl.ds(seg, 1), :])

            @pl.when(jnp.logical_not(one_segment))
            def _(row0=row0, finish=finish):
                finish(lambda tab_ref: _row_select(tab_ref, row0, ts))

    if side_cast:
        side_out_ref[...] = side_in_ref[...].astype(BF16)
    if side_mod_blocks:
        s = jax.nn.silu(ct_ref[...])
        w = wmod_ref[...]
        row_id = lax.broadcasted_iota(jnp.int32, (8, w.shape[1]), 0)
        out = jnp.broadcast_to(bmod_ref[...], (8, w.shape[1]))
        for r in range(BATCH + 1):
            out = out + jnp.where(row_id == r, jnp.sum(w * s[:, r:r + 1], axis=0, keepdims=True), 0.0)
        mod_ref[...] = out


def _matmul_ws(a_list, w, layer, *, tn, tm, epilogue="bf16", a_col_blocks=None, k_sizes=None, norm_w=None,
               res=None, mod=None, gate_idx=0, next_norm_w=None, n_out=None, n_sub=1, side_cast=None, side_mod=None, single_buffer_w=False, name="matmul"):
    assert BATCH == 2
    _, kdim, n_w = w.shape
    n = n_w if n_out is None else n_out
    if k_sizes is None:
        k_sizes = [a.shape[1] for a in a_list]
    if a_col_blocks is None:
        a_col_blocks = [0] * len(a_list)
    assert sum(k_sizes) == kdim and n % tn == 0 and M_ROWS % tm == 0
    gated = epilogue in ("gated_residual", "gated_residual_norm")
    in_specs = [pl.BlockSpec((tm, ks), functools.partial(lambda j, i, cb: (i, cb), cb=cb))
                for ks, cb in zip(k_sizes, a_col_blocks)]
    w_mode = dict(pipeline_mode=pl.Buffered(1)) if (n == tn or single_buffer_w) else {}
    in_specs.append(pl.BlockSpec((None, kdim, tn), lambda j, i: (layer, 0, j), **w_mode))
    operands = list(a_list) + [w]
    if norm_w is not None:
        assert len(a_list) == 1
        in_specs.append(pl.BlockSpec((1, kdim), lambda j, i: (0, 0)))
        operands.append(norm_w.reshape(1, kdim))
    out_specs = pl.BlockSpec((tm, tn), lambda j, i: (i, j))
    out_shape = jax.ShapeDtypeStruct((M_ROWS, n), F32 if gated else BF16)
    if gated:
        blocks_per_vec = D_MODEL // tn
        in_specs.append(pl.BlockSpec((tm, tn), lambda j, i: (i, j)))
        in_specs.append(pl.BlockSpec((8, tn), lambda j, i: (0, gate_idx * blocks_per_vec + j)))
        operands += [res, mod]
    if epilogue == "gated_residual_norm":
        assert tn == n == D_MODEL
        in_specs.append(pl.BlockSpec((1, D_MODEL), lambda j, i: (0, 0)))
        in_specs.append(pl.BlockSpec((8, D_MODEL), lambda j, i: (0, gate_idx + 1)))
        in_specs.append(pl.BlockSpec((8, D_MODEL), lambda j, i: (0, gate_idx + 2)))
        operands += [next_norm_w.reshape(1, D_MODEL), mod, mod]
        out_specs = [out_specs, pl.BlockSpec((tm, tn), lambda j, i: (i, j))]
        out_shape = [out_shape, jax.ShapeDtypeStruct((M_ROWS, n), BF16)]
    if side_cast is not None:
        side_w, side_layer = side_cast
        _, side_k, side_n = side_w.shape
        row_tiles = M_ROWS // tm
        slab = side_k // ((n // tn) * row_tiles)
        assert slab * (n // tn) * row_tiles == side_k and not isinstance(out_specs, list)
        in_specs.append(pl.BlockSpec((None, slab, side_n), lambda j, i: (side_layer, j * row_tiles + i, 0)))
        operands.append(side_w)
        out_specs = [out_specs, pl.BlockSpec((slab, side_n), lambda j, i: (j * row_tiles + i, 0))]
        out_shape = [out_shape, jax.ShapeDtypeStruct((side_k, side_n), BF16)]
    side_mod_blocks = 0
    if side_mod is not None:
        ct, w_mod, b_mod, mod_layer = side_mod
        row_tiles = M_ROWS // tm
        side_mod_blocks = 6 * D_MODEL // MOD_COLS
        assert side_mod_blocks <= (n // tn) * row_tiles and isinstance(out_specs, list)
        blk = lambda j, i: jnp.minimum(j * row_tiles + i, side_mod_blocks - 1)
        in_specs.append(pl.BlockSpec((D_MODEL, 8), lambda j, i: (0, 0)))
        in_specs.append(pl.BlockSpec((None, D_MODEL, MOD_COLS), lambda j, i: (mod_layer, 0, blk(j, i))))
        in_specs.append(pl.BlockSpec((None, 1, MOD_COLS), lambda j, i: (mod_layer, 0, blk(j, i))))
        operands += [ct, w_mod, b_mod.reshape(DEPTH, 1, 6 * D_MODEL)]
        out_specs = out_specs + [pl.BlockSpec((8, MOD_COLS), lambda j, i: (0, blk(j, i)))]
        out_shape = out_shape + [jax.ShapeDtypeStruct((8, 6 * D_MODEL), F32)]
    cast = w.dtype != BF16
    return pl.pallas_call(
        functools.partial(_mm_ws_kernel, k_sizes=tuple(k_sizes), epilogue=epilogue, norm=norm_w is not None,
                          tm=tm, valid_cols=None if n == n_w else n_w, cast=cast, n_sub=n_sub,
                          side_cast=side_cast is not None, side_mod_blocks=side_mod_blocks),
        grid=(n // tn, M_ROWS // tm),
        in_specs=in_specs,
        out_specs=out_specs,
        out_shape=out_shape,
        scratch_shapes=[pltpu.VMEM((kdim, tn), BF16)] if cast else [],
        compiler_params=_params(("arbitrary", "arbitrary")),
        name=name,
    )(*operands)


def _retention_kernel(q_ref, k_ref, v_ref, g_ref, cos_ref, sin_ref, ld_ref, gn_ref, o_ref, kb_ref, kv_ref, st_ref):
    c = RET_CHUNK
    ctx_chunks = CTX_LEN // c
    lat_chunks = SEQ // c
    n_chunks = ctx_chunks + lat_chunks
    sc = HEAD_DIM ** -0.5
    lg = jnp.log1p(-jnp.exp2(ld_ref[...]))
    lgf = lg[0:1, 0:1]
    lgb = lg[1:2, 0:1]
    ii = lax.broadcasted_iota(jnp.int32, (c, 1), 0).astype(F32)
    jj = lax.broadcasted_iota(jnp.int32, (1, c), 1).astype(F32)
    diff = ii - jj
    dmat = jnp.where(diff >= 0, jnp.exp(lgf * jnp.maximum(diff, 0.0)), jnp.exp(lgb * jnp.maximum(-diff, 0.0)))
    qf_dec = jnp.exp(lgf * (ii + 1.0))
    qb_dec = jnp.exp(lgb * (c - ii))
    kf_dec = jnp.exp(lgf * (c - 1.0 - ii))
    kb_dec = jnp.exp(lgb * ii)
    gcf = jnp.exp(lgf * c)
    gcb = jnp.exp(lgb * c)

    def rows(n):
        return pl.ds(pl.multiple_of(n * c, c), c)

    def roped(ref, n):
        x = ref[rows(n), :].astype(F32)
        return _rotate(x, cos_ref[rows(n), :], sin_ref[rows(n), :], HEAD_DIM // 2)

    def increments(n, carry):
        k = roped(k_ref, n)
        kb_ref[rows(n), :] = k.astype(BF16)
        kd = jnp.concatenate([(k * kf_dec).astype(BF16), (k * kb_dec).astype(BF16)], axis=1)
        kv_ref[n] = lax.dot_general(kd, v_ref[rows(n), :], TN_DIMS, preferred_element_type=F32)
        return carry

    def recurrence(n0, cnt, s_f, s_b):
        def body(t, carry):
            s_f, s_b = carry
            nf = n0 + t
            nb = n0 + cnt - 1 - t
            st_ref[nf, 0:c, :] = s_f.astype(BF16)
            st_ref[nb, c:2 * c, :] = s_b.astype(BF16)
            return gcf * s_f + kv_ref[nf, 0:c, :], gcb * s_b + kv_ref[nb, c:2 * c, :]
        return lax.fori_loop(0, cnt, body, (s_f, s_b))

    def outputs(n, carry):
        q = roped(q_ref, n) * sc
        v = v_ref[rows(n), :]
        scores = lax.dot_general(q.astype(BF16), kb_ref[rows(n), :], NT_DIMS, preferred_element_type=F32) * dmat
        qd = jnp.concatenate([(q * qf_dec).astype(BF16), (q * qb_dec).astype(BF16)], axis=1)
        o = (jnp.dot(scores.astype(BF16), v, preferred_element_type=F32)
             + jnp.dot(qd, st_ref[n], preferred_element_type=F32))
        mu = jnp.mean(o, axis=-1, keepdims=True)
        oc = o - mu
        var = jnp.mean(oc * oc, axis=-1, keepdims=True)
        on = oc * lax.rsqrt(var + EPS) * gn_ref[...]
        g = g_ref[rows(n), :].astype(F32)
        o_ref[rows(n), :] = (jax.nn.silu(g) * on).astype(BF16)
        return carry

    lax.fori_loop(0, n_chunks, increments, 0, unroll=RET_UNROLL)
    zero = jnp.zeros((c, c), F32)
    s_f, s_b = recurrence(0, ctx_chunks, zero, zero)
    recurrence(ctx_chunks, lat_chunks, s_f, s_b)
    lax.fori_loop(0, n_chunks, outputs, 0, unroll=RET_UNROLL)


def _retention(proj, log2_decay, gn_w, cos, sin):
    h = RET_HEADS
    ld = jnp.broadcast_to(log2_decay.T[:, :, None], (h, 2, LANES))
    seq_spec = lambda cb0: pl.BlockSpec((TB, HEAD_DIM), functools.partial(lambda b, hh, cb0: (b, cb0 + hh), cb0=cb0))
    tab_spec = pl.BlockSpec((TB, LANES), lambda b, hh: (0, 0))
    return pl.pallas_call(
        _retention_kernel,
        grid=(BATCH, h),
        in_specs=[seq_spec(0), seq_spec(h), seq_spec(2 * h), seq_spec(3 * h), tab_spec, tab_spec,
                  pl.BlockSpec((None, 2, LANES), lambda b, hh: (hh, 0, 0)),
                  pl.BlockSpec((1, HEAD_DIM), lambda b, hh: (0, hh))],
        out_specs=pl.BlockSpec((TB, HEAD_DIM), lambda b, hh: (b, hh)),
        out_shape=jax.ShapeDtypeStruct((M_ROWS, h * HEAD_DIM), BF16),
        scratch_shapes=[pltpu.VMEM((TB, HEAD_DIM), BF16),
                        pltpu.VMEM((TB // RET_CHUNK, 2 * RET_CHUNK, HEAD_DIM), F32),
                        pltpu.VMEM((TB // RET_CHUNK, 2 * RET_CHUNK, HEAD_DIM), BF16)],
        compiler_params=_params(("arbitrary", "arbitrary")),
        name="retention",
    )(proj, proj, proj, proj, cos, sin, ld, gn_w.reshape(1, h * HEAD_DIM))


def _window_kernel(sink_ref, q_ref, k_ref, v_ref, cos_ref, sin_ref, o_ref, kr_ref, vx_ref):
    blk = WIN_BLOCK
    ctx_chunks = CTX_LEN // blk
    n_chunks = TB // blk
    kvh = pl.program_id(1)
    sc = HEAD_DIM ** -0.5
    half = HEAD_DIM // 4
    rows_q = WIN_GROUP * blk

    kr_ref[...] = _rotate(k_ref[...].astype(F32), cos_ref[...], sin_ref[...], half).astype(BF16)
    vx_ref[:, 0:HEAD_DIM] = v_ref[...]
    vx_ref[:, HEAD_DIM:2 * HEAD_DIM] = jnp.ones((TB, HEAD_DIM), BF16)

    row_iota = lax.broadcasted_iota(jnp.int32, (rows_q, 1), 0)
    head = row_iota // blk
    sink = jnp.zeros((rows_q, 1), F32)
    for g in range(WIN_GROUP):
        sink = jnp.where(head == g, sink_ref[kvh * WIN_GROUP + g], sink)

    def chunk(n, carry):
        rq = pl.ds(pl.multiple_of(n * blk, blk), blk)
        cos = cos_ref[rq, :]
        sin = sin_ref[rq, :]
        q = jnp.concatenate(
            [(_rotate(q_ref[rq, g * HEAD_DIM:(g + 1) * HEAD_DIM].astype(F32), cos, sin, half) * sc).astype(BF16)
             for g in range(WIN_GROUP)], axis=0)

        cstart = jnp.clip(n - 1, ctx_chunks, n_chunks - 3)
        rw = pl.ds(pl.multiple_of(cstart * blk, blk), 3 * blk)
        s_c = lax.dot_general(q, kr_ref[0:CTX_LEN, :], NT_DIMS, preferred_element_type=F32)
        s_w = lax.dot_general(q, kr_ref[rw, :], NT_DIMS, preferred_element_type=F32)
        qpos = (n - ctx_chunks) * blk + lax.broadcasted_iota(jnp.int32, (blk, 1), 0)
        kpos = (cstart - ctx_chunks) * blk + lax.broadcasted_iota(jnp.int32, (1, 3 * blk), 1)
        valid = (jnp.abs(qpos - kpos) <= WIN_RADIUS) & (n >= ctx_chunks)
        s_w = jnp.concatenate([jnp.where(valid, s_w[g * blk:(g + 1) * blk, :], NEG_INF)
                               for g in range(WIN_GROUP)], axis=0)

        m = jnp.maximum(jnp.maximum(jnp.max(s_c, axis=-1, keepdims=True), jnp.max(s_w, axis=-1, keepdims=True)),
                        sink)
        p_c = jnp.exp(s_c - m).astype(BF16)
        p_w = jnp.exp(s_w - m).astype(BF16)
        ox = (jnp.dot(p_c, vx_ref[0:CTX_LEN, :], preferred_element_type=F32)
              + jnp.dot(p_w, vx_ref[rw, :], preferred_element_type=F32))
        o = ox[:, 0:HEAD_DIM] / (ox[:, HEAD_DIM:2 * HEAD_DIM] + jnp.exp(sink - m))
        for g in range(WIN_GROUP):
            o_ref[rq, g * HEAD_DIM:(g + 1) * HEAD_DIM] = o[g * blk:(g + 1) * blk, :].astype(BF16)
        return carry

    lax.fori_loop(0, n_chunks, chunk, 0, unroll=2)


def _window_attention(proj, sink, cos, sin):
    gw = WIN_GROUP * HEAD_DIM
    q_cb0 = 4 * RET_HEADS * HEAD_DIM // gw
    k_cb0 = (4 * RET_HEADS + WIN_HEADS) * HEAD_DIM // HEAD_DIM
    v_cb0 = k_cb0 + WIN_KV_HEADS
    tab_spec = pl.BlockSpec((TB, LANES), lambda b, kh: (0, 0))
    return pl.pallas_call(
        _window_kernel,
        grid=(BATCH, WIN_KV_HEADS),
        in_specs=[
            pl.BlockSpec(memory_space=pltpu.SMEM),
            pl.BlockSpec((TB, gw), lambda b, kh: (b, q_cb0 + kh)),
            pl.BlockSpec((TB, HEAD_DIM), lambda b, kh: (b, k_cb0 + kh)),
            pl.BlockSpec((TB, HEAD_DIM), lambda b, kh: (b, v_cb0 + kh)),
            tab_spec, tab_spec,
        ],
        out_specs=pl.BlockSpec((TB, gw), lambda b, kh: (b, kh)),
        out_shape=jax.ShapeDtypeStruct((M_ROWS, WIN_HEADS * HEAD_DIM), BF16),
        scratch_shapes=[pltpu.VMEM((TB, HEAD_DIM), BF16), pltpu.VMEM((TB, 2 * HEAD_DIM), BF16)],
        compiler_params=_params(("arbitrary", "arbitrary")),
        name="window_attention",
    )(sink, proj, proj, proj, cos, sin)


def _conv_kernel(b_ref, c_ref, x_ref, w_ref, o_ref):
    u = c_ref[...].astype(F32) * x_ref[...].astype(F32)
    row = lax.broadcasted_iota(jnp.int32, (TB, 1), 0)
    prev = jnp.where((row == 0) | (row == CTX_LEN), 0.0, pltpu.roll(u, 1, 0))
    nxt = jnp.where((row == CTX_LEN - 1) | (row == TB - 1), 0.0, pltpu.roll(u, TB - 1, 0))
    z = prev * w_ref[0:1, :] + u * w_ref[1:2, :] + nxt * w_ref[2:3, :]
    o_ref[...] = (b_ref[...].astype(F32) * z).astype(BF16)


def _short_conv(proj, conv_w):
    tc = 256
    nblk = CONV_WIDTH // tc
    spec = lambda part: pl.BlockSpec((TB, tc), functools.partial(lambda b, j, part: (b, part * nblk + j), part=part))
    return pl.pallas_call(
        _conv_kernel,
        grid=(BATCH, nblk),
        in_specs=[spec(0), spec(1), spec(2), pl.BlockSpec((3, tc), lambda b, j: (0, j))],
        out_specs=pl.BlockSpec((TB, tc), lambda b, j: (b, j)),
        out_shape=jax.ShapeDtypeStruct((M_ROWS, CONV_WIDTH), BF16),
        compiler_params=_params(("arbitrary", "arbitrary")),
        name="short_conv",
    )(proj, proj, proj, conv_w)


MLA_HEADS_PER_STEP = 2
MLA_Q_TILE = 512
MLA_KEY_CHUNK = 1024


def _mla_kernel(q_ref, kv_ref, kr_ref, cos_ref, sin_ref, o_ref, kc_ref, vx_ref):
    sc = (MLA_NOPE + MLA_ROPE) ** -0.5
    half = MLA_ROPE // 4
    kv_w = MLA_NOPE + MLA_V

    kr = _rotate(kr_ref[...].astype(F32), cos_ref[...], sin_ref[...], half).astype(BF16)
    for hh in range(MLA_HEADS_PER_STEP):
        kc_ref[hh, :, 0:MLA_NOPE] = kv_ref[:, hh * kv_w:hh * kv_w + MLA_NOPE]
        kc_ref[hh, :, MLA_NOPE:MLA_QK_PAD] = kr
        vx_ref[hh, :, 0:MLA_V] = kv_ref[:, hh * kv_w + MLA_NOPE:(hh + 1) * kv_w]
        vx_ref[hh, :, MLA_V:2 * MLA_V] = jnp.ones((TB, MLA_V), BF16)

    def attend(row0, n_rows, key_chunks):
        rq = pl.ds(row0, n_rows)
        cos = cos_ref[rq, :]
        sin = sin_ref[rq, :]
        for hh in range(MLA_HEADS_PER_STEP):
            c0 = hh * MLA_QK_PAD
            qn = (q_ref[rq, c0:c0 + MLA_NOPE].astype(F32) * sc).astype(BF16)
            qr = (_rotate(q_ref[rq, c0 + MLA_NOPE:c0 + MLA_QK_PAD].astype(F32), cos, sin, half) * sc).astype(BF16)
            q = jnp.concatenate([qn, qr], axis=1)
            m = None
            acc = None
            for k0, kn in key_chunks:
                s = lax.dot_general(q, kc_ref[hh, k0:k0 + kn, :], NT_DIMS, preferred_element_type=F32)
                m_chunk = jnp.max(s, axis=-1, keepdims=True)
                m_new = m_chunk if m is None else jnp.maximum(m, m_chunk)
                p = jnp.exp(s - m_new).astype(BF16)
                pv = jnp.dot(p, vx_ref[hh, k0:k0 + kn, :], preferred_element_type=F32)
                acc = pv if acc is None else acc * jnp.exp(m - m_new) + pv
                m = m_new
            o_ref[rq, hh * MLA_V:(hh + 1) * MLA_V] = (acc[:, 0:MLA_V] / acc[:, MLA_V:2 * MLA_V]).astype(BF16)

    ctx_keys = [(0, CTX_LEN)]
    all_keys = ctx_keys + [(CTX_LEN + k * MLA_KEY_CHUNK, MLA_KEY_CHUNK) for k in range(SEQ // MLA_KEY_CHUNK)]
    attend(0, CTX_LEN, ctx_keys)

    def latent_tile(t, carry):
        attend(pl.multiple_of(CTX_LEN + t * MLA_Q_TILE, ROW_TILE), MLA_Q_TILE, all_keys)
        return carry

    lax.fori_loop(0, SEQ // MLA_Q_TILE, latent_tile, 0)


def _mla_attention(qcat, kv, proj, cos, sin):
    hps = MLA_HEADS_PER_STEP
    kr_cb = (3 * CONV_WIDTH + MLA_Q_RANK + MLA_KV_RANK) // LANES
    tab_spec = pl.BlockSpec((TB, LANES), lambda b, h: (0, 0))
    return pl.pallas_call(
        _mla_kernel,
        grid=(BATCH, MLA_HEADS // hps),
        in_specs=[
            pl.BlockSpec((TB, hps * MLA_QK_PAD), lambda b, h: (b, h)),
            pl.BlockSpec((TB, hps * (MLA_NOPE + MLA_V)), lambda b, h: (b, h)),
            pl.BlockSpec((TB, LANES), lambda b, h: (b, kr_cb)),
            tab_spec, tab_spec,
        ],
        out_specs=pl.BlockSpec((TB, hps * MLA_V), lambda b, h: (b, h)),
        out_shape=jax.ShapeDtypeStruct((M_ROWS, MLA_HEADS * MLA_V), BF16),
        scratch_shapes=[pltpu.VMEM((hps, TB, MLA_QK_PAD), BF16), pltpu.VMEM((hps, TB, 2 * MLA_V), BF16)],
        compiler_params=_params(("arbitrary", "arbitrary")),
        name="mla_attention",
    )(qcat, kv, proj, cos, sin)


def kernel(x, c, ctx, c_ctx, w_mod, b_mod, norm1_w, norm2_w, mlp_w1, mlp_w2, ev_w_in, ev_ret_log2_decay,
           ev_ret_gn_w, ev_sink, ev_w_out, od_w_in, od_conv_w, od_q_norm_w, od_kv_norm_w, od_w_uq, od_w_ukv,
           od_w_out, norm_f):
    assert x.shape == (BATCH, SEQ, D_MODEL) and ctx.shape == (BATCH, CTX_LEN, D_MODEL)
    rows = SEQ // GRID_W
    row = np.repeat(np.arange(rows), GRID_W)
    col = np.tile(np.arange(GRID_W), rows)
    pos = np.arange(SEQ)
    ret_cos, ret_sin = _rope_tables([pos], HEAD_DIM)
    win_cos, win_sin = _rope_tables([row, col], HEAD_DIM)
    mla_cos, mla_sin = _rope_tables([row, col], MLA_ROPE)

    cvec = jnp.concatenate([c, c_ctx[None], jnp.zeros((8 - BATCH - 1, D_MODEL), F32)], axis=0)
    mod2 = _modulation(cvec, w_mod, b_mod, 1)[0]
    c_cols = cvec.T

    n_odd = od_w_uq.shape[0]
    w_uq = jnp.pad(od_w_uq.reshape(n_odd, MLA_Q_RANK, MLA_HEADS, MLA_NOPE + MLA_ROPE),
                   ((0, 0), (0, 0), (0, 0), (0, MLA_QK_PAD - MLA_NOPE - MLA_ROPE))
                   ).reshape(n_odd, MLA_Q_RANK, MLA_HEADS * MLA_QK_PAD)

    w_in_odd = jnp.pad(od_w_in.astype(BF16), ((0, 0), (0, 0), (0, ODD_IN_PAD - ODD_IN)))

    for layer in range(DEPTH):
        j = layer // 2
        if layer == 0:
            h, n1 = _embed_norm(x, ctx, norm1_w[layer], mod2.reshape(8, 6, D_MODEL))
        else:
            n1 = _norm_mod(h, norm1_w[layer], mod2, 0)
        if layer % 2 == 0:
            proj = _matmul_ws([n1], ev_w_in, j, tn=EVEN_IN // 4, tm=MM_TM_SMALL, name="even_in_proj")
            mix_a = _retention(proj, ev_ret_log2_decay[j], ev_ret_gn_w[j], ret_cos, ret_sin)
            mix_b = _window_attention(proj, ev_sink[j], win_cos, win_sin)
            w_out = ev_w_out
        else:
            proj = _matmul_ws([n1], w_in_odd, j, tn=ODD_IN_PAD, tm=MM_TM_SMALL, name="odd_in_proj")
            mix_a = _short_conv(proj, od_conv_w[j])
            qcat = _matmul_ws([proj], w_uq, j, tn=MLA_HEADS * MLA_QK_PAD, tm=MM_TM_BIG, k_sizes=[MLA_Q_RANK],
                              a_col_blocks=[3 * CONV_WIDTH // MLA_Q_RANK], norm_w=od_q_norm_w[j], name="mla_q_up")
            kv = _matmul_ws([proj], od_w_ukv, j, tn=MLA_HEADS * (MLA_NOPE + MLA_V), tm=MM_TM_BIG,
                            k_sizes=[MLA_KV_RANK], a_col_blocks=[(3 * CONV_WIDTH + MLA_Q_RANK) // MLA_KV_RANK],
                            norm_w=od_kv_norm_w[j], name="mla_kv_up")
            mix_b = _mla_attention(qcat, kv, proj, mla_cos, mla_sin)
            w_out = od_w_out
        h, n2 = _matmul_ws([mix_a, mix_b], w_out, j, tn=D_MODEL, tm=MM_TM_SMALL, epilogue="gated_residual_norm",
                           res=h, mod=mod2, gate_idx=2, next_norm_w=norm2_w[layer], n_sub=2, name="mixer_out_proj")
        side_mod = (c_cols, w_mod, b_mod, layer + 1) if layer + 1 < DEPTH else None
        ff, w2_bf16, *mod_next = _matmul_ws([n2], mlp_w1, layer, tn=1024, tm=MM_TM_BIG, epilogue="relu2",
                                            side_cast=(mlp_w2, layer), side_mod=side_mod, name="mlp_up")
        h = _matmul_ws([ff], w2_bf16[None], 0, tn=1024, tm=MM_TM_DOWN, epilogue="gated_residual",
                       res=h, mod=mod2, gate_idx=5, single_buffer_w=True, name="mlp_down")
        if mod_next:
            mod2 = mod_next[0]
    return _final_norm(h, norm_f)
```

```python
import functools

import numpy as np
import jax
import jax.numpy as jnp
from jax import lax
from jax.experimental import pallas as pl
from jax.experimental.pallas import tpu as pltpu

F32 = jnp.float32
BF16 = jnp.bfloat16

D_MODEL = 2048
BATCH = 2
SEQ = 4096
DEPTH = 4
GRID_W = 64
CTX_LEN = 256
HEAD_DIM = 128
ROPE_BASE = 10000.0
EPS = 1e-6
NEG_INF = -1e30

RET_HEADS = 8
RET_CHUNK = 128
RET_UNROLL = 8
WIN_HEADS = 8
WIN_KV_HEADS = 2
WIN_GROUP = WIN_HEADS // WIN_KV_HEADS
WIN_RADIUS = 128
WIN_BLOCK = 128
CONV_WIDTH = 1024
MLA_HEADS = 8
MLA_Q_RANK = 512
MLA_KV_RANK = 256
MLA_NOPE = 128
MLA_ROPE = 64
MLA_V = 128
D_FF = 4 * D_MODEL

EVEN_IN = 4 * RET_HEADS * HEAD_DIM + (WIN_HEADS + 2 * WIN_KV_HEADS) * HEAD_DIM
ODD_IN = 3 * CONV_WIDTH + MLA_Q_RANK + MLA_KV_RANK + MLA_ROPE
ODD_IN_PAD = 3968
MLA_QK_PAD = 256

TB = CTX_LEN + SEQ
M_ROWS = BATCH * TB
LANES = 128
ROW_TILE = 256
MM_TM_BIG = M_ROWS // 8
MM_TM_SMALL = M_ROWS // 16
MOD_COLS = 256
MM_TM_DOWN = 512
VMEM_LIMIT = 60 * 1024 * 1024

NT_DIMS = (((1,), (1,)), ((), ()))
TN_DIMS = (((0,), (0,)), ((), ()))


def _params(sem, vmem=VMEM_LIMIT):
    return pltpu.CompilerParams(dimension_semantics=sem, vmem_limit_bytes=vmem)


def _rope_tables(pos_list, d_rot):
    per = d_rot // len(pos_list)
    inv = ROPE_BASE ** (-np.arange(0, per, 2, dtype=np.float64) / per)
    cos_parts, sin_parts = [], []
    for pos in pos_list:
        ang = pos.astype(np.float64)[:, None] * inv[None, :]
        c, s = np.cos(ang), np.sin(ang)
        cos_parts += [c, c]
        sin_parts += [-s, s]
    pad = ((0, 0), (0, LANES - d_rot))
    cos = np.pad(np.concatenate(cos_parts, axis=-1), pad)
    sin = np.pad(np.concatenate(sin_parts, axis=-1), pad)
    ctx_cos = np.pad(np.ones((CTX_LEN, d_rot)), pad)
    ctx_sin = np.zeros((CTX_LEN, LANES))
    return (jnp.asarray(np.concatenate([ctx_cos, cos], axis=0), F32),
            jnp.asarray(np.concatenate([ctx_sin, sin], axis=0), F32))


def _rotate(x, cos, sin, half):
    if 2 * half == LANES:
        partner = pltpu.roll(x, half, 1)
    else:
        lane = lax.broadcasted_iota(jnp.int32, (1, LANES), 1)
        first = (lane % (2 * half)) < half
        partner = jnp.where(first, pltpu.roll(x, LANES - half, 1), pltpu.roll(x, half, 1))
    return x * cos + partner * sin


def _modulation_kernel(c_ref, w_ref, b_ref, o_ref):
    s = jax.nn.silu(c_ref[...]).astype(BF16)
    o_ref[...] = jnp.dot(s, w_ref[...].astype(BF16), preferred_element_type=F32) + b_ref[...]


def _modulation(cvec, w_mod, b_mod, n_cols):
    tn = 512
    return pl.pallas_call(
        _modulation_kernel,
        grid=(n_cols // tn,),
        in_specs=[
            pl.BlockSpec((8, D_MODEL), lambda j: (0, 0)),
            pl.BlockSpec((None, D_MODEL, tn), lambda j: (0, 0, j)),
            pl.BlockSpec((None, 1, tn), lambda j: (0, 0, j)),
        ],
        out_specs=pl.BlockSpec((8, tn), lambda j: (0, j)),
        out_shape=jax.ShapeDtypeStruct((8, n_cols), F32),
        compiler_params=_params(("arbitrary",)),
        name="modulation",
    )(cvec, w_mod, b_mod.reshape(DEPTH, 1, 6 * D_MODEL))


def _segment_of_tile(i):
    tiles = TB // ROW_TILE
    return jnp.where(i % tiles == 0, BATCH, i // tiles)


def _embed_norm_kernel(x_ref, ctx_ref, w_ref, mod_ref, h_ref, n_ref):
    is_ctx = pl.program_id(0) % (TB // ROW_TILE) == 0
    x = jnp.where(is_ctx, ctx_ref[...], x_ref[...])
    h_ref[...] = x
    y = x * lax.rsqrt(jnp.mean(x * x, axis=-1, keepdims=True) + EPS) * w_ref[...]
    n_ref[...] = (y * (1.0 + mod_ref[1:2, :]) + mod_ref[0:1, :]).astype(BF16)


def _embed_norm(x, ctx, norm_w, mod3):
    assert CTX_LEN == ROW_TILE
    tiles = TB // ROW_TILE
    row_spec = pl.BlockSpec((ROW_TILE, D_MODEL), lambda i: (i, 0))
    return pl.pallas_call(
        _embed_norm_kernel,
        grid=(M_ROWS // ROW_TILE,),
        in_specs=[
            pl.BlockSpec((None, ROW_TILE, D_MODEL), lambda i: (i // tiles, jnp.maximum(i % tiles - 1, 0), 0)),
            pl.BlockSpec((None, ROW_TILE, D_MODEL), lambda i: (i // tiles, 0, 0)),
            pl.BlockSpec((1, D_MODEL), lambda i: (0, 0)),
            pl.BlockSpec((None, mod3.shape[1], D_MODEL), lambda i: (_segment_of_tile(i), 0, 0)),
        ],
        out_specs=[row_spec, row_spec],
        out_shape=[jax.ShapeDtypeStruct((M_ROWS, D_MODEL), F32), jax.ShapeDtypeStruct((M_ROWS, D_MODEL), BF16)],
        compiler_params=_params(("arbitrary",)),
        name="embed_norm",
    )(x, ctx, norm_w.reshape(1, D_MODEL), mod3)


def _norm_mod_kernel(x_ref, w_ref, shift_ref, scale_ref, o_ref, *, tm):
    row0 = pl.program_id(0) * tm
    second = row0 >= TB
    within = row0 - jnp.where(second, TB, 0)
    one_segment = (within >= CTX_LEN) & (within + tm <= TB)
    x = x_ref[...]
    y = x * lax.rsqrt(jnp.mean(x * x, axis=-1, keepdims=True) + EPS)

    def finish(pick):
        o_ref[...] = (y * (w_ref[...] * (1.0 + pick(scale_ref))) + pick(shift_ref)).astype(BF16)

    @pl.when(one_segment)
    def _():
        seg = second.astype(jnp.int32)
        finish(lambda tab_ref: tab_ref[pl.ds(seg, 1), :])

    @pl.when(jnp.logical_not(one_segment))
    def _():
        finish(lambda tab_ref: _row_select(tab_ref, row0, tm))


def _norm_mod(h, norm_w, mod, shift_idx):
    tm = MM_TM_SMALL
    return pl.pallas_call(
        functools.partial(_norm_mod_kernel, tm=tm),
        grid=(M_ROWS // tm,),
        in_specs=[
            pl.BlockSpec((tm, D_MODEL), lambda i: (i, 0)),
            pl.BlockSpec((1, D_MODEL), lambda i: (0, 0)),
            pl.BlockSpec((8, D_MODEL), lambda i: (0, shift_idx)),
            pl.BlockSpec((8, D_MODEL), lambda i: (0, shift_idx + 1)),
        ],
        out_specs=pl.BlockSpec((tm, D_MODEL), lambda i: (i, 0)),
        out_shape=jax.ShapeDtypeStruct((M_ROWS, D_MODEL), BF16),
        compiler_params=_params(("arbitrary",)),
        name="norm_mod",
    )(h, norm_w.reshape(1, D_MODEL), mod, mod)


def _final_norm_kernel(x_ref, w_ref, o_ref):
    x = x_ref[...]
    o_ref[...] = x * lax.rsqrt(jnp.mean(x * x, axis=-1, keepdims=True) + EPS) * w_ref[...]


def _final_norm(h, norm_w):
    tiles = TB // ROW_TILE
    ctx_tiles = CTX_LEN // ROW_TILE
    return pl.pallas_call(
        _final_norm_kernel,
        grid=(BATCH, SEQ // ROW_TILE),
        in_specs=[
            pl.BlockSpec((ROW_TILE, D_MODEL), lambda b, t: (b * tiles + ctx_tiles + t, 0)),
            pl.BlockSpec((1, D_MODEL), lambda b, t: (0, 0)),
        ],
        out_specs=pl.BlockSpec((None, ROW_TILE, D_MODEL), lambda b, t: (b, t, 0)),
        out_shape=jax.ShapeDtypeStruct((BATCH, SEQ, D_MODEL), F32),
        compiler_params=_params(("arbitrary", "arbitrary")),
        name="final_norm",
    )(h, norm_w.reshape(1, D_MODEL))


def _row_select(tab_ref, row0, tm):
    row = row0 + lax.broadcasted_iota(jnp.int32, (tm, 1), 0)
    second = row >= TB
    is_ctx = (row - jnp.where(second, TB, 0)) < CTX_LEN
    return jnp.where(is_ctx, tab_ref[2:3, :], jnp.where(second, tab_ref[1:2, :], tab_ref[0:1, :]))


def _mm_ws_kernel(*refs, k_sizes, epilogue, norm, tm, cast, n_sub, side_cast, side_mod_blocks):
    n_a = len(k_sizes)
    a_refs = refs[:n_a]
    w_ref = refs[n_a]
    pos = n_a + 1
    if norm:
        nw_ref = refs[pos]
        pos += 1
    if epilogue in ("gated_residual", "gated_residual_norm"):
        res_ref, gate_ref = refs[pos], refs[pos + 1]
        pos += 2
    if epilogue == "gated_residual_norm":
        n2w_ref, shift_ref, scale_ref = refs[pos:pos + 3]
        pos += 3
    if side_cast:
        side_in_ref = refs[pos]
        pos += 1
    if side_mod_blocks:
        ct_ref, wmod_ref, bmod_ref = refs[pos:pos + 3]
        pos += 3
    o_ref = refs[pos]
    pos += 1
    if epilogue == "gated_residual_norm":
        n_ref = refs[pos]
        pos += 1
    i = pl.program_id(1)
    if side_cast:
        side_out_ref = refs[pos]
        pos += 1
    if side_mod_blocks:
        mod_ref = refs[pos]
        pos += 1

    if cast:
        wb_ref = refs[pos]

        @pl.when(i == 0)
        def _():
            wb_ref[...] = w_ref[...].astype(BF16)
    else:
        wb_ref = w_ref

    ts = tm // n_sub
    for sub in range(n_sub):
        rs = slice(sub * ts, (sub + 1) * ts)
        row0 = i * tm + sub * ts
        acc = None
        off = 0
        for a_ref, ks in zip(a_refs, k_sizes):
            a = a_ref[rs, :]
            if norm:
                af = a.astype(F32)
                af = af * lax.rsqrt(jnp.mean(af * af, axis=-1, keepdims=True) + EPS) * nw_ref[...]
                a = af.astype(BF16)
            part = jnp.dot(a, wb_ref[off:off + ks, :], preferred_element_type=F32)
            acc = part if acc is None else acc + part
            off += ks

        if epilogue == "bf16":
            o_ref[rs, :] = acc.astype(BF16)
        elif epilogue == "relu2":
            r = jnp.maximum(acc, 0.0)
            o_ref[rs, :] = (r * r).astype(BF16)
        else:
            second = row0 >= TB
            within = row0 - jnp.where(second, TB, 0)
            one_segment = (within >= CTX_LEN) & (within + ts <= TB)

            def finish(pick, acc=acc, rs=rs):
                h = res_ref[rs, :] + pick(gate_ref) * acc
                o_ref[rs, :] = h
                if epilogue == "gated_residual_norm":
                    y = h * lax.rsqrt(jnp.mean(h * h, axis=-1, keepdims=True) + EPS)
                    n_ref[rs, :] = (y * (n2w_ref[...] * (1.0 + pick(scale_ref))) + pick(shift_ref)).astype(BF16)

            @pl.when(one_segment)
            def _(second=second, finish=finish):
                seg = second.astype(jnp.int32)
                finish(lambda tab_ref: tab_ref[pl.ds(seg, 1), :])

            @pl.when(jnp.logical_not(one_segment))
            def _(row0=row0, finish=finish):
                finish(lambda tab_ref: _row_select(tab_ref, row0, ts))

    if side_cast:
        side_out_ref[...] = side_in_ref[...].astype(BF16)
    if side_mod_blocks:
        s = jax.nn.silu(ct_ref[...])
        w = wmod_ref[...]
        row_id = lax.broadcasted_iota(jnp.int32, (8, w.shape[1]), 0)
        out = jnp.broadcast_to(bmod_ref[...], (8, w.shape[1]))
        for r in range(BATCH + 1):
            out = out + jnp.where(row_id == r, jnp.sum(w * s[:, r:r + 1], axis=0, keepdims=True), 0.0)
        mod_ref[...] = out


def _matmul_ws(a_list, w, layer, *, tn, tm, epilogue="bf16", a_col_blocks=None, k_sizes=None, norm_w=None,
               res=None, mod=None, gate_idx=0, next_norm_w=None, n_sub=1, side_cast=None, side_mod=None,
               single_buffer_w=False, name="matmul"):
    assert BATCH == 2
    _, kdim, n = w.shape
    if k_sizes is None:
        k_sizes = [a.shape[1] for a in a_list]
    if a_col_blocks is None:
        a_col_blocks = [0] * len(a_list)
    assert sum(k_sizes) == kdim and n % tn == 0 and M_ROWS % tm == 0
    row_tiles = M_ROWS // tm
    n_steps = (n // tn) * row_tiles
    step = lambda j, i: j * row_tiles + i
    gated = epilogue in ("gated_residual", "gated_residual_norm")
    in_specs = [pl.BlockSpec((tm, ks), functools.partial(lambda j, i, cb: (i, cb), cb=cb))
                for ks, cb in zip(k_sizes, a_col_blocks)]
    w_mode = dict(pipeline_mode=pl.Buffered(1)) if (n == tn or single_buffer_w) else {}
    in_specs.append(pl.BlockSpec((None, kdim, tn), lambda j, i: (layer, 0, j), **w_mode))
    operands = list(a_list) + [w]
    if norm_w is not None:
        assert len(a_list) == 1
        in_specs.append(pl.BlockSpec((1, kdim), lambda j, i: (0, 0)))
        operands.append(norm_w.reshape(1, kdim))
    out_specs = [pl.BlockSpec((tm, tn), lambda j, i: (i, j))]
    out_shape = [jax.ShapeDtypeStruct((M_ROWS, n), F32 if gated else BF16)]
    if gated:
        blocks_per_vec = D_MODEL // tn
        in_specs.append(pl.BlockSpec((tm, tn), lambda j, i: (i, j)))
        in_specs.append(pl.BlockSpec((8, tn), lambda j, i: (0, gate_idx * blocks_per_vec + j)))
        operands += [res, mod]
    if epilogue == "gated_residual_norm":
        assert tn == n == D_MODEL
        in_specs.append(pl.BlockSpec((1, D_MODEL), lambda j, i: (0, 0)))
        in_specs.append(pl.BlockSpec((8, D_MODEL), lambda j, i: (0, gate_idx + 1)))
        in_specs.append(pl.BlockSpec((8, D_MODEL), lambda j, i: (0, gate_idx + 2)))
        operands += [next_norm_w.reshape(1, D_MODEL), mod, mod]
        out_specs.append(pl.BlockSpec((tm, tn), lambda j, i: (i, j)))
        out_shape.append(jax.ShapeDtypeStruct((M_ROWS, n), BF16))
    if side_cast is not None:
        side_w, side_layer = side_cast
        _, side_k, side_n = side_w.shape
        slab = side_k // n_steps
        assert slab * n_steps == side_k
        in_specs.append(pl.BlockSpec((None, slab, side_n), lambda j, i: (side_layer, step(j, i), 0)))
        operands.append(side_w)
        out_specs.append(pl.BlockSpec((slab, side_n), lambda j, i: (step(j, i), 0)))
        out_shape.append(jax.ShapeDtypeStruct((side_k, side_n), BF16))
    side_mod_blocks = 0
    if side_mod is not None:
        ct, w_mod, b_mod, mod_layer, first_blk, side_mod_blocks = side_mod
        assert side_mod_blocks <= n_steps
        rel = lambda j, i: jnp.minimum(step(j, i), side_mod_blocks - 1)
        in_specs.append(pl.BlockSpec((D_MODEL, 8), lambda j, i: (0, 0)))
        in_specs.append(pl.BlockSpec((None, D_MODEL, MOD_COLS), lambda j, i: (mod_layer, 0, first_blk + rel(j, i))))
        in_specs.append(pl.BlockSpec((None, 1, MOD_COLS), lambda j, i: (mod_layer, 0, first_blk + rel(j, i))))
        operands += [ct, w_mod, b_mod.reshape(DEPTH, 1, 6 * D_MODEL)]
        out_specs.append(pl.BlockSpec((8, MOD_COLS), lambda j, i: (0, rel(j, i))))
        out_shape.append(jax.ShapeDtypeStruct((8, side_mod_blocks * MOD_COLS), F32))
    cast = w.dtype != BF16
    outs = pl.pallas_call(
        functools.partial(_mm_ws_kernel, k_sizes=tuple(k_sizes), epilogue=epilogue, norm=norm_w is not None,
                          tm=tm, cast=cast, n_sub=n_sub, side_cast=side_cast is not None,
                          side_mod_blocks=side_mod_blocks),
        grid=(n // tn, row_tiles),
        in_specs=in_specs,
        out_specs=out_specs,
        out_shape=out_shape,
        scratch_shapes=[pltpu.VMEM((kdim, tn), BF16)] if cast else [],
        compiler_params=_params(("arbitrary", "arbitrary")),
        name=name,
    )(*operands)
    return outs[0] if len(outs) == 1 else outs


def _retention_kernel(q_ref, k_ref, v_ref, g_ref, cos_ref, sin_ref, ld_ref, gn_ref, o_ref, kb_ref, kv_ref, st_ref):
    c = RET_CHUNK
    ctx_chunks = CTX_LEN // c
    lat_chunks = SEQ // c
    n_chunks = ctx_chunks + lat_chunks
    sc = HEAD_DIM ** -0.5
    lg = jnp.log1p(-jnp.exp2(ld_ref[...]))
    lgf = lg[0:1, 0:1]
    lgb = lg[1:2, 0:1]
    ii = lax.broadcasted_iota(jnp.int32, (c, 1), 0).astype(F32)
    jj = lax.broadcasted_iota(jnp.int32, (1, c), 1).astype(F32)
    diff = ii - jj
    dmat = jnp.where(diff >= 0, jnp.exp(lgf * jnp.maximum(diff, 0.0)), jnp.exp(lgb * jnp.maximum(-diff, 0.0)))
    qf_dec = jnp.exp(lgf * (ii + 1.0))
    qb_dec = jnp.exp(lgb * (c - ii))
    kf_dec = jnp.exp(lgf * (c - 1.0 - ii))
    kb_dec = jnp.exp(lgb * ii)
    gcf = jnp.exp(lgf * c)
    gcb = jnp.exp(lgb * c)

    def rows(n):
        return pl.ds(pl.multiple_of(n * c, c), c)

    def roped(ref, n):
        x = ref[rows(n), :].astype(F32)
        return _rotate(x, cos_ref[rows(n), :], sin_ref[rows(n), :], HEAD_DIM // 2)

    def increments(n, carry):
        k = roped(k_ref, n)
        kb_ref[rows(n), :] = k.astype(BF16)
        kd = jnp.concatenate([(k * kf_dec).astype(BF16), (k * kb_dec).astype(BF16)], axis=1)
        kv_ref[n] = lax.dot_general(kd, v_ref[rows(n), :], TN_DIMS, preferred_element_type=F32)
        return carry

    def recurrence(n0, cnt, s_f, s_b):
        def body(t, carry):
            s_f, s_b = carry
            nf = n0 + t
            nb = n0 + cnt - 1 - t
            st_ref[nf, 0:c, :] = s_f.astype(BF16)
            st_ref[nb, c:2 * c, :] = s_b.astype(BF16)
            return gcf * s_f + kv_ref[nf, 0:c, :], gcb * s_b + kv_ref[nb, c:2 * c, :]
        return lax.fori_loop(0, cnt, body, (s_f, s_b))

    def outputs(n, carry):
        q = roped(q_ref, n) * sc
        v = v_ref[rows(n), :]
        scores = lax.dot_general(q.astype(BF16), kb_ref[rows(n), :], NT_DIMS, preferred_element_type=F32) * dmat
        qd = jnp.concatenate([(q * qf_dec).astype(BF16), (q * qb_dec).astype(BF16)], axis=1)
        o = (jnp.dot(scores.astype(BF16), v, preferred_element_type=F32)
             + jnp.dot(qd, st_ref[n], preferred_element_type=F32))
        mu = jnp.mean(o, axis=-1, keepdims=True)
        oc = o - mu
        var = jnp.mean(oc * oc, axis=-1, keepdims=True)
        on = oc * lax.rsqrt(var + EPS) * gn_ref[...]
        g = g_ref[rows(n), :].astype(F32)
        o_ref[rows(n), :] = (jax.nn.silu(g) * on).astype(BF16)
        return carry

    lax.fori_loop(0, n_chunks, increments, 0, unroll=RET_UNROLL)
    zero = jnp.zeros((c, c), F32)
    s_f, s_b = recurrence(0, ctx_chunks, zero, zero)
    recurrence(ctx_chunks, lat_chunks, s_f, s_b)
    lax.fori_loop(0, n_chunks, outputs, 0, unroll=RET_UNROLL)


def _retention(proj, log2_decay, gn_w, cos, sin):
    h = RET_HEADS
    ld = jnp.broadcast_to(log2_decay.T[:, :, None], (h, 2, LANES))
    seq_spec = lambda cb0: pl.BlockSpec((TB, HEAD_DIM), functools.partial(lambda b, hh, cb0: (b, cb0 + hh), cb0=cb0))
    tab_spec = pl.BlockSpec((TB, LANES), lambda b, hh: (0, 0))
    return pl.pallas_call(
        _retention_kernel,
        grid=(BATCH, h),
        in_specs=[seq_spec(0), seq_spec(h), seq_spec(2 * h), seq_spec(3 * h), tab_spec, tab_spec,
                  pl.BlockSpec((None, 2, LANES), lambda b, hh: (hh, 0, 0)),
                  pl.BlockSpec((1, HEAD_DIM), lambda b, hh: (0, hh))],
        out_specs=pl.BlockSpec((TB, HEAD_DIM), lambda b, hh: (b, hh)),
        out_shape=jax.ShapeDtypeStruct((M_ROWS, h * HEAD_DIM), BF16),
        scratch_shapes=[pltpu.VMEM((TB, HEAD_DIM), BF16),
                        pltpu.VMEM((TB // RET_CHUNK, 2 * RET_CHUNK, HEAD_DIM), F32),
                        pltpu.VMEM((TB // RET_CHUNK, 2 * RET_CHUNK, HEAD_DIM), BF16)],
        compiler_params=_params(("arbitrary", "arbitrary")),
        name="retention",
    )(proj, proj, proj, proj, cos, sin, ld, gn_w.reshape(1, h * HEAD_DIM))


def _window_kernel(sink_ref, q_ref, k_ref, v_ref, cos_ref, sin_ref, o_ref, kr_ref, vx_ref):
    blk = WIN_BLOCK
    ctx_chunks = CTX_LEN // blk
    n_chunks = TB // blk
    kvh = pl.program_id(1)
    sc = HEAD_DIM ** -0.5
    half = HEAD_DIM // 4
    rows_q = WIN_GROUP * blk

    kr_ref[...] = _rotate(k_ref[...].astype(F32), cos_ref[...], sin_ref[...], half).astype(BF16)
    vx_ref[:, 0:HEAD_DIM] = v_ref[...]
    vx_ref[:, HEAD_DIM:2 * HEAD_DIM] = jnp.ones((TB, HEAD_DIM), BF16)

    row_iota = lax.broadcasted_iota(jnp.int32, (rows_q, 1), 0)
    head = row_iota // blk
    sink = jnp.zeros((rows_q, 1), F32)
    for g in range(WIN_GROUP):
        sink = jnp.where(head == g, sink_ref[kvh * WIN_GROUP + g], sink)

    def chunk(n, carry):
        rq = pl.ds(pl.multiple_of(n * blk, blk), blk)
        cos = cos_ref[rq, :]
        sin = sin_ref[rq, :]
        q = jnp.concatenate(
            [(_rotate(q_ref[rq, g * HEAD_DIM:(g + 1) * HEAD_DIM].astype(F32), cos, sin, half) * sc).astype(BF16)
             for g in range(WIN_GROUP)], axis=0)

        cstart = jnp.clip(n - 1, ctx_chunks, n_chunks - 3)
        rw = pl.ds(pl.multiple_of(cstart * blk, blk), 3 * blk)
        s_c = lax.dot_general(q, kr_ref[0:CTX_LEN, :], NT_DIMS, preferred_element_type=F32)
        s_w = lax.dot_general(q, kr_ref[rw, :], NT_DIMS, preferred_element_type=F32)
        qpos = (n - ctx_chunks) * blk + lax.broadcasted_iota(jnp.int32, (blk, 1), 0)
        kpos = (cstart - ctx_chunks) * blk + lax.broadcasted_iota(jnp.int32, (1, 3 * blk), 1)
        valid = (jnp.abs(qpos - kpos) <= WIN_RADIUS) & (n >= ctx_chunks)
        s_w = jnp.concatenate([jnp.where(valid, s_w[g * blk:(g + 1) * blk, :], NEG_INF)
                               for g in range(WIN_GROUP)], axis=0)

        m = jnp.maximum(jnp.maximum(jnp.max(s_c, axis=-1, keepdims=True), jnp.max(s_w, axis=-1, keepdims=True)),
                        sink)
        p_c = jnp.exp(s_c - m).astype(BF16)
        p_w = jnp.exp(s_w - m).astype(BF16)
        ox = (jnp.dot(p_c, vx_ref[0:CTX_LEN, :], preferred_element_type=F32)
              + jnp.dot(p_w, vx_ref[rw, :], preferred_element_type=F32))
        o = ox[:, 0:HEAD_DIM] / (ox[:, HEAD_DIM:2 * HEAD_DIM] + jnp.exp(sink - m))
        for g in range(WIN_GROUP):
            o_ref[rq, g * HEAD_DIM:(g + 1) * HEAD_DIM] = o[g * blk:(g + 1) * blk, :].astype(BF16)
        return carry

    lax.fori_loop(0, n_chunks, chunk, 0, unroll=2)


def _window_attention(proj, sink, cos, sin):
    gw = WIN_GROUP * HEAD_DIM
    q_cb0 = 4 * RET_HEADS * HEAD_DIM // gw
    k_cb0 = (4 * RET_HEADS + WIN_HEADS) * HEAD_DIM // HEAD_DIM
    v_cb0 = k_cb0 + WIN_KV_HEADS
    tab_spec = pl.BlockSpec((TB, LANES), lambda b, kh: (0, 0))
    return pl.pallas_call(
        _window_kernel,
        grid=(BATCH, WIN_KV_HEADS),
        in_specs=[
            pl.BlockSpec(memory_space=pltpu.SMEM),
            pl.BlockSpec((TB, gw), lambda b, kh: (b, q_cb0 + kh)),
            pl.BlockSpec((TB, HEAD_DIM), lambda b, kh: (b, k_cb0 + kh)),
            pl.BlockSpec((TB, HEAD_DIM), lambda b, kh: (b, v_cb0 + kh)),
            tab_spec, tab_spec,
        ],
        out_specs=pl.BlockSpec((TB, gw), lambda b, kh: (b, kh)),
        out_shape=jax.ShapeDtypeStruct((M_ROWS, WIN_HEADS * HEAD_DIM), BF16),
        scratch_shapes=[pltpu.VMEM((TB, HEAD_DIM), BF16), pltpu.VMEM((TB, 2 * HEAD_DIM), BF16)],
        compiler_params=_params(("arbitrary", "arbitrary")),
        name="window_attention",
    )(sink, proj, proj, proj, cos, sin)


def _conv_kernel(b_ref, c_ref, x_ref, w_ref, o_ref):
    u = c_ref[...].astype(F32) * x_ref[...].astype(F32)
    row = lax.broadcasted_iota(jnp.int32, (TB, 1), 0)
    prev = jnp.where((row == 0) | (row == CTX_LEN), 0.0, pltpu.roll(u, 1, 0))
    nxt = jnp.where((row == CTX_LEN - 1) | (row == TB - 1), 0.0, pltpu.roll(u, TB - 1, 0))
    z = prev * w_ref[0:1, :] + u * w_ref[1:2, :] + nxt * w_ref[2:3, :]
    o_ref[...] = (b_ref[...].astype(F32) * z).astype(BF16)


def _short_conv(proj, conv_w):
    tc = 256
    nblk = CONV_WIDTH // tc
    spec = lambda part: pl.BlockSpec((TB, tc), functools.partial(lambda b, j, part: (b, part * nblk + j), part=part))
    return pl.pallas_call(
        _conv_kernel,
        grid=(BATCH, nblk),
        in_specs=[spec(0), spec(1), spec(2), pl.BlockSpec((3, tc), lambda b, j: (0, j))],
        out_specs=pl.BlockSpec((TB, tc), lambda b, j: (b, j)),
        out_shape=jax.ShapeDtypeStruct((M_ROWS, CONV_WIDTH), BF16),
        compiler_params=_params(("arbitrary", "arbitrary")),
        name="short_conv",
    )(proj, proj, proj, conv_w)


MLA_HEADS_PER_STEP = 2
MLA_Q_TILE = 512
MLA_KEY_CHUNK = 1024


def _mla_kernel(q_ref, kv_ref, kr_ref, cos_ref, sin_ref, o_ref, kc_ref, vx_ref):
    sc = (MLA_NOPE + MLA_ROPE) ** -0.5
    half = MLA_ROPE // 4
    kv_w = MLA_NOPE + MLA_V

    kr = _rotate(kr_ref[...].astype(F32), cos_ref[...], sin_ref[...], half).astype(BF16)
    for hh in range(MLA_HEADS_PER_STEP):
        kc_ref[hh, :, 0:MLA_NOPE] = kv_ref[:, hh * kv_w:hh * kv_w + MLA_NOPE]
        kc_ref[hh, :, MLA_NOPE:MLA_QK_PAD] = kr
        vx_ref[hh, :, 0:MLA_V] = kv_ref[:, hh * kv_w + MLA_NOPE:(hh + 1) * kv_w]
        vx_ref[hh, :, MLA_V:2 * MLA_V] = jnp.ones((TB, MLA_V), BF16)

    def attend(row0, n_rows, key_chunks):
        rq = pl.ds(row0, n_rows)
        cos = cos_ref[rq, :]
        sin = sin_ref[rq, :]
        for hh in range(MLA_HEADS_PER_STEP):
            c0 = hh * MLA_QK_PAD
            qn = (q_ref[rq, c0:c0 + MLA_NOPE].astype(F32) * sc).astype(BF16)
            qr = (_rotate(q_ref[rq, c0 + MLA_NOPE:c0 + MLA_QK_PAD].astype(F32), cos, sin, half) * sc).astype(BF16)
            q = jnp.concatenate([qn, qr], axis=1)
            m = None
            acc = None
            for k0, kn in key_chunks:
                s = lax.dot_general(q, kc_ref[hh, k0:k0 + kn, :], NT_DIMS, preferred_element_type=F32)
                m_chunk = jnp.max(s, axis=-1, keepdims=True)
                m_new = m_chunk if m is None else jnp.maximum(m, m_chunk)
                p = jnp.exp(s - m_new).astype(BF16)
                pv = jnp.dot(p, vx_ref[hh, k0:k0 + kn, :], preferred_element_type=F32)
                acc = pv if acc is None else acc * jnp.exp(m - m_new) + pv
                m = m_new
            o_ref[rq, hh * MLA_V:(hh + 1) * MLA_V] = (acc[:, 0:MLA_V] / acc[:, MLA_V:2 * MLA_V]).astype(BF16)

    ctx_keys = [(0, CTX_LEN)]
    all_keys = ctx_keys + [(CTX_LEN + k * MLA_KEY_CHUNK, MLA_KEY_CHUNK) for k in range(SEQ // MLA_KEY_CHUNK)]
    attend(0, CTX_LEN, ctx_keys)

    def latent_tile(t, carry):
        attend(pl.multiple_of(CTX_LEN + t * MLA_Q_TILE, ROW_TILE), MLA_Q_TILE, all_keys)
        return carry

    lax.fori_loop(0, SEQ // MLA_Q_TILE, latent_tile, 0)


def _mla_attention(qcat, kv, proj, cos, sin):
    hps = MLA_HEADS_PER_STEP
    kr_cb = (3 * CONV_WIDTH + MLA_Q_RANK + MLA_KV_RANK) // LANES
    tab_spec = pl.BlockSpec((TB, LANES), lambda b, h: (0, 0))
    return pl.pallas_call(
        _mla_kernel,
        grid=(BATCH, MLA_HEADS // hps),
        in_specs=[
            pl.BlockSpec((TB, hps * MLA_QK_PAD), lambda b, h: (b, h)),
            pl.BlockSpec((TB, hps * (MLA_NOPE + MLA_V)), lambda b, h: (b, h)),
            pl.BlockSpec((TB, LANES), lambda b, h: (b, kr_cb)),
            tab_spec, tab_spec,
        ],
        out_specs=pl.BlockSpec((TB, hps * MLA_V), lambda b, h: (b, h)),
        out_shape=jax.ShapeDtypeStruct((M_ROWS, MLA_HEADS * MLA_V), BF16),
        scratch_shapes=[pltpu.VMEM((hps, TB, MLA_QK_PAD), BF16), pltpu.VMEM((hps, TB, 2 * MLA_V), BF16)],
        compiler_params=_params(("arbitrary", "arbitrary")),
        name="mla_attention",
    )(qcat, kv, proj, cos, sin)


def kernel(x, c, ctx, c_ctx, w_mod, b_mod, norm1_w, norm2_w, mlp_w1, mlp_w2, ev_w_in, ev_ret_log2_decay,
           ev_ret_gn_w, ev_sink, ev_w_out, od_w_in, od_conv_w, od_q_norm_w, od_kv_norm_w, od_w_uq, od_w_ukv,
           od_w_out, norm_f):
    assert x.shape == (BATCH, SEQ, D_MODEL) and ctx.shape == (BATCH, CTX_LEN, D_MODEL)
    rows = SEQ // GRID_W
    row = np.repeat(np.arange(rows), GRID_W)
    col = np.tile(np.arange(GRID_W), rows)
    pos = np.arange(SEQ)
    ret_cos, ret_sin = _rope_tables([pos], HEAD_DIM)
    win_cos, win_sin = _rope_tables([row, col], HEAD_DIM)
    mla_cos, mla_sin = _rope_tables([row, col], MLA_ROPE)

    cvec = jnp.concatenate([c, c_ctx[None], jnp.zeros((8 - BATCH - 1, D_MODEL), F32)], axis=0)
    mod_head = _modulation(cvec, w_mod, b_mod, 2 * D_MODEL)
    c_cols = cvec.T
    head_blocks = 2 * D_MODEL // MOD_COLS
    all_blocks = 6 * D_MODEL // MOD_COLS

    n_odd = od_w_uq.shape[0]
    w_uq = jnp.pad(od_w_uq.astype(BF16).reshape(n_odd, MLA_Q_RANK, MLA_HEADS, MLA_NOPE + MLA_ROPE),
                   ((0, 0), (0, 0), (0, 0), (0, MLA_QK_PAD - MLA_NOPE - MLA_ROPE))
                   ).reshape(n_odd, MLA_Q_RANK, MLA_HEADS * MLA_QK_PAD)

    w_in_odd = jnp.pad(od_w_in.astype(BF16), ((0, 0), (0, 0), (0, ODD_IN_PAD - ODD_IN)))

    for layer in range(DEPTH):
        j = layer // 2
        if layer == 0:
            h, n1 = _embed_norm(x, ctx, norm1_w[layer], mod_head.reshape(8, 2, D_MODEL))
        else:
            n1 = _norm_mod(h, norm1_w[layer], mod2, 0)
        if layer % 2 == 0:
            side_mod = (c_cols, w_mod, b_mod, 0, head_blocks, all_blocks - head_blocks) if layer == 0 else None
            proj = _matmul_ws([n1], ev_w_in, j, tn=EVEN_IN // 4, tm=MM_TM_SMALL, side_mod=side_mod,
                              name="even_in_proj")
            if layer == 0:
                proj, mod_tail = proj
                mod2 = jnp.concatenate([mod_head, mod_tail], axis=1)
            mix_a = _retention(proj, ev_ret_log2_decay[j], ev_ret_gn_w[j], ret_cos, ret_sin)
            mix_b = _window_attention(proj, ev_sink[j], win_cos, win_sin)
            w_out = ev_w_out
        else:
            proj = _matmul_ws([n1], w_in_odd, j, tn=ODD_IN_PAD, tm=MM_TM_SMALL, name="odd_in_proj")
            mix_a = _short_conv(proj, od_conv_w[j])
            qcat = _matmul_ws([proj], w_uq, j, tn=MLA_HEADS * MLA_QK_PAD, tm=MM_TM_BIG, k_sizes=[MLA_Q_RANK],
                              a_col_blocks=[3 * CONV_WIDTH // MLA_Q_RANK], norm_w=od_q_norm_w[j], name="mla_q_up")
            kv = _matmul_ws([proj], od_w_ukv, j, tn=MLA_HEADS * (MLA_NOPE + MLA_V), tm=MM_TM_BIG,
                            k_sizes=[MLA_KV_RANK], a_col_blocks=[(3 * CONV_WIDTH + MLA_Q_RANK) // MLA_KV_RANK],
                            norm_w=od_kv_norm_w[j], name="mla_kv_up")
            mix_b = _mla_attention(qcat, kv, proj, mla_cos, mla_sin)
            w_out = od_w_out
        h, n2 = _matmul_ws([mix_a, mix_b], w_out, j, tn=D_MODEL, tm=MM_TM_SMALL, epilogue="gated_residual_norm",
                           res=h, mod=mod2, gate_idx=2, next_norm_w=norm2_w[layer], n_sub=2, name="mixer_out_proj")
        side_mod = (c_cols, w_mod, b_mod, layer + 1, 0, all_blocks) if layer + 1 < DEPTH else None
        ff, w2_bf16, *mod_next = _matmul_ws([n2], mlp_w1, layer, tn=1024, tm=MM_TM_BIG, epilogue="relu2",
                                            side_cast=(mlp_w2, layer), side_mod=side_mod, name="mlp_up")
        h = _matmul_ws([ff], w2_bf16[None], 0, tn=1024, tm=MM_TM_DOWN, epilogue="gated_residual",
                       res=h, mod=mod2, gate_idx=5, single_buffer_w=True, name="mlp_down")
        if mod_next:
            mod2 = mod_next[0]
    return _final_norm(h, norm_f)
```

```python
import functools

import numpy as np
import jax
import jax.numpy as jnp
from jax import lax
from jax.experimental import pallas as pl
from jax.experimental.pallas import tpu as pltpu

F32 = jnp.float32
BF16 = jnp.bfloat16

D_MODEL = 2048
BATCH = 2
SEQ = 4096
DEPTH = 4
GRID_W = 64
CTX_LEN = 256
HEAD_DIM = 128
ROPE_BASE = 10000.0
EPS = 1e-6
NEG_INF = -1e30

RET_HEADS = 8
RET_CHUNK = 128
RET_UNROLL_STATE = 8
RET_UNROLL_OUT = 17
WIN_HEADS = 8
WIN_KV_HEADS = 2
WIN_GROUP = WIN_HEADS // WIN_KV_HEADS
WIN_RADIUS = 128
WIN_BLOCK = 128
CONV_WIDTH = 1024
MLA_HEADS = 8
MLA_Q_RANK = 512
MLA_KV_RANK = 256
MLA_NOPE = 128
MLA_ROPE = 64
MLA_V = 128
D_FF = 4 * D_MODEL

EVEN_IN = 4 * RET_HEADS * HEAD_DIM + (WIN_HEADS + 2 * WIN_KV_HEADS) * HEAD_DIM
ODD_IN = 3 * CONV_WIDTH + MLA_Q_RANK + MLA_KV_RANK + MLA_ROPE
ODD_IN_PAD = 3968
MLA_QK_PAD = 256

TB = CTX_LEN + SEQ
M_ROWS = BATCH * TB
LANES = 128
ROW_TILE = 256
MM_TM_BIG = M_ROWS // 8
MM_TM_SMALL = M_ROWS // 16
MOD_COLS = 256
MM_TM_DOWN = 512
VMEM_LIMIT = 60 * 1024 * 1024

NT_DIMS = (((1,), (1,)), ((), ()))
TN_DIMS = (((0,), (0,)), ((), ()))


def _params(sem, vmem=VMEM_LIMIT):
    return pltpu.CompilerParams(dimension_semantics=sem, vmem_limit_bytes=vmem)


def _rope_tables(pos_list, d_rot):
    per = d_rot // len(pos_list)
    inv = ROPE_BASE ** (-np.arange(0, per, 2, dtype=np.float64) / per)
    cos_parts, sin_parts = [], []
    for pos in pos_list:
        ang = pos.astype(np.float64)[:, None] * inv[None, :]
        c, s = np.cos(ang), np.sin(ang)
        cos_parts += [c, c]
        sin_parts += [-s, s]
    pad = ((0, 0), (0, LANES - d_rot))
    cos = np.pad(np.concatenate(cos_parts, axis=-1), pad)
    sin = np.pad(np.concatenate(sin_parts, axis=-1), pad)
    ctx_cos = np.pad(np.ones((CTX_LEN, d_rot)), pad)
    ctx_sin = np.zeros((CTX_LEN, LANES))
    return (jnp.asarray(np.concatenate([ctx_cos, cos], axis=0), F32),
            jnp.asarray(np.concatenate([ctx_sin, sin], axis=0), F32))


def _rotate(x, cos, sin, half):
    if 2 * half == LANES:
        partner = pltpu.roll(x, half, 1)
    else:
        lane = lax.broadcasted_iota(jnp.int32, (1, LANES), 1)
        first = (lane % (2 * half)) < half
        partner = jnp.where(first, pltpu.roll(x, LANES - half, 1), pltpu.roll(x, half, 1))
    return x * cos + partner * sin


def _modulation_kernel(c_ref, w_ref, b_ref, o_ref):
    s = jax.nn.silu(c_ref[...]).astype(BF16)
    o_ref[...] = jnp.dot(s, w_ref[...].astype(BF16), preferred_element_type=F32) + b_ref[...]


def _modulation(cvec, w_mod, b_mod, n_cols):
    tn = 1024
    return pl.pallas_call(
        _modulation_kernel,
        grid=(n_cols // tn,),
        in_specs=[
            pl.BlockSpec((8, D_MODEL), lambda j: (0, 0)),
            pl.BlockSpec((None, D_MODEL, tn), lambda j: (0, 0, j)),
            pl.BlockSpec((None, 1, tn), lambda j: (0, 0, j)),
        ],
        out_specs=pl.BlockSpec((8, tn), lambda j: (0, j)),
        out_shape=jax.ShapeDtypeStruct((8, n_cols), F32),
        compiler_params=_params(("arbitrary",)),
        name="modulation",
    )(cvec, w_mod, b_mod.reshape(DEPTH, 1, 6 * D_MODEL))


def _segment_of_tile(i):
    tiles = TB // ROW_TILE
    return jnp.where(i % tiles == 0, BATCH, i // tiles)


def _embed_norm_kernel(x_ref, ctx_ref, w_ref, mod_ref, h_ref, n_ref):
    is_ctx = pl.program_id(0) % (TB // ROW_TILE) == 0
    x = jnp.where(is_ctx, ctx_ref[...], x_ref[...])
    h_ref[...] = x
    y = x * lax.rsqrt(jnp.mean(x * x, axis=-1, keepdims=True) + EPS) * w_ref[...]
    n_ref[...] = (y * (1.0 + mod_ref[1:2, :]) + mod_ref[0:1, :]).astype(BF16)


def _embed_norm(x, ctx, norm_w, mod3):
    assert CTX_LEN == ROW_TILE
    tiles = TB // ROW_TILE
    row_spec = pl.BlockSpec((ROW_TILE, D_MODEL), lambda i: (i, 0))
    return pl.pallas_call(
        _embed_norm_kernel,
        grid=(M_ROWS // ROW_TILE,),
        in_specs=[
            pl.BlockSpec((None, ROW_TILE, D_MODEL), lambda i: (i // tiles, jnp.maximum(i % tiles - 1, 0), 0)),
            pl.BlockSpec((None, ROW_TILE, D_MODEL), lambda i: (i // tiles, 0, 0)),
            pl.BlockSpec((1, D_MODEL), lambda i: (0, 0)),
            pl.BlockSpec((None, mod3.shape[1], D_MODEL), lambda i: (_segment_of_tile(i), 0, 0)),
        ],
        out_specs=[row_spec, row_spec],
        out_shape=[jax.ShapeDtypeStruct((M_ROWS, D_MODEL), F32), jax.ShapeDtypeStruct((M_ROWS, D_MODEL), BF16)],
        compiler_params=_params(("arbitrary",)),
        name="embed_norm",
    )(x, ctx, norm_w.reshape(1, D_MODEL), mod3)


def _norm_mod_kernel(x_ref, w_ref, shift_ref, scale_ref, o_ref, *, tm):
    row0 = pl.program_id(0) * tm
    second = row0 >= TB
    within = row0 - jnp.where(second, TB, 0)
    one_segment = (within >= CTX_LEN) & (within + tm <= TB)
    x = x_ref[...]
    y = x * lax.rsqrt(jnp.mean(x * x, axis=-1, keepdims=True) + EPS)

    def finish(pick):
        o_ref[...] = (y * (w_ref[...] * (1.0 + pick(scale_ref))) + pick(shift_ref)).astype(BF16)

    @pl.when(one_segment)
    def _():
        seg = second.astype(jnp.int32)
        finish(lambda tab_ref: tab_ref[pl.ds(seg, 1), :])

    @pl.when(jnp.logical_not(one_segment))
    def _():
        finish(lambda tab_ref: _row_select(tab_ref, row0, tm))


def _norm_mod(h, norm_w, mod, shift_idx):
    tm = MM_TM_SMALL
    return pl.pallas_call(
        functools.partial(_norm_mod_kernel, tm=tm),
        grid=(M_ROWS // tm,),
        in_specs=[
            pl.BlockSpec((tm, D_MODEL), lambda i: (i, 0)),
            pl.BlockSpec((1, D_MODEL), lambda i: (0, 0)),
            pl.BlockSpec((8, D_MODEL), lambda i: (0, shift_idx)),
            pl.BlockSpec((8, D_MODEL), lambda i: (0, shift_idx + 1)),
        ],
        out_specs=pl.BlockSpec((tm, D_MODEL), lambda i: (i, 0)),
        out_shape=jax.ShapeDtypeStruct((M_ROWS, D_MODEL), BF16),
        compiler_params=_params(("arbitrary",)),
        name="norm_mod",
    )(h, norm_w.reshape(1, D_MODEL), mod, mod)


def _final_norm_kernel(x_ref, w_ref, o_ref):
    x = x_ref[...]
    o_ref[...] = x * lax.rsqrt(jnp.mean(x * x, axis=-1, keepdims=True) + EPS) * w_ref[...]


def _final_norm(h, norm_w):
    tiles = TB // ROW_TILE
    ctx_tiles = CTX_LEN // ROW_TILE
    return pl.pallas_call(
        _final_norm_kernel,
        grid=(BATCH, SEQ // ROW_TILE),
        in_specs=[
            pl.BlockSpec((ROW_TILE, D_MODEL), lambda b, t: (b * tiles + ctx_tiles + t, 0)),
            pl.BlockSpec((1, D_MODEL), lambda b, t: (0, 0)),
        ],
        out_specs=pl.BlockSpec((None, ROW_TILE, D_MODEL), lambda b, t: (b, t, 0)),
        out_shape=jax.ShapeDtypeStruct((BATCH, SEQ, D_MODEL), F32),
        compiler_params=_params(("arbitrary", "arbitrary")),
        name="final_norm",
    )(h, norm_w.reshape(1, D_MODEL))


def _row_select(tab_ref, row0, tm):
    row = row0 + lax.broadcasted_iota(jnp.int32, (tm, 1), 0)
    second = row >= TB
    is_ctx = (row - jnp.where(second, TB, 0)) < CTX_LEN
    return jnp.where(is_ctx, tab_ref[2:3, :], jnp.where(second, tab_ref[1:2, :], tab_ref[0:1, :]))


def _mm_ws_kernel(*refs, k_sizes, epilogue, norm, tm, cast, n_sub, side_cast, side_mod_blocks):
    n_a = len(k_sizes)
    a_refs = refs[:n_a]
    w_ref = refs[n_a]
    pos = n_a + 1
    if norm:
        nw_ref = refs[pos]
        pos += 1
    if epilogue in ("gated_residual", "gated_residual_norm"):
        res_ref, gate_ref = refs[pos], refs[pos + 1]
        pos += 2
    if epilogue == "gated_residual_norm":
        n2w_ref, shift_ref, scale_ref = refs[pos:pos + 3]
        pos += 3
    if side_cast:
        side_in_ref = refs[pos]
        pos += 1
    if side_mod_blocks:
        ct_ref, wmod_ref, bmod_ref = refs[pos:pos + 3]
        pos += 3
    o_ref = refs[pos]
    pos += 1
    if epilogue == "gated_residual_norm":
        n_ref = refs[pos]
        pos += 1
    i = pl.program_id(1)
    if side_cast:
        side_out_ref = refs[pos]
        pos += 1
    if side_mod_blocks:
        mod_ref = refs[pos]
        pos += 1

    if cast:
        wb_ref = refs[pos]

        @pl.when(i == 0)
        def _():
            wb_ref[...] = w_ref[...].astype(BF16)
    else:
        wb_ref = w_ref

    ts = tm // n_sub
    for sub in range(n_sub):
        rs = slice(sub * ts, (sub + 1) * ts)
        row0 = i * tm + sub * ts
        acc = None
        off = 0
        for a_ref, ks in zip(a_refs, k_sizes):
            a = a_ref[rs, :]
            if norm:
                af = a.astype(F32)
                af = af * lax.rsqrt(jnp.mean(af * af, axis=-1, keepdims=True) + EPS) * nw_ref[...]
                a = af.astype(BF16)
            part = jnp.dot(a, wb_ref[off:off + ks, :], preferred_element_type=F32)
            acc = part if acc is None else acc + part
            off += ks

        if epilogue == "bf16":
            o_ref[rs, :] = acc.astype(BF16)
        elif epilogue == "relu2":
            r = jnp.maximum(acc, 0.0)
            o_ref[rs, :] = (r * r).astype(BF16)
        else:
            second = row0 >= TB
            within = row0 - jnp.where(second, TB, 0)
            one_segment = (within >= CTX_LEN) & (within + ts <= TB)

            def finish(pick, acc=acc, rs=rs):
                h = res_ref[rs, :] + pick(gate_ref) * acc
                o_ref[rs, :] = h
                if epilogue == "gated_residual_norm":
                    y = h * lax.rsqrt(jnp.mean(h * h, axis=-1, keepdims=True) + EPS)
                    n_ref[rs, :] = (y * (n2w_ref[...] * (1.0 + pick(scale_ref))) + pick(shift_ref)).astype(BF16)

            @pl.when(one_segment)
            def _(second=second, finish=finish):
                seg = second.astype(jnp.int32)
                finish(lambda tab_ref: tab_ref[pl.ds(seg, 1), :])

            @pl.when(jnp.logical_not(one_segment))
            def _(row0=row0, finish=finish):
                finish(lambda tab_ref: _row_select(tab_ref, row0, ts))

    if side_cast:
        side_out_ref[...] = side_in_ref[...].astype(BF16)
    if side_mod_blocks:
        s = jax.nn.silu(ct_ref[...])
        w = wmod_ref[...]
        row_id = lax.broadcasted_iota(jnp.int32, (8, w.shape[1]), 0)
        out = jnp.broadcast_to(bmod_ref[...], (8, w.shape[1]))
        for r in range(BATCH + 1):
            out = out + jnp.where(row_id == r, jnp.sum(w * s[:, r:r + 1], axis=0, keepdims=True), 0.0)
        mod_ref[...] = out


def _matmul_ws(a_list, w, layer, *, tn, tm, epilogue="bf16", a_col_blocks=None, k_sizes=None, norm_w=None,
               res=None, mod=None, gate_idx=0, next_norm_w=None, n_sub=1, side_cast=None, side_mod=None,
               single_buffer_w=False, name="matmul"):
    assert BATCH == 2
    _, kdim, n = w.shape
    if k_sizes is None:
        k_sizes = [a.shape[1] for a in a_list]
    if a_col_blocks is None:
        a_col_blocks = [0] * len(a_list)
    assert sum(k_sizes) == kdim and n % tn == 0 and M_ROWS % tm == 0
    row_tiles = M_ROWS // tm
    n_steps = (n // tn) * row_tiles
    step = lambda j, i: j * row_tiles + i
    gated = epilogue in ("gated_residual", "gated_residual_norm")
    in_specs = [pl.BlockSpec((tm, ks), functools.partial(lambda j, i, cb: (i, cb), cb=cb))
                for ks, cb in zip(k_sizes, a_col_blocks)]
    w_mode = dict(pipeline_mode=pl.Buffered(1)) if (n == tn or single_buffer_w) else {}
    in_specs.append(pl.BlockSpec((None, kdim, tn), lambda j, i: (layer, 0, j), **w_mode))
    operands = list(a_list) + [w]
    if norm_w is not None:
        assert len(a_list) == 1
        in_specs.append(pl.BlockSpec((1, kdim), lambda j, i: (0, 0)))
        operands.append(norm_w.reshape(1, kdim))
    out_specs = [pl.BlockSpec((tm, tn), lambda j, i: (i, j))]
    out_shape = [jax.ShapeDtypeStruct((M_ROWS, n), F32 if gated else BF16)]
    if gated:
        blocks_per_vec = D_MODEL // tn
        in_specs.append(pl.BlockSpec((tm, tn), lambda j, i: (i, j)))
        in_specs.append(pl.BlockSpec((8, tn), lambda j, i: (0, gate_idx * blocks_per_vec + j)))
        operands += [res, mod]
    if epilogue == "gated_residual_norm":
        assert tn == n == D_MODEL
        in_specs.append(pl.BlockSpec((1, D_MODEL), lambda j, i: (0, 0)))
        in_specs.append(pl.BlockSpec((8, D_MODEL), lambda j, i: (0, gate_idx + 1)))
        in_specs.append(pl.BlockSpec((8, D_MODEL), lambda j, i: (0, gate_idx + 2)))
        operands += [next_norm_w.reshape(1, D_MODEL), mod, mod]
        out_specs.append(pl.BlockSpec((tm, tn), lambda j, i: (i, j)))
        out_shape.append(jax.ShapeDtypeStruct((M_ROWS, n), BF16))
    if side_cast is not None:
        side_w, side_layer = side_cast
        _, side_k, side_n = side_w.shape
        slab = side_k // n_steps
        assert slab * n_steps == side_k
        in_specs.append(pl.BlockSpec((None, slab, side_n), lambda j, i: (side_layer, step(j, i), 0)))
        operands.append(side_w)
        out_specs.append(pl.BlockSpec((slab, side_n), lambda j, i: (step(j, i), 0)))
        out_shape.append(jax.ShapeDtypeStruct((side_k, side_n), BF16))
    side_mod_blocks = 0
    if side_mod is not None:
        ct, w_mod, b_mod, mod_layer, first_blk, side_mod_blocks = side_mod
        assert side_mod_blocks <= n_steps
        rel = lambda j, i: jnp.minimum(step(j, i), side_mod_blocks - 1)
        in_specs.append(pl.BlockSpec((D_MODEL, 8), lambda j, i: (0, 0)))
        in_specs.append(pl.BlockSpec((None, D_MODEL, MOD_COLS), lambda j, i: (mod_layer, 0, first_blk + rel(j, i))))
        in_specs.append(pl.BlockSpec((None, 1, MOD_COLS), lambda j, i: (mod_layer, 0, first_blk + rel(j, i))))
        operands += [ct, w_mod, b_mod.reshape(DEPTH, 1, 6 * D_MODEL)]
        out_specs.append(pl.BlockSpec((8, MOD_COLS), lambda j, i: (0, rel(j, i))))
        out_shape.append(jax.ShapeDtypeStruct((8, side_mod_blocks * MOD_COLS), F32))
    cast = w.dtype != BF16
    outs = pl.pallas_call(
        functools.partial(_mm_ws_kernel, k_sizes=tuple(k_sizes), epilogue=epilogue, norm=norm_w is not None,
                          tm=tm, cast=cast, n_sub=n_sub, side_cast=side_cast is not None,
                          side_mod_blocks=side_mod_blocks),
        grid=(n // tn, row_tiles),
        in_specs=in_specs,
        out_specs=out_specs,
        out_shape=out_shape,
        scratch_shapes=[pltpu.VMEM((kdim, tn), BF16)] if cast else [],
        compiler_params=_params(("arbitrary", "arbitrary")),
        name=name,
    )(*operands)
    return outs[0] if len(outs) == 1 else outs


def _retention_kernel(q_ref, k_ref, v_ref, g_ref, cos_ref, sin_ref, ld_ref, gn_ref, o_ref, kb_ref, kv_ref, st_ref):
    c = RET_CHUNK
    ctx_chunks = CTX_LEN // c
    lat_chunks = SEQ // c
    n_chunks = ctx_chunks + lat_chunks
    sc = HEAD_DIM ** -0.5
    lg = jnp.log1p(-jnp.exp2(ld_ref[...]))
    lgf = lg[0:1, 0:1]
    lgb = lg[1:2, 0:1]
    ii = lax.broadcasted_iota(jnp.int32, (c, 1), 0).astype(F32)
    jj = lax.broadcasted_iota(jnp.int32, (1, c), 1).astype(F32)
    diff = ii - jj
    dmat = jnp.where(diff >= 0, jnp.exp(lgf * jnp.maximum(diff, 0.0)), jnp.exp(lgb * jnp.maximum(-diff, 0.0)))
    qf_dec = jnp.exp(lgf * (ii + 1.0))
    qb_dec = jnp.exp(lgb * (c - ii))
    kf_dec = jnp.exp(lgf * (c - 1.0 - ii))
    kb_dec = jnp.exp(lgb * ii)
    gcf = jnp.exp(lgf * c)
    gcb = jnp.exp(lgb * c)

    def rows(n):
        return pl.ds(pl.multiple_of(n * c, c), c)

    def roped(ref, n):
        x = ref[rows(n), :].astype(F32)
        return _rotate(x, cos_ref[rows(n), :], sin_ref[rows(n), :], HEAD_DIM // 2)

    def increments(n, carry):
        k = roped(k_ref, n)
        kb_ref[rows(n), :] = k.astype(BF16)
        kd = jnp.concatenate([(k * kf_dec).astype(BF16), (k * kb_dec).astype(BF16)], axis=1)
        kv_ref[n] = lax.dot_general(kd, v_ref[rows(n), :], TN_DIMS, preferred_element_type=F32)
        return carry

    def recurrence(n0, cnt, s_f, s_b):
        def body(t, carry):
            s_f, s_b = carry
            nf = n0 + t
            nb = n0 + cnt - 1 - t
            st_ref[nf, 0:c, :] = s_f.astype(BF16)
            st_ref[nb, c:2 * c, :] = s_b.astype(BF16)
            return gcf * s_f + kv_ref[nf, 0:c, :], gcb * s_b + kv_ref[nb, c:2 * c, :]
        return lax.fori_loop(0, cnt, body, (s_f, s_b))

    def outputs(n, carry):
        q = roped(q_ref, n) * sc
        v = v_ref[rows(n), :]
        scores = lax.dot_general(q.astype(BF16), kb_ref[rows(n), :], NT_DIMS, preferred_element_type=F32) * dmat
        qd = jnp.concatenate([(q * qf_dec).astype(BF16), (q * qb_dec).astype(BF16)], axis=1)
        o = (jnp.dot(scores.astype(BF16), v, preferred_element_type=F32)
             + jnp.dot(qd, st_ref[n], preferred_element_type=F32))
        mu = jnp.mean(o, axis=-1, keepdims=True)
        oc = o - mu
        var = jnp.mean(oc * oc, axis=-1, keepdims=True)
        on = oc * lax.rsqrt(var + EPS) * gn_ref[...]
        g = g_ref[rows(n), :].astype(F32)
        o_ref[rows(n), :] = (jax.nn.silu(g) * on).astype(BF16)
        return carry

    lax.fori_loop(0, n_chunks, increments, 0, unroll=RET_UNROLL_STATE)
    zero = jnp.zeros((c, c), F32)
    s_f, s_b = recurrence(0, ctx_chunks, zero, zero)
    recurrence(ctx_chunks, lat_chunks, s_f, s_b)
    lax.fori_loop(0, n_chunks, outputs, 0, unroll=RET_UNROLL_OUT)


def _retention(proj, log2_decay, gn_w, cos, sin):
    h = RET_HEADS
    ld = jnp.broadcast_to(log2_decay.T[:, :, None], (h, 2, LANES))
    seq_spec = lambda cb0: pl.BlockSpec((TB, HEAD_DIM), functools.partial(lambda b, hh, cb0: (b, cb0 + hh), cb0=cb0))
    tab_spec = pl.BlockSpec((TB, LANES), lambda b, hh: (0, 0))
    return pl.pallas_call(
        _retention_kernel,
        grid=(BATCH, h),
        in_specs=[seq_spec(0), seq_spec(h), seq_spec(2 * h), seq_spec(3 * h), tab_spec, tab_spec,
                  pl.BlockSpec((None, 2, LANES), lambda b, hh: (hh, 0, 0)),
                  pl.BlockSpec((1, HEAD_DIM), lambda b, hh: (0, hh))],
        out_specs=pl.BlockSpec((TB, HEAD_DIM), lambda b, hh: (b, hh)),
        out_shape=jax.ShapeDtypeStruct((M_ROWS, h * HEAD_DIM), BF16),
        scratch_shapes=[pltpu.VMEM((TB, HEAD_DIM), BF16),
                        pltpu.VMEM((TB // RET_CHUNK, 2 * RET_CHUNK, HEAD_DIM), F32),
                        pltpu.VMEM((TB // RET_CHUNK, 2 * RET_CHUNK, HEAD_DIM), BF16)],
        compiler_params=_params(("arbitrary", "arbitrary")),
        name="retention",
    )(proj, proj, proj, proj, cos, sin, ld, gn_w.reshape(1, h * HEAD_DIM))


def _window_kernel(sink_ref, q_ref, k_ref, v_ref, cos_ref, sin_ref, o_ref, kr_ref, vx_ref):
    blk = WIN_BLOCK
    ctx_chunks = CTX_LEN // blk
    n_chunks = TB // blk
    kvh = pl.program_id(1)
    sc = HEAD_DIM ** -0.5
    half = HEAD_DIM // 4
    rows_q = WIN_GROUP * blk

    kr_ref[...] = _rotate(k_ref[...].astype(F32), cos_ref[...], sin_ref[...], half).astype(BF16)
    vx_ref[:, 0:HEAD_DIM] = v_ref[...]
    vx_ref[:, HEAD_DIM:2 * HEAD_DIM] = jnp.ones((TB, HEAD_DIM), BF16)

    row_iota = lax.broadcasted_iota(jnp.int32, (rows_q, 1), 0)
    head = row_iota // blk
    sink = jnp.zeros((rows_q, 1), F32)
    for g in range(WIN_GROUP):
        sink = jnp.where(head == g, sink_ref[kvh * WIN_GROUP + g], sink)

    def chunk(n, carry):
        rq = pl.ds(pl.multiple_of(n * blk, blk), blk)
        cos = cos_ref[rq, :]
        sin = sin_ref[rq, :]
        q = jnp.concatenate(
            [(_rotate(q_ref[rq, g * HEAD_DIM:(g + 1) * HEAD_DIM].astype(F32), cos, sin, half) * sc).astype(BF16)
             for g in range(WIN_GROUP)], axis=0)

        cstart = jnp.clip(n - 1, ctx_chunks, n_chunks - 3)
        rw = pl.ds(pl.multiple_of(cstart * blk, blk), 3 * blk)
        s_c = lax.dot_general(q, kr_ref[0:CTX_LEN, :], NT_DIMS, preferred_element_type=F32)
        s_w = lax.dot_general(q, kr_ref[rw, :], NT_DIMS, preferred_element_type=F32)
        qpos = (n - ctx_chunks) * blk + lax.broadcasted_iota(jnp.int32, (blk, 1), 0)
        kpos = (cstart - ctx_chunks) * blk + lax.broadcasted_iota(jnp.int32, (1, 3 * blk), 1)
        valid = (jnp.abs(qpos - kpos) <= WIN_RADIUS) & (n >= ctx_chunks)
        s_w = jnp.concatenate([jnp.where(valid, s_w[g * blk:(g + 1) * blk, :], NEG_INF)
                               for g in range(WIN_GROUP)], axis=0)

        m = jnp.maximum(jnp.maximum(jnp.max(s_c, axis=-1, keepdims=True), jnp.max(s_w, axis=-1, keepdims=True)),
                        sink)
        p_c = jnp.exp(s_c - m).astype(BF16)
        p_w = jnp.exp(s_w - m).astype(BF16)
        ox = (jnp.dot(p_c, vx_ref[0:CTX_LEN, :], preferred_element_type=F32)
              + jnp.dot(p_w, vx_ref[rw, :], preferred_element_type=F32))
        o = ox[:, 0:HEAD_DIM] / (ox[:, HEAD_DIM:2 * HEAD_DIM] + jnp.exp(sink - m))
        for g in range(WIN_GROUP):
            o_ref[rq, g * HEAD_DIM:(g + 1) * HEAD_DIM] = o[g * blk:(g + 1) * blk, :].astype(BF16)
        return carry

    lax.fori_loop(0, n_chunks, chunk, 0, unroll=2)


def _window_attention(proj, sink, cos, sin):
    gw = WIN_GROUP * HEAD_DIM
    q_cb0 = 4 * RET_HEADS * HEAD_DIM // gw
    k_cb0 = (4 * RET_HEADS + WIN_HEADS) * HEAD_DIM // HEAD_DIM
    v_cb0 = k_cb0 + WIN_KV_HEADS
    tab_spec = pl.BlockSpec((TB, LANES), lambda b, kh: (0, 0))
    return pl.pallas_call(
        _window_kernel,
        grid=(BATCH, WIN_KV_HEADS),
        in_specs=[
            pl.BlockSpec(memory_space=pltpu.SMEM),
            pl.BlockSpec((TB, gw), lambda b, kh: (b, q_cb0 + kh)),
            pl.BlockSpec((TB, HEAD_DIM), lambda b, kh: (b, k_cb0 + kh)),
            pl.BlockSpec((TB, HEAD_DIM), lambda b, kh: (b, v_cb0 + kh)),
            tab_spec, tab_spec,
        ],
        out_specs=pl.BlockSpec((TB, gw), lambda b, kh: (b, kh)),
        out_shape=jax.ShapeDtypeStruct((M_ROWS, WIN_HEADS * HEAD_DIM), BF16),
        scratch_shapes=[pltpu.VMEM((TB, HEAD_DIM), BF16), pltpu.VMEM((TB, 2 * HEAD_DIM), BF16)],
        compiler_params=_params(("arbitrary", "arbitrary")),
        name="window_attention",
    )(sink, proj, proj, proj, cos, sin)


def _conv_kernel(b_ref, c_ref, x_ref, w_ref, o_ref):
    u = c_ref[...].astype(F32) * x_ref[...].astype(F32)
    row = lax.broadcasted_iota(jnp.int32, (TB, 1), 0)
    prev = jnp.where((row == 0) | (row == CTX_LEN), 0.0, pltpu.roll(u, 1, 0))
    nxt = jnp.where((row == CTX_LEN - 1) | (row == TB - 1), 0.0, pltpu.roll(u, TB - 1, 0))
    z = prev * w_ref[0:1, :] + u * w_ref[1:2, :] + nxt * w_ref[2:3, :]
    o_ref[...] = (b_ref[...].astype(F32) * z).astype(BF16)


def _short_conv(proj, conv_w):
    tc = 256
    nblk = CONV_WIDTH // tc
    spec = lambda part: pl.BlockSpec((TB, tc), functools.partial(lambda b, j, part: (b, part * nblk + j), part=part))
    return pl.pallas_call(
        _conv_kernel,
        grid=(BATCH, nblk),
        in_specs=[spec(0), spec(1), spec(2), pl.BlockSpec((3, tc), lambda b, j: (0, j))],
        out_specs=pl.BlockSpec((TB, tc), lambda b, j: (b, j)),
        out_shape=jax.ShapeDtypeStruct((M_ROWS, CONV_WIDTH), BF16),
        compiler_params=_params(("arbitrary", "arbitrary")),
        name="short_conv",
    )(proj, proj, proj, conv_w)


MLA_HEADS_PER_STEP = 2
MLA_Q_TILE = 512
MLA_KEY_CHUNK = 1024


def _mla_kernel(q_ref, kv_ref, kr_ref, cos_ref, sin_ref, o_ref, kc_ref, vx_ref):
    sc = (MLA_NOPE + MLA_ROPE) ** -0.5
    half = MLA_ROPE // 4
    kv_w = MLA_NOPE + MLA_V

    kr = _rotate(kr_ref[...].astype(F32), cos_ref[...], sin_ref[...], half).astype(BF16)
    for hh in range(MLA_HEADS_PER_STEP):
        kc_ref[hh, :, 0:MLA_NOPE] = kv_ref[:, hh * kv_w:hh * kv_w + MLA_NOPE]
        kc_ref[hh, :, MLA_NOPE:MLA_QK_PAD] = kr
        vx_ref[hh, :, 0:MLA_V] = kv_ref[:, hh * kv_w + MLA_NOPE:(hh + 1) * kv_w]
        vx_ref[hh, :, MLA_V:2 * MLA_V] = jnp.ones((TB, MLA_V), BF16)

    def attend(row0, n_rows, key_chunks):
        rq = pl.ds(row0, n_rows)
        cos = cos_ref[rq, :]
        sin = sin_ref[rq, :]
        for hh in range(MLA_HEADS_PER_STEP):
            c0 = hh * MLA_QK_PAD
            qn = (q_ref[rq, c0:c0 + MLA_NOPE].astype(F32) * sc).astype(BF16)
            qr = (_rotate(q_ref[rq, c0 + MLA_NOPE:c0 + MLA_QK_PAD].astype(F32), cos, sin, half) * sc).astype(BF16)
            q = jnp.concatenate([qn, qr], axis=1)
            m = None
            acc = None
            for k0, kn in key_chunks:
                s = lax.dot_general(q, kc_ref[hh, k0:k0 + kn, :], NT_DIMS, preferred_element_type=F32)
                m_chunk = jnp.max(s, axis=-1, keepdims=True)
                m_new = m_chunk if m is None else jnp.maximum(m, m_chunk)
                p = jnp.exp(s - m_new).astype(BF16)
                pv = jnp.dot(p, vx_ref[hh, k0:k0 + kn, :], preferred_element_type=F32)
                acc = pv if acc is None else acc * jnp.exp(m - m_new) + pv
                m = m_new
            o_ref[rq, hh * MLA_V:(hh + 1) * MLA_V] = (acc[:, 0:MLA_V] / acc[:, MLA_V:2 * MLA_V]).astype(BF16)

    ctx_keys = [(0, CTX_LEN)]
    all_keys = ctx_keys + [(CTX_LEN + k * MLA_KEY_CHUNK, MLA_KEY_CHUNK) for k in range(SEQ // MLA_KEY_CHUNK)]
    attend(0, CTX_LEN, ctx_keys)

    def latent_tile(t, carry):
        attend(pl.multiple_of(CTX_LEN + t * MLA_Q_TILE, ROW_TILE), MLA_Q_TILE, all_keys)
        return carry

    lax.fori_loop(0, SEQ // MLA_Q_TILE, latent_tile, 0)


def _mla_attention(qcat, kv, proj, cos, sin):
    hps = MLA_HEADS_PER_STEP
    kr_cb = (3 * CONV_WIDTH + MLA_Q_RANK + MLA_KV_RANK) // LANES
    tab_spec = pl.BlockSpec((TB, LANES), lambda b, h: (0, 0))
    return pl.pallas_call(
        _mla_kernel,
        grid=(BATCH, MLA_HEADS // hps),
        in_specs=[
            pl.BlockSpec((TB, hps * MLA_QK_PAD), lambda b, h: (b, h)),
            pl.BlockSpec((TB, hps * (MLA_NOPE + MLA_V)), lambda b, h: (b, h)),
            pl.BlockSpec((TB, LANES), lambda b, h: (b, kr_cb)),
            tab_spec, tab_spec,
        ],
        out_specs=pl.BlockSpec((TB, hps * MLA_V), lambda b, h: (b, h)),
        out_shape=jax.ShapeDtypeStruct((M_ROWS, MLA_HEADS * MLA_V), BF16),
        scratch_shapes=[pltpu.VMEM((hps, TB, MLA_QK_PAD), BF16), pltpu.VMEM((hps, TB, 2 * MLA_V), BF16)],
        compiler_params=_params(("arbitrary", "arbitrary")),
        name="mla_attention",
    )(qcat, kv, proj, cos, sin)


def kernel(x, c, ctx, c_ctx, w_mod, b_mod, norm1_w, norm2_w, mlp_w1, mlp_w2, ev_w_in, ev_ret_log2_decay,
           ev_ret_gn_w, ev_sink, ev_w_out, od_w_in, od_conv_w, od_q_norm_w, od_kv_norm_w, od_w_uq, od_w_ukv,
           od_w_out, norm_f):
    assert x.shape == (BATCH, SEQ, D_MODEL) and ctx.shape == (BATCH, CTX_LEN, D_MODEL)
    rows = SEQ // GRID_W
    row = np.repeat(np.arange(rows), GRID_W)
    col = np.tile(np.arange(GRID_W), rows)
    pos = np.arange(SEQ)
    ret_cos, ret_sin = _rope_tables([pos], HEAD_DIM)
    win_cos, win_sin = _rope_tables([row, col], HEAD_DIM)
    mla_cos, mla_sin = _rope_tables([row, col], MLA_ROPE)

    cvec = jnp.concatenate([c, c_ctx[None], jnp.zeros((8 - BATCH - 1, D_MODEL), F32)], axis=0)
    mod_head = _modulation(cvec, w_mod, b_mod, 2 * D_MODEL)
    c_cols = cvec.T
    head_blocks = 2 * D_MODEL // MOD_COLS
    all_blocks = 6 * D_MODEL // MOD_COLS

    n_odd = od_w_uq.shape[0]
    w_uq = jnp.pad(od_w_uq.astype(BF16).reshape(n_odd, MLA_Q_RANK, MLA_HEADS, MLA_NOPE + MLA_ROPE),
                   ((0, 0), (0, 0), (0, 0), (0, MLA_QK_PAD - MLA_NOPE - MLA_ROPE))
                   ).reshape(n_odd, MLA_Q_RANK, MLA_HEADS * MLA_QK_PAD)

    w_in_odd = jnp.pad(od_w_in.astype(BF16), ((0, 0), (0, 0), (0, ODD_IN_PAD - ODD_IN)))

    for layer in range(DEPTH):
        j = layer // 2
        if layer == 0:
            h, n1 = _embed_norm(x, ctx, norm1_w[layer], mod_head.reshape(8, 2, D_MODEL))
        else:
            n1 = _norm_mod(h, norm1_w[layer], mod2, 0)
        if layer % 2 == 0:
            side_mod = (c_cols, w_mod, b_mod, 0, head_blocks, all_blocks - head_blocks) if layer == 0 else None
            proj = _matmul_ws([n1], ev_w_in, j, tn=EVEN_IN // 4, tm=MM_TM_SMALL, side_mod=side_mod,
                              name="even_in_proj")
            if layer == 0:
                proj, mod_tail = proj
                mod2 = jnp.concatenate([mod_head, mod_tail], axis=1)
            mix_a = _retention(proj, ev_ret_log2_decay[j], ev_ret_gn_w[j], ret_cos, ret_sin)
            mix_b = _window_attention(proj, ev_sink[j], win_cos, win_sin)
            w_out = ev_w_out
        else:
            proj = _matmul_ws([n1], w_in_odd, j, tn=ODD_IN_PAD, tm=MM_TM_SMALL, name="odd_in_proj")
            mix_a = _short_conv(proj, od_conv_w[j])
            qcat = _matmul_ws([proj], w_uq, j, tn=MLA_HEADS * MLA_QK_PAD, tm=MM_TM_BIG, k_sizes=[MLA_Q_RANK],
                              a_col_blocks=[3 * CONV_WIDTH // MLA_Q_RANK], norm_w=od_q_norm_w[j], name="mla_q_up")
            kv = _matmul_ws([proj], od_w_ukv, j, tn=MLA_HEADS * (MLA_NOPE + MLA_V), tm=MM_TM_BIG,
                            k_sizes=[MLA_KV_RANK], a_col_blocks=[(3 * CONV_WIDTH + MLA_Q_RANK) // MLA_KV_RANK],
                            norm_w=od_kv_norm_w[j], name="mla_kv_up")
            mix_b = _mla_attention(qcat, kv, proj, mla_cos, mla_sin)
            w_out = od_w_out
        h, n2 = _matmul_ws([mix_a, mix_b], w_out, j, tn=D_MODEL, tm=MM_TM_SMALL, epilogue="gated_residual_norm",
                           res=h, mod=mod2, gate_idx=2, next_norm_w=norm2_w[layer], n_sub=2, name="mixer_out_proj")
        side_mod = (c_cols, w_mod, b_mod, layer + 1, 0, all_blocks) if layer + 1 < DEPTH else None
        ff, w2_bf16, *mod_next = _matmul_ws([n2], mlp_w1, layer, tn=1024, tm=MM_TM_BIG, epilogue="relu2",
                                            side_cast=(mlp_w2, layer), side_mod=side_mod, name="mlp_up")
        h = _matmul_ws([ff], w2_bf16[None], 0, tn=1024, tm=MM_TM_DOWN, epilogue="gated_residual",
                       res=h, mod=mod2, gate_idx=5, single_buffer_w=True, name="mlp_down")
        if mod_next:
            mod2 = mod_next[0]
    return _final_norm(h, norm_f)
```

```python
import functools

import numpy as np
import jax
import jax.numpy as jnp
from jax import lax
from jax.experimental import pallas as pl
from jax.experimental.pallas import tpu as pltpu

F32 = jnp.float32
BF16 = jnp.bfloat16

D_MODEL = 2048
BATCH = 2
SEQ = 4096
DEPTH = 4
GRID_W = 64
CTX_LEN = 256
HEAD_DIM = 128
ROPE_BASE = 10000.0
EPS = 1e-6
NEG_INF = -1e30

RET_HEADS = 8
RET_CHUNK = 128
RET_UNROLL_STATE = 8
RET_UNROLL_OUT = 17
WIN_HEADS = 8
WIN_KV_HEADS = 2
WIN_GROUP = WIN_HEADS // WIN_KV_HEADS
WIN_RADIUS = 128
WIN_BLOCK = 128
CONV_WIDTH = 1024
MLA_HEADS = 8
MLA_Q_RANK = 512
MLA_KV_RANK = 256
MLA_NOPE = 128
MLA_ROPE = 64
MLA_V = 128
D_FF = 4 * D_MODEL

EVEN_IN = 4 * RET_HEADS * HEAD_DIM + (WIN_HEADS + 2 * WIN_KV_HEADS) * HEAD_DIM
ODD_IN = 3 * CONV_WIDTH + MLA_Q_RANK + MLA_KV_RANK + MLA_ROPE
ODD_IN_PAD = 3968
MLA_QK_PAD = 256

TB = CTX_LEN + SEQ
M_ROWS = BATCH * TB
LANES = 128
ROW_TILE = 256
MM_TM_BIG = M_ROWS // 8
MM_TM_SMALL = M_ROWS // 16
MOD_COLS = 256
MM_TM_DOWN = 512
VMEM_LIMIT = 60 * 1024 * 1024

NT_DIMS = (((1,), (1,)), ((), ()))
TN_DIMS = (((0,), (0,)), ((), ()))


def _params(sem, vmem=VMEM_LIMIT):
    return pltpu.CompilerParams(dimension_semantics=sem, vmem_limit_bytes=vmem)


def _rope_tables(pos_list, d_rot):
    per = d_rot // len(pos_list)
    inv = ROPE_BASE ** (-np.arange(0, per, 2, dtype=np.float64) / per)
    cos_parts, sin_parts = [], []
    for pos in pos_list:
        ang = pos.astype(np.float64)[:, None] * inv[None, :]
        c, s = np.cos(ang), np.sin(ang)
        cos_parts += [c, c]
        sin_parts += [-s, s]
    pad = ((0, 0), (0, LANES - d_rot))
    cos = np.pad(np.concatenate(cos_parts, axis=-1), pad)
    sin = np.pad(np.concatenate(sin_parts, axis=-1), pad)
    ctx_cos = np.pad(np.ones((CTX_LEN, d_rot)), pad)
    ctx_sin = np.zeros((CTX_LEN, LANES))
    return (jnp.asarray(np.concatenate([ctx_cos, cos], axis=0), F32),
            jnp.asarray(np.concatenate([ctx_sin, sin], axis=0), F32))


def _rotate(x, cos, sin, half):
    if 2 * half == LANES:
        partner = pltpu.roll(x, half, 1)
    else:
        lane = lax.broadcasted_iota(jnp.int32, (1, LANES), 1)
        first = (lane % (2 * half)) < half
        partner = jnp.where(first, pltpu.roll(x, LANES - half, 1), pltpu.roll(x, half, 1))
    return x * cos + partner * sin


def _modulation_kernel(c_ref, w_ref, b_ref, o_ref):
    s = jax.nn.silu(c_ref[...]).astype(BF16)
    o_ref[...] = jnp.dot(s, w_ref[...].astype(BF16), preferred_element_type=F32) + b_ref[...]


def _modulation(cvec, w_mod, b_mod, n_cols):
    tn = 1024
    return pl.pallas_call(
        _modulation_kernel,
        grid=(n_cols // tn,),
        in_specs=[
            pl.BlockSpec((8, D_MODEL), lambda j: (0, 0)),
            pl.BlockSpec((None, D_MODEL, tn), lambda j: (0, 0, j)),
            pl.BlockSpec((None, 1, tn), lambda j: (0, 0, j)),
        ],
        out_specs=pl.BlockSpec((8, tn), lambda j: (0, j)),
        out_shape=jax.ShapeDtypeStruct((8, n_cols), F32),
        compiler_params=_params(("arbitrary",)),
        name="modulation",
    )(cvec, w_mod, b_mod.reshape(DEPTH, 1, 6 * D_MODEL))


def _segment_of_tile(i):
    tiles = TB // ROW_TILE
    return jnp.where(i % tiles == 0, BATCH, i // tiles)


def _embed_norm_kernel(x_ref, ctx_ref, w_ref, mod_ref, h_ref, n_ref):
    is_ctx = pl.program_id(0) % (TB // ROW_TILE) == 0
    x = jnp.where(is_ctx, ctx_ref[...], x_ref[...])
    h_ref[...] = x
    y = x * lax.rsqrt(jnp.mean(x * x, axis=-1, keepdims=True) + EPS) * w_ref[...]
    n_ref[...] = (y * (1.0 + mod_ref[1:2, :]) + mod_ref[0:1, :]).astype(BF16)


def _embed_norm(x, ctx, norm_w, mod3):
    assert CTX_LEN == ROW_TILE
    tiles = TB // ROW_TILE
    row_spec = pl.BlockSpec((ROW_TILE, D_MODEL), lambda i: (i, 0))
    return pl.pallas_call(
        _embed_norm_kernel,
        grid=(M_ROWS // ROW_TILE,),
        in_specs=[
            pl.BlockSpec((None, ROW_TILE, D_MODEL), lambda i: (i // tiles, jnp.maximum(i % tiles - 1, 0), 0)),
            pl.BlockSpec((None, ROW_TILE, D_MODEL), lambda i: (i // tiles, 0, 0)),
            pl.BlockSpec((1, D_MODEL), lambda i: (0, 0)),
            pl.BlockSpec((None, mod3.shape[1], D_MODEL), lambda i: (_segment_of_tile(i), 0, 0)),
        ],
        out_specs=[row_spec, row_spec],
        out_shape=[jax.ShapeDtypeStruct((M_ROWS, D_MODEL), F32), jax.ShapeDtypeStruct((M_ROWS, D_MODEL), BF16)],
        compiler_params=_params(("arbitrary",)),
        name="embed_norm",
    )(x, ctx, norm_w.reshape(1, D_MODEL), mod3)


def _norm_mod_kernel(x_ref, w_ref, shift_ref, scale_ref, o_ref, *, tm):
    row0 = pl.program_id(0) * tm
    second = row0 >= TB
    within = row0 - jnp.where(second, TB, 0)
    one_segment = (within >= CTX_LEN) & (within + tm <= TB)
    x = x_ref[...]
    y = x * lax.rsqrt(jnp.mean(x * x, axis=-1, keepdims=True) + EPS)

    def finish(pick):
        o_ref[...] = (y * (w_ref[...] * (1.0 + pick(scale_ref))) + pick(shift_ref)).astype(BF16)

    @pl.when(one_segment)
    def _():
        seg = second.astype(jnp.int32)
        finish(lambda tab_ref: tab_ref[pl.ds(seg, 1), :])

    @pl.when(jnp.logical_not(one_segment))
    def _():
        finish(lambda tab_ref: _row_select(tab_ref, row0, tm))


def _norm_mod(h, norm_w, mod, shift_idx):
    tm = MM_TM_SMALL
    return pl.pallas_call(
        functools.partial(_norm_mod_kernel, tm=tm),
        grid=(M_ROWS // tm,),
        in_specs=[
            pl.BlockSpec((tm, D_MODEL), lambda i: (i, 0)),
            pl.BlockSpec((1, D_MODEL), lambda i: (0, 0)),
            pl.BlockSpec((8, D_MODEL), lambda i: (0, shift_idx)),
            pl.BlockSpec((8, D_MODEL), lambda i: (0, shift_idx + 1)),
        ],
        out_specs=pl.BlockSpec((tm, D_MODEL), lambda i: (i, 0)),
        out_shape=jax.ShapeDtypeStruct((M_ROWS, D_MODEL), BF16),
        compiler_params=_params(("arbitrary",)),
        name="norm_mod",
    )(h, norm_w.reshape(1, D_MODEL), mod, mod)


def _final_norm_kernel(x_ref, w_ref, o_ref):
    x = x_ref[...]
    o_ref[...] = x * lax.rsqrt(jnp.mean(x * x, axis=-1, keepdims=True) + EPS) * w_ref[...]


def _final_norm(h, norm_w):
    tiles = TB // ROW_TILE
    ctx_tiles = CTX_LEN // ROW_TILE
    return pl.pallas_call(
        _final_norm_kernel,
        grid=(BATCH, SEQ // ROW_TILE),
        in_specs=[
            pl.BlockSpec((ROW_TILE, D_MODEL), lambda b, t: (b * tiles + ctx_tiles + t, 0)),
            pl.BlockSpec((1, D_MODEL), lambda b, t: (0, 0)),
        ],
        out_specs=pl.BlockSpec((None, ROW_TILE, D_MODEL), lambda b, t: (b, t, 0)),
        out_shape=jax.ShapeDtypeStruct((BATCH, SEQ, D_MODEL), F32),
        compiler_params=_params(("arbitrary", "arbitrary")),
        name="final_norm",
    )(h, norm_w.reshape(1, D_MODEL))


def _row_select(tab_ref, row0, tm):
    row = row0 + lax.broadcasted_iota(jnp.int32, (tm, 1), 0)
    second = row >= TB
    is_ctx = (row - jnp.where(second, TB, 0)) < CTX_LEN
    return jnp.where(is_ctx, tab_ref[2:3, :], jnp.where(second, tab_ref[1:2, :], tab_ref[0:1, :]))


def _mm_ws_kernel(*refs, k_sizes, epilogue, norm, tm, cast, n_sub, side_cast, side_mod_blocks):
    n_a = len(k_sizes)
    a_refs = refs[:n_a]
    w_ref = refs[n_a]
    pos = n_a + 1
    if norm:
        nw_ref = refs[pos]
        pos += 1
    if epilogue in ("gated_residual", "gated_residual_norm"):
        res_ref, gate_ref = refs[pos], refs[pos + 1]
        pos += 2
    if epilogue == "gated_residual_norm":
        n2w_ref, shift_ref, scale_ref = refs[pos:pos + 3]
        pos += 3
    if side_cast:
        side_in_ref = refs[pos]
        pos += 1
    if side_mod_blocks:
        ct_ref, wmod_ref, bmod_ref = refs[pos:pos + 3]
        pos += 3
    o_ref = refs[pos]
    pos += 1
    if epilogue == "gated_residual_norm":
        n_ref = refs[pos]
        pos += 1
    i = pl.program_id(1)
    if side_cast:
        side_out_ref = refs[pos]
        pos += 1
    if side_mod_blocks:
        mod_ref = refs[pos]
        pos += 1

    if cast:
        wb_ref = refs[pos]

        @pl.when(i == 0)
        def _():
            wb_ref[...] = w_ref[...].astype(BF16)
    else:
        wb_ref = w_ref

    ts = tm // n_sub
    for sub in range(n_sub):
        rs = slice(sub * ts, (sub + 1) * ts)
        row0 = i * tm + sub * ts
        acc = None
        off = 0
        for a_ref, ks in zip(a_refs, k_sizes):
            a = a_ref[rs, :]
            if norm:
                af = a.astype(F32)
                af = af * lax.rsqrt(jnp.mean(af * af, axis=-1, keepdims=True) + EPS) * nw_ref[...]
                a = af.astype(BF16)
            part = jnp.dot(a, wb_ref[off:off + ks, :], preferred_element_type=F32)
            acc = part if acc is None else acc + part
            off += ks

        if epilogue == "bf16":
            o_ref[rs, :] = acc.astype(BF16)
        elif epilogue == "relu2":
            r = jnp.maximum(acc, 0.0)
            o_ref[rs, :] = (r * r).astype(BF16)
        else:
            second = row0 >= TB
            within = row0 - jnp.where(second, TB, 0)
            one_segment = (within >= CTX_LEN) & (within + ts <= TB)

            def finish(pick, acc=acc, rs=rs):
                h = res_ref[rs, :] + pick(gate_ref) * acc
                o_ref[rs, :] = h
                if epilogue == "gated_residual_norm":
                    y = h * lax.rsqrt(jnp.mean(h * h, axis=-1, keepdims=True) + EPS)
                    n_ref[rs, :] = (y * (n2w_ref[...] * (1.0 + pick(scale_ref))) + pick(shift_ref)).astype(BF16)

            @pl.when(one_segment)
            def _(second=second, finish=finish):
                seg = second.astype(jnp.int32)
                finish(lambda tab_ref: tab_ref[pl.ds(seg, 1), :])

            @pl.when(jnp.logical_not(one_segment))
            def _(row0=row0, finish=finish):
                finish(lambda tab_ref: _row_select(tab_ref, row0, ts))

    if side_cast:
        side_out_ref[...] = side_in_ref[...].astype(BF16)
    if side_mod_blocks:
        s = jax.nn.silu(ct_ref[...])
        w = wmod_ref[...]
        row_id = lax.broadcasted_iota(jnp.int32, (8, w.shape[1]), 0)
        out = jnp.broadcast_to(bmod_ref[...], (8, w.shape[1]))
        for r in range(BATCH + 1):
            out = out + jnp.where(row_id == r, jnp.sum(w * s[:, r:r + 1], axis=0, keepdims=True), 0.0)
        mod_ref[...] = out


def _matmul_ws(a_list, w, layer, *, tn, tm, epilogue="bf16", a_col_blocks=None, k_sizes=None, norm_w=None,
               res=None, mod=None, gate_idx=0, next_norm_w=None, n_sub=1, side_cast=None, side_mod=None,
               single_buffer_w=False, name="matmul"):
    assert BATCH == 2
    _, kdim, n = w.shape
    if k_sizes is None:
        k_sizes = [a.shape[1] for a in a_list]
    if a_col_blocks is None:
        a_col_blocks = [0] * len(a_list)
    assert sum(k_sizes) == kdim and n % tn == 0 and M_ROWS % tm == 0
    row_tiles = M_ROWS // tm
    n_steps = (n // tn) * row_tiles
    step = lambda j, i: j * row_tiles + i
    gated = epilogue in ("gated_residual", "gated_residual_norm")
    in_specs = [pl.BlockSpec((tm, ks), functools.partial(lambda j, i, cb: (i, cb), cb=cb))
                for ks, cb in zip(k_sizes, a_col_blocks)]
    w_mode = dict(pipeline_mode=pl.Buffered(1)) if (n == tn or single_buffer_w) else {}
    in_specs.append(pl.BlockSpec((None, kdim, tn), lambda j, i: (layer, 0, j), **w_mode))
    operands = list(a_list) + [w]
    if norm_w is not None:
        assert len(a_list) == 1
        in_specs.append(pl.BlockSpec((1, kdim), lambda j, i: (0, 0)))
        operands.append(norm_w.reshape(1, kdim))
    out_specs = [pl.BlockSpec((tm, tn), lambda j, i: (i, j))]
    out_shape = [jax.ShapeDtypeStruct((M_ROWS, n), F32 if gated else BF16)]
    if gated:
        blocks_per_vec = D_MODEL // tn
        in_specs.append(pl.BlockSpec((tm, tn), lambda j, i: (i, j)))
        in_specs.append(pl.BlockSpec((8, tn), lambda j, i: (0, gate_idx * blocks_per_vec + j)))
        operands += [res, mod]
    if epilogue == "gated_residual_norm":
        assert tn == n == D_MODEL
        in_specs.append(pl.BlockSpec((1, D_MODEL), lambda j, i: (0, 0)))
        in_specs.append(pl.BlockSpec((8, D_MODEL), lambda j, i: (0, gate_idx + 1)))
        in_specs.append(pl.BlockSpec((8, D_MODEL), lambda j, i: (0, gate_idx + 2)))
        operands += [next_norm_w.reshape(1, D_MODEL), mod, mod]
        out_specs.append(pl.BlockSpec((tm, tn), lambda j, i: (i, j)))
        out_shape.append(jax.ShapeDtypeStruct((M_ROWS, n), BF16))
    if side_cast is not None:
        side_w, side_layer = side_cast
        _, side_k, side_n = side_w.shape
        slab = side_k // n_steps
        assert slab * n_steps == side_k
        in_specs.append(pl.BlockSpec((None, slab, side_n), lambda j, i: (side_layer, step(j, i), 0)))
        operands.append(side_w)
        out_specs.append(pl.BlockSpec((slab, side_n), lambda j, i: (step(j, i), 0)))
        out_shape.append(jax.ShapeDtypeStruct((side_k, side_n), BF16))
    side_mod_blocks = 0
    if side_mod is not None:
        ct, w_mod, b_mod, mod_layer, first_blk, side_mod_blocks = side_mod
        assert side_mod_blocks <= n_steps
        rel = lambda j, i: jnp.minimum(step(j, i), side_mod_blocks - 1)
        in_specs.append(pl.BlockSpec((D_MODEL, 8), lambda j, i: (0, 0)))
        in_specs.append(pl.BlockSpec((None, D_MODEL, MOD_COLS), lambda j, i: (mod_layer, 0, first_blk + rel(j, i))))
        in_specs.append(pl.BlockSpec((None, 1, MOD_COLS), lambda j, i: (mod_layer, 0, first_blk + rel(j, i))))
        operands += [ct, w_mod, b_mod.reshape(DEPTH, 1, 6 * D_MODEL)]
        out_specs.append(pl.BlockSpec((8, MOD_COLS), lambda j, i: (0, rel(j, i))))
        out_shape.append(jax.ShapeDtypeStruct((8, side_mod_blocks * MOD_COLS), F32))
    cast = w.dtype != BF16
    outs = pl.pallas_call(
        functools.partial(_mm_ws_kernel, k_sizes=tuple(k_sizes), epilogue=epilogue, norm=norm_w is not None,
                          tm=tm, cast=cast, n_sub=n_sub, side_cast=side_cast is not None,
                          side_mod_blocks=side_mod_blocks),
        grid=(n // tn, row_tiles),
        in_specs=in_specs,
        out_specs=out_specs,
        out_shape=out_shape,
        scratch_shapes=[pltpu.VMEM((kdim, tn), BF16)] if cast else [],
        compiler_params=_params(("arbitrary", "arbitrary")),
        name=name,
    )(*operands)
    return outs[0] if len(outs) == 1 else outs


def _retention_kernel(q_ref, k_ref, v_ref, g_ref, cos_ref, sin_ref, ld_ref, gn_ref, o_ref, kb_ref, kv_ref, st_ref):
    c = RET_CHUNK
    ctx_chunks = CTX_LEN // c
    lat_chunks = SEQ // c
    n_chunks = ctx_chunks + lat_chunks
    sc = HEAD_DIM ** -0.5
    lg = jnp.log1p(-jnp.exp2(ld_ref[...]))
    lgf = lg[0:1, 0:1]
    lgb = lg[1:2, 0:1]
    ii = lax.broadcasted_iota(jnp.int32, (c, 1), 0).astype(F32)
    jj = lax.broadcasted_iota(jnp.int32, (1, c), 1).astype(F32)
    diff = ii - jj
    dmat = jnp.where(diff >= 0, jnp.exp(lgf * jnp.maximum(diff, 0.0)), jnp.exp(lgb * jnp.maximum(-diff, 0.0)))
    qf_dec = jnp.exp(lgf * (ii + 1.0))
    qb_dec = jnp.exp(lgb * (c - ii))
    kf_dec = jnp.exp(lgf * (c - 1.0 - ii))
    kb_dec = jnp.exp(lgb * ii)
    gcf = jnp.exp(lgf * c)
    gcb = jnp.exp(lgb * c)

    def rows(n):
        return pl.ds(pl.multiple_of(n * c, c), c)

    def roped(ref, n):
        x = ref[rows(n), :].astype(F32)
        return _rotate(x, cos_ref[rows(n), :], sin_ref[rows(n), :], HEAD_DIM // 2)

    def increments(n, carry):
        k = roped(k_ref, n)
        kb_ref[rows(n), :] = k.astype(BF16)
        kd = jnp.concatenate([(k * kf_dec).astype(BF16), (k * kb_dec).astype(BF16)], axis=1)
        kv_ref[n] = lax.dot_general(kd, v_ref[rows(n), :], TN_DIMS, preferred_element_type=F32)
        return carry

    def recurrence(n0, cnt, s_f, s_b):
        def body(t, carry):
            s_f, s_b = carry
            nf = n0 + t
            nb = n0 + cnt - 1 - t
            st_ref[nf, 0:c, :] = s_f.astype(BF16)
            st_ref[nb, c:2 * c, :] = s_b.astype(BF16)
            return gcf * s_f + kv_ref[nf, 0:c, :], gcb * s_b + kv_ref[nb, c:2 * c, :]
        return lax.fori_loop(0, cnt, body, (s_f, s_b))

    def outputs(n, carry):
        q = roped(q_ref, n) * sc
        v = v_ref[rows(n), :]
        scores = lax.dot_general(q.astype(BF16), kb_ref[rows(n), :], NT_DIMS, preferred_element_type=F32) * dmat
        qd = jnp.concatenate([(q * qf_dec).astype(BF16), (q * qb_dec).astype(BF16)], axis=1)
        o = (jnp.dot(scores.astype(BF16), v, preferred_element_type=F32)
             + jnp.dot(qd, st_ref[n], preferred_element_type=F32))
        mu = jnp.mean(o, axis=-1, keepdims=True)
        oc = o - mu
        var = jnp.mean(oc * oc, axis=-1, keepdims=True)
        on = oc * lax.rsqrt(var + EPS) * gn_ref[...]
        g = g_ref[rows(n), :].astype(F32)
        o_ref[rows(n), :] = (jax.nn.silu(g) * on).astype(BF16)
        return carry

    lax.fori_loop(0, n_chunks, increments, 0, unroll=RET_UNROLL_STATE)
    zero = jnp.zeros((c, c), F32)
    s_f, s_b = recurrence(0, ctx_chunks, zero, zero)
    recurrence(ctx_chunks, lat_chunks, s_f, s_b)
    lax.fori_loop(0, n_chunks, outputs, 0, unroll=RET_UNROLL_OUT)


def _retention(proj, log2_decay, gn_w, cos, sin):
    h = RET_HEADS
    ld = jnp.broadcast_to(log2_decay.T[:, :, None], (h, 2, LANES))
    seq_spec = lambda cb0: pl.BlockSpec((TB, HEAD_DIM), functools.partial(lambda b, hh, cb0: (b, cb0 + hh), cb0=cb0))
    tab_spec = pl.BlockSpec((TB, LANES), lambda b, hh: (0, 0))
    return pl.pallas_call(
        _retention_kernel,
        grid=(BATCH, h),
        in_specs=[seq_spec(0), seq_spec(h), seq_spec(2 * h), seq_spec(3 * h), tab_spec, tab_spec,
                  pl.BlockSpec((None, 2, LANES), lambda b, hh: (hh, 0, 0)),
                  pl.BlockSpec((1, HEAD_DIM), lambda b, hh: (0, hh))],
        out_specs=pl.BlockSpec((TB, HEAD_DIM), lambda b, hh: (b, hh)),
        out_shape=jax.ShapeDtypeStruct((M_ROWS, h * HEAD_DIM), BF16),
        scratch_shapes=[pltpu.VMEM((TB, HEAD_DIM), BF16),
                        pltpu.VMEM((TB // RET_CHUNK, 2 * RET_CHUNK, HEAD_DIM), F32),
                        pltpu.VMEM((TB // RET_CHUNK, 2 * RET_CHUNK, HEAD_DIM), BF16)],
        compiler_params=_params(("arbitrary", "arbitrary")),
        name="retention",
    )(proj, proj, proj, proj, cos, sin, ld, gn_w.reshape(1, h * HEAD_DIM))


def _window_kernel(sink_ref, q_ref, k_ref, v_ref, cos_ref, sin_ref, o_ref, kr_ref, vx_ref):
    blk = WIN_BLOCK
    ctx_chunks = CTX_LEN // blk
    n_chunks = TB // blk
    kvh = pl.program_id(1)
    sc = HEAD_DIM ** -0.5
    half = HEAD_DIM // 4
    rows_q = WIN_GROUP * blk

    kr_ref[...] = _rotate(k_ref[...].astype(F32), cos_ref[...], sin_ref[...], half).astype(BF16)
    vx_ref[:, 0:HEAD_DIM] = v_ref[...]
    vx_ref[:, HEAD_DIM:2 * HEAD_DIM] = jnp.ones((TB, HEAD_DIM), BF16)

    row_iota = lax.broadcasted_iota(jnp.int32, (rows_q, 1), 0)
    head = row_iota // blk
    sink = jnp.zeros((rows_q, 1), F32)
    for g in range(WIN_GROUP):
        sink = jnp.where(head == g, sink_ref[kvh * WIN_GROUP + g], sink)

    def chunk(n, carry):
        rq = pl.ds(pl.multiple_of(n * blk, blk), blk)
        cos = cos_ref[rq, :]
        sin = sin_ref[rq, :]
        q = jnp.concatenate(
            [(_rotate(q_ref[rq, g * HEAD_DIM:(g + 1) * HEAD_DIM].astype(F32), cos, sin, half) * sc).astype(BF16)
             for g in range(WIN_GROUP)], axis=0)

        cstart = jnp.clip(n - 1, ctx_chunks, n_chunks - 3)
        rw = pl.ds(pl.multiple_of(cstart * blk, blk), 3 * blk)
        s_c = lax.dot_general(q, kr_ref[0:CTX_LEN, :], NT_DIMS, preferred_element_type=F32)
        s_w = lax.dot_general(q, kr_ref[rw, :], NT_DIMS, preferred_element_type=F32)
        qpos = (n - ctx_chunks) * blk + lax.broadcasted_iota(jnp.int32, (blk, 1), 0)
        kpos = (cstart - ctx_chunks) * blk + lax.broadcasted_iota(jnp.int32, (1, 3 * blk), 1)
        valid = (jnp.abs(qpos - kpos) <= WIN_RADIUS) & (n >= ctx_chunks)
        s_w = jnp.concatenate([jnp.where(valid, s_w[g * blk:(g + 1) * blk, :], NEG_INF)
                               for g in range(WIN_GROUP)], axis=0)

        m = jnp.maximum(jnp.maximum(jnp.max(s_c, axis=-1, keepdims=True), jnp.max(s_w, axis=-1, keepdims=True)),
                        sink)
        p_c = jnp.exp(s_c - m).astype(BF16)
        p_w = jnp.exp(s_w - m).astype(BF16)
        ox = (jnp.dot(p_c, vx_ref[0:CTX_LEN, :], preferred_element_type=F32)
              + jnp.dot(p_w, vx_ref[rw, :], preferred_element_type=F32))
        o = ox[:, 0:HEAD_DIM] / (ox[:, HEAD_DIM:2 * HEAD_DIM] + jnp.exp(sink - m))
        for g in range(WIN_GROUP):
            o_ref[rq, g * HEAD_DIM:(g + 1) * HEAD_DIM] = o[g * blk:(g + 1) * blk, :].astype(BF16)
        return carry

    lax.fori_loop(0, n_chunks, chunk, 0, unroll=2)


def _window_attention(proj, sink, cos, sin):
    gw = WIN_GROUP * HEAD_DIM
    q_cb0 = 4 * RET_HEADS * HEAD_DIM // gw
    k_cb0 = (4 * RET_HEADS + WIN_HEADS) * HEAD_DIM // HEAD_DIM
    v_cb0 = k_cb0 + WIN_KV_HEADS
    tab_spec = pl.BlockSpec((TB, LANES), lambda b, kh: (0, 0))
    return pl.pallas_call(
        _window_kernel,
        grid=(BATCH, WIN_KV_HEADS),
        in_specs=[
            pl.BlockSpec(memory_space=pltpu.SMEM),
            pl.BlockSpec((TB, gw), lambda b, kh: (b, q_cb0 + kh)),
            pl.BlockSpec((TB, HEAD_DIM), lambda b, kh: (b, k_cb0 + kh)),
            pl.BlockSpec((TB, HEAD_DIM), lambda b, kh: (b, v_cb0 + kh)),
            tab_spec, tab_spec,
        ],
        out_specs=pl.BlockSpec((TB, gw), lambda b, kh: (b, kh)),
        out_shape=jax.ShapeDtypeStruct((M_ROWS, WIN_HEADS * HEAD_DIM), BF16),
        scratch_shapes=[pltpu.VMEM((TB, HEAD_DIM), BF16), pltpu.VMEM((TB, 2 * HEAD_DIM), BF16)],
        compiler_params=_params(("arbitrary", "arbitrary")),
        name="window_attention",
    )(sink, proj, proj, proj, cos, sin)


def _conv_kernel(b_ref, c_ref, x_ref, w_ref, o_ref):
    u = c_ref[...].astype(F32) * x_ref[...].astype(F32)
    row = lax.broadcasted_iota(jnp.int32, (TB, 1), 0)
    prev = jnp.where((row == 0) | (row == CTX_LEN), 0.0, pltpu.roll(u, 1, 0))
    nxt = jnp.where((row == CTX_LEN - 1) | (row == TB - 1), 0.0, pltpu.roll(u, TB - 1, 0))
    z = prev * w_ref[0:1, :] + u * w_ref[1:2, :] + nxt * w_ref[2:3, :]
    o_ref[...] = (b_ref[...].astype(F32) * z).astype(BF16)


def _short_conv(proj, conv_w):
    tc = 256
    nblk = CONV_WIDTH // tc
    spec = lambda part: pl.BlockSpec((TB, tc), functools.partial(lambda b, j, part: (b, part * nblk + j), part=part))
    return pl.pallas_call(
        _conv_kernel,
        grid=(BATCH, nblk),
        in_specs=[spec(0), spec(1), spec(2), pl.BlockSpec((3, tc), lambda b, j: (0, j))],
        out_specs=pl.BlockSpec((TB, tc), lambda b, j: (b, j)),
        out_shape=jax.ShapeDtypeStruct((M_ROWS, CONV_WIDTH), BF16),
        compiler_params=_params(("arbitrary", "arbitrary")),
        name="short_conv",
    )(proj, proj, proj, conv_w)


MLA_HEADS_PER_STEP = 2
MLA_Q_TILE = 512
MLA_KEY_CHUNK = 1024


def _mla_kernel(q_ref, kv_ref, kr_ref, cos_ref, sin_ref, o_ref, kc_ref, vx_ref):
    sc = (MLA_NOPE + MLA_ROPE) ** -0.5
    half = MLA_ROPE // 4
    kv_w = MLA_NOPE + MLA_V

    kr = _rotate(kr_ref[...].astype(F32), cos_ref[...], sin_ref[...], half).astype(BF16)
    for hh in range(MLA_HEADS_PER_STEP):
        kc_ref[hh, :, 0:MLA_NOPE] = kv_ref[:, hh * kv_w:hh * kv_w + MLA_NOPE]
        kc_ref[hh, :, MLA_NOPE:MLA_QK_PAD] = kr
        vx_ref[hh, :, 0:MLA_V] = kv_ref[:, hh * kv_w + MLA_NOPE:(hh + 1) * kv_w]
        vx_ref[hh, :, MLA_V:2 * MLA_V] = jnp.ones((TB, MLA_V), BF16)

    def attend(row0, n_rows, key_chunks):
        rq = pl.ds(row0, n_rows)
        cos = cos_ref[rq, :]
        sin = sin_ref[rq, :]
        for hh in range(MLA_HEADS_PER_STEP):
            c0 = hh * MLA_QK_PAD
            qn = (q_ref[rq, c0:c0 + MLA_NOPE].astype(F32) * sc).astype(BF16)
            qr = (_rotate(q_ref[rq, c0 + MLA_NOPE:c0 + MLA_QK_PAD].astype(F32), cos, sin, half) * sc).astype(BF16)
            q = jnp.concatenate([qn, qr], axis=1)
            m = None
            acc = None
            for k0, kn in key_chunks:
                s = lax.dot_general(q, kc_ref[hh, k0:k0 + kn, :], NT_DIMS, preferred_element_type=F32)
                m_chunk = jnp.max(s, axis=-1, keepdims=True)
                m_new = m_chunk if m is None else jnp.maximum(m, m_chunk)
                p = jnp.exp(s - m_new).astype(BF16)
                pv = jnp.dot(p, vx_ref[hh, k0:k0 + kn, :], preferred_element_type=F32)
                acc = pv if acc is None else acc * jnp.exp(m - m_new) + pv
                m = m_new
            o_ref[rq, hh * MLA_V:(hh + 1) * MLA_V] = (acc[:, 0:MLA_V] / acc[:, MLA_V:2 * MLA_V]).astype(BF16)

    ctx_keys = [(0, CTX_LEN)]
    all_keys = ctx_keys + [(CTX_LEN + k * MLA_KEY_CHUNK, MLA_KEY_CHUNK) for k in range(SEQ // MLA_KEY_CHUNK)]
    attend(0, CTX_LEN, ctx_keys)

    def latent_tile(t, carry):
        attend(pl.multiple_of(CTX_LEN + t * MLA_Q_TILE, ROW_TILE), MLA_Q_TILE, all_keys)
        return carry

    lax.fori_loop(0, SEQ // MLA_Q_TILE, latent_tile, 0)


def _mla_attention(qcat, kv, proj, cos, sin):
    hps = MLA_HEADS_PER_STEP
    kr_cb = (3 * CONV_WIDTH + MLA_Q_RANK + MLA_KV_RANK) // LANES
    tab_spec = pl.BlockSpec((TB, LANES), lambda b, h: (0, 0))
    return pl.pallas_call(
        _mla_kernel,
        grid=(BATCH, MLA_HEADS // hps),
        in_specs=[
            pl.BlockSpec((TB, hps * MLA_QK_PAD), lambda b, h: (b, h)),
            pl.BlockSpec((TB, hps * (MLA_NOPE + MLA_V)), lambda b, h: (b, h)),
            pl.BlockSpec((TB, LANES), lambda b, h: (b, kr_cb)),
            tab_spec, tab_spec,
        ],
        out_specs=pl.BlockSpec((TB, hps * MLA_V), lambda b, h: (b, h)),
        out_shape=jax.ShapeDtypeStruct((M_ROWS, MLA_HEADS * MLA_V), BF16),
        scratch_shapes=[pltpu.VMEM((hps, TB, MLA_QK_PAD), BF16), pltpu.VMEM((hps, TB, 2 * MLA_V), BF16)],
        compiler_params=_params(("arbitrary", "arbitrary")),
        name="mla_attention",
    )(qcat, kv, proj, cos, sin)


def kernel(x, c, ctx, c_ctx, w_mod, b_mod, norm1_w, norm2_w, mlp_w1, mlp_w2, ev_w_in, ev_ret_log2_decay,
           ev_ret_gn_w, ev_sink, ev_w_out, od_w_in, od_conv_w, od_q_norm_w, od_kv_norm_w, od_w_uq, od_w_ukv,
           od_w_out, norm_f):
    assert x.shape == (BATCH, SEQ, D_MODEL) and ctx.shape == (BATCH, CTX_LEN, D_MODEL)
    rows = SEQ // GRID_W
    row = np.repeat(np.arange(rows), GRID_W)
    col = np.tile(np.arange(GRID_W), rows)
    pos = np.arange(SEQ)
    ret_cos, ret_sin = _rope_tables([pos], HEAD_DIM)
    win_cos, win_sin = _rope_tables([row, col], HEAD_DIM)
    mla_cos, mla_sin = _rope_tables([row, col], MLA_ROPE)

    cvec = jnp.concatenate([c, c_ctx[None], jnp.zeros((8 - BATCH - 1, D_MODEL), F32)], axis=0)
    mod_head = _modulation(cvec, w_mod, b_mod, 2 * D_MODEL)
    c_cols = cvec.T
    head_blocks = 2 * D_MODEL // MOD_COLS
    all_blocks = 6 * D_MODEL // MOD_COLS

    n_odd = od_w_uq.shape[0]
    w_uq = jnp.pad(od_w_uq.astype(BF16).reshape(n_odd, MLA_Q_RANK, MLA_HEADS, MLA_NOPE + MLA_ROPE),
                   ((0, 0), (0, 0), (0, 0), (0, MLA_QK_PAD - MLA_NOPE - MLA_ROPE))
                   ).reshape(n_odd, MLA_Q_RANK, MLA_HEADS * MLA_QK_PAD)

    w_in_odd = jnp.pad(od_w_in.astype(BF16), ((0, 0), (0, 0), (0, ODD_IN_PAD - ODD_IN)))

    for layer in range(DEPTH):
        j = layer // 2
        if layer == 0:
            h, n1 = _embed_norm(x, ctx, norm1_w[layer], mod_head.reshape(8, 2, D_MODEL))
        else:
            n1 = _norm_mod(h, norm1_w[layer], mod2, 0)
        if layer % 2 == 0:
            side_mod = (c_cols, w_mod, b_mod, 0, head_blocks, all_blocks - head_blocks) if layer == 0 else None
            proj = _matmul_ws([n1], ev_w_in, j, tn=EVEN_IN // 4, tm=MM_TM_BIG, side_mod=side_mod,
                              name="even_in_proj")
            if layer == 0:
                proj, mod_tail = proj
                mod2 = jnp.concatenate([mod_head, mod_tail], axis=1)
            mix_a = _retention(proj, ev_ret_log2_decay[j], ev_ret_gn_w[j], ret_cos, ret_sin)
            mix_b = _window_attention(proj, ev_sink[j], win_cos, win_sin)
            w_out = ev_w_out
        else:
            proj = _matmul_ws([n1], w_in_odd, j, tn=ODD_IN_PAD, tm=MM_TM_SMALL, name="odd_in_proj")
            mix_a = _short_conv(proj, od_conv_w[j])
            qcat = _matmul_ws([proj], w_uq, j, tn=MLA_HEADS * MLA_QK_PAD, tm=MM_TM_BIG, k_sizes=[MLA_Q_RANK],
                              a_col_blocks=[3 * CONV_WIDTH // MLA_Q_RANK], norm_w=od_q_norm_w[j], name="mla_q_up")
            kv = _matmul_ws([proj], od_w_ukv, j, tn=MLA_HEADS * (MLA_NOPE + MLA_V), tm=MM_TM_BIG,
                            k_sizes=[MLA_KV_RANK], a_col_blocks=[(3 * CONV_WIDTH + MLA_Q_RANK) // MLA_KV_RANK],
                            norm_w=od_kv_norm_w[j], name="mla_kv_up")
            mix_b = _mla_attention(qcat, kv, proj, mla_cos, mla_sin)
            w_out = od_w_out
        h, n2 = _matmul_ws([mix_a, mix_b], w_out, j, tn=D_MODEL, tm=MM_TM_SMALL, epilogue="gated_residual_norm",
                           res=h, mod=mod2, gate_idx=2, next_norm_w=norm2_w[layer], n_sub=2, name="mixer_out_proj")
        side_mod = (c_cols, w_mod, b_mod, layer + 1, 0, all_blocks) if layer + 1 < DEPTH else None
        ff, w2_bf16, *mod_next = _matmul_ws([n2], mlp_w1, layer, tn=1024, tm=MM_TM_BIG, epilogue="relu2",
                                            side_cast=(mlp_w2, layer), side_mod=side_mod, name="mlp_up")
        h = _matmul_ws([ff], w2_bf16[None], 0, tn=1024, tm=MM_TM_DOWN, epilogue="gated_residual",
                       res=h, mod=mod2, gate_idx=5, single_buffer_w=True, name="mlp_down")
        if mod_next:
            mod2 = mod_next[0]
    return _final_norm(h, norm_f)
```

```python
import functools

import numpy as np
import jax
import jax.numpy as jnp
from jax import lax
from jax.experimental import pallas as pl
from jax.experimental.pallas import tpu as pltpu

F32 = jnp.float32
BF16 = jnp.bfloat16

D_MODEL = 2048
BATCH = 2
SEQ = 4096
DEPTH = 4
GRID_W = 64
CTX_LEN = 256
HEAD_DIM = 128
ROPE_BASE = 10000.0
EPS = 1e-6
NEG_INF = -1e30

RET_HEADS = 8
RET_CHUNK = 128
RET_UNROLL_STATE = 8
RET_UNROLL_OUT = 17
WIN_HEADS = 8
WIN_KV_HEADS = 2
WIN_GROUP = WIN_HEADS // WIN_KV_HEADS
WIN_RADIUS = 128
WIN_BLOCK = 128
WIN_UNROLL = 4
CONV_WIDTH = 1024
MLA_HEADS = 8
MLA_Q_RANK = 512
MLA_KV_RANK = 256
MLA_NOPE = 128
MLA_ROPE = 64
MLA_V = 128
D_FF = 4 * D_MODEL

EVEN_IN = 4 * RET_HEADS * HEAD_DIM + (WIN_HEADS + 2 * WIN_KV_HEADS) * HEAD_DIM
ODD_IN = 3 * CONV_WIDTH + MLA_Q_RANK + MLA_KV_RANK + MLA_ROPE
ODD_IN_PAD = 3968
MLA_QK_PAD = 256

TB = CTX_LEN + SEQ
M_ROWS = BATCH * TB
LANES = 128
ROW_TILE = 256
MM_TM_BIG = M_ROWS // 8
MM_TM_SMALL = M_ROWS // 16
MOD_COLS = 256
MM_TM_DOWN = 512
VMEM_LIMIT = 60 * 1024 * 1024

NT_DIMS = (((1,), (1,)), ((), ()))
TN_DIMS = (((0,), (0,)), ((), ()))


def _params(sem, vmem=VMEM_LIMIT):
    return pltpu.CompilerParams(dimension_semantics=sem, vmem_limit_bytes=vmem)


def _rope_tables(pos_list, d_rot):
    per = d_rot // len(pos_list)
    inv = ROPE_BASE ** (-np.arange(0, per, 2, dtype=np.float64) / per)
    cos_parts, sin_parts = [], []
    for pos in pos_list:
        ang = pos.astype(np.float64)[:, None] * inv[None, :]
        c, s = np.cos(ang), np.sin(ang)
        cos_parts += [c, c]
        sin_parts += [-s, s]
    pad = ((0, 0), (0, LANES - d_rot))
    cos = np.pad(np.concatenate(cos_parts, axis=-1), pad)
    sin = np.pad(np.concatenate(sin_parts, axis=-1), pad)
    ctx_cos = np.pad(np.ones((CTX_LEN, d_rot)), pad)
    ctx_sin = np.zeros((CTX_LEN, LANES))
    return (jnp.asarray(np.concatenate([ctx_cos, cos], axis=0), F32),
            jnp.asarray(np.concatenate([ctx_sin, sin], axis=0), F32))


def _rotate(x, cos, sin, half):
    if 2 * half == LANES:
        partner = pltpu.roll(x, half, 1)
    else:
        lane = lax.broadcasted_iota(jnp.int32, (1, LANES), 1)
        first = (lane % (2 * half)) < half
        partner = jnp.where(first, pltpu.roll(x, LANES - half, 1), pltpu.roll(x, half, 1))
    return x * cos + partner * sin


def _modulation_kernel(c_ref, w_ref, b_ref, o_ref):
    s = jax.nn.silu(c_ref[...]).astype(BF16)
    o_ref[...] = jnp.dot(s, w_ref[...].astype(BF16), preferred_element_type=F32) + b_ref[...]


def _modulation(cvec, w_mod, b_mod, n_cols):
    tn = 1024
    return pl.pallas_call(
        _modulation_kernel,
        grid=(n_cols // tn,),
        in_specs=[
            pl.BlockSpec((8, D_MODEL), lambda j: (0, 0)),
            pl.BlockSpec((None, D_MODEL, tn), lambda j: (0, 0, j)),
            pl.BlockSpec((None, 1, tn), lambda j: (0, 0, j)),
        ],
        out_specs=pl.BlockSpec((8, tn), lambda j: (0, j)),
        out_shape=jax.ShapeDtypeStruct((8, n_cols), F32),
        compiler_params=_params(("arbitrary",)),
        name="modulation",
    )(cvec, w_mod, b_mod.reshape(DEPTH, 1, 6 * D_MODEL))


def _segment_of_tile(i):
    tiles = TB // ROW_TILE
    return jnp.where(i % tiles == 0, BATCH, i // tiles)


def _embed_norm_kernel(x_ref, ctx_ref, w_ref, mod_ref, h_ref, n_ref):
    is_ctx = pl.program_id(0) % (TB // ROW_TILE) == 0
    x = jnp.where(is_ctx, ctx_ref[...], x_ref[...])
    h_ref[...] = x
    y = x * lax.rsqrt(jnp.mean(x * x, axis=-1, keepdims=True) + EPS) * w_ref[...]
    n_ref[...] = (y * (1.0 + mod_ref[1:2, :]) + mod_ref[0:1, :]).astype(BF16)


def _embed_norm(x, ctx, norm_w, mod3):
    assert CTX_LEN == ROW_TILE
    tiles = TB // ROW_TILE
    row_spec = pl.BlockSpec((ROW_TILE, D_MODEL), lambda i: (i, 0))
    return pl.pallas_call(
        _embed_norm_kernel,
        grid=(M_ROWS // ROW_TILE,),
        in_specs=[
            pl.BlockSpec((None, ROW_TILE, D_MODEL), lambda i: (i // tiles, jnp.maximum(i % tiles - 1, 0), 0)),
            pl.BlockSpec((None, ROW_TILE, D_MODEL), lambda i: (i // tiles, 0, 0)),
            pl.BlockSpec((1, D_MODEL), lambda i: (0, 0)),
            pl.BlockSpec((None, mod3.shape[1], D_MODEL), lambda i: (_segment_of_tile(i), 0, 0)),
        ],
        out_specs=[row_spec, row_spec],
        out_shape=[jax.ShapeDtypeStruct((M_ROWS, D_MODEL), F32), jax.ShapeDtypeStruct((M_ROWS, D_MODEL), BF16)],
        compiler_params=_params(("arbitrary",)),
        name="embed_norm",
    )(x, ctx, norm_w.reshape(1, D_MODEL), mod3)


def _norm_mod_kernel(x_ref, w_ref, shift_ref, scale_ref, o_ref, *, tm):
    row0 = pl.program_id(0) * tm
    second = row0 >= TB
    within = row0 - jnp.where(second, TB, 0)
    one_segment = (within >= CTX_LEN) & (within + tm <= TB)
    x = x_ref[...]
    y = x * lax.rsqrt(jnp.mean(x * x, axis=-1, keepdims=True) + EPS)

    def finish(pick):
        o_ref[...] = (y * (w_ref[...] * (1.0 + pick(scale_ref))) + pick(shift_ref)).astype(BF16)

    @pl.when(one_segment)
    def _():
        seg = second.astype(jnp.int32)
        finish(lambda tab_ref: tab_ref[pl.ds(seg, 1), :])

    @pl.when(jnp.logical_not(one_segment))
    def _():
        finish(lambda tab_ref: _row_select(tab_ref, row0, tm))


def _norm_mod(h, norm_w, mod, shift_idx):
    tm = MM_TM_BIG
    return pl.pallas_call(
        functools.partial(_norm_mod_kernel, tm=tm),
        grid=(M_ROWS // tm,),
        in_specs=[
            pl.BlockSpec((tm, D_MODEL), lambda i: (i, 0)),
            pl.BlockSpec((1, D_MODEL), lambda i: (0, 0)),
            pl.BlockSpec((8, D_MODEL), lambda i: (0, shift_idx)),
            pl.BlockSpec((8, D_MODEL), lambda i: (0, shift_idx + 1)),
        ],
        out_specs=pl.BlockSpec((tm, D_MODEL), lambda i: (i, 0)),
        out_shape=jax.ShapeDtypeStruct((M_ROWS, D_MODEL), BF16),
        compiler_params=_params(("arbitrary",)),
        name="norm_mod",
    )(h, norm_w.reshape(1, D_MODEL), mod, mod)


def _final_norm_kernel(x_ref, w_ref, o_ref):
    x = x_ref[...]
    o_ref[...] = x * lax.rsqrt(jnp.mean(x * x, axis=-1, keepdims=True) + EPS) * w_ref[...]


def _final_norm(h, norm_w):
    tiles = TB // ROW_TILE
    ctx_tiles = CTX_LEN // ROW_TILE
    return pl.pallas_call(
        _final_norm_kernel,
        grid=(BATCH, SEQ // ROW_TILE),
        in_specs=[
            pl.BlockSpec((ROW_TILE, D_MODEL), lambda b, t: (b * tiles + ctx_tiles + t, 0)),
            pl.BlockSpec((1, D_MODEL), lambda b, t: (0, 0)),
        ],
        out_specs=pl.BlockSpec((None, ROW_TILE, D_MODEL), lambda b, t: (b, t, 0)),
        out_shape=jax.ShapeDtypeStruct((BATCH, SEQ, D_MODEL), F32),
        compiler_params=_params(("arbitrary", "arbitrary")),
        name="final_norm",
    )(h, norm_w.reshape(1, D_MODEL))


def _row_select(tab_ref, row0, tm):
    row = row0 + lax.broadcasted_iota(jnp.int32, (tm, 1), 0)
    second = row >= TB
    is_ctx = (row - jnp.where(second, TB, 0)) < CTX_LEN
    return jnp.where(is_ctx, tab_ref[2:3, :], jnp.where(second, tab_ref[1:2, :], tab_ref[0:1, :]))


def _mm_ws_kernel(*refs, k_sizes, epilogue, norm, tm, cast, n_sub, side_cast, side_mod_blocks):
    n_a = len(k_sizes)
    a_refs = refs[:n_a]
    w_ref = refs[n_a]
    pos = n_a + 1
    if norm:
        nw_ref = refs[pos]
        pos += 1
    if epilogue in ("gated_residual", "gated_residual_norm"):
        res_ref, gate_ref = refs[pos], refs[pos + 1]
        pos += 2
    if epilogue == "gated_residual_norm":
        n2w_ref, shift_ref, scale_ref = refs[pos:pos + 3]
        pos += 3
    if side_cast:
        side_in_ref = refs[pos]
        pos += 1
    if side_mod_blocks:
        ct_ref, wmod_ref, bmod_ref = refs[pos:pos + 3]
        pos += 3
    o_ref = refs[pos]
    pos += 1
    if epilogue == "gated_residual_norm":
        n_ref = refs[pos]
        pos += 1
    i = pl.program_id(1)
    if side_cast:
        side_out_ref = refs[pos]
        pos += 1
    if side_mod_blocks:
        mod_ref = refs[pos]
        pos += 1

    if cast:
        wb_ref = refs[pos]

        @pl.when(i == 0)
        def _():
            wb_ref[...] = w_ref[...].astype(BF16)
    else:
        wb_ref = w_ref

    ts = tm // n_sub
    for sub in range(n_sub):
        rs = slice(sub * ts, (sub + 1) * ts)
        row0 = i * tm + sub * ts
        acc = None
        off = 0
        for a_ref, ks in zip(a_refs, k_sizes):
            a = a_ref[rs, :]
            if norm:
                af = a.astype(F32)
                af = af * lax.rsqrt(jnp.mean(af * af, axis=-1, keepdims=True) + EPS) * nw_ref[...]
                a = af.astype(BF16)
            part = jnp.dot(a, wb_ref[off:off + ks, :], preferred_element_type=F32)
            acc = part if acc is None else acc + part
            off += ks

        if epilogue == "bf16":
            o_ref[rs, :] = acc.astype(BF16)
        elif epilogue == "relu2":
            r = jnp.maximum(acc, 0.0)
            o_ref[rs, :] = (r * r).astype(BF16)
        else:
            second = row0 >= TB
            within = row0 - jnp.where(second, TB, 0)
            one_segment = (within >= CTX_LEN) & (within + ts <= TB)

            def finish(pick, acc=acc, rs=rs):
                h = res_ref[rs, :] + pick(gate_ref) * acc
                o_ref[rs, :] = h
                if epilogue == "gated_residual_norm":
                    y = h * lax.rsqrt(jnp.mean(h * h, axis=-1, keepdims=True) + EPS)
                    n_ref[rs, :] = (y * (n2w_ref[...] * (1.0 + pick(scale_ref))) + pick(shift_ref)).astype(BF16)

            @pl.when(one_segment)
            def _(second=second, finish=finish):
                seg = second.astype(jnp.int32)
                finish(lambda tab_ref: tab_ref[pl.ds(seg, 1), :])

            @pl.when(jnp.logical_not(one_segment))
            def _(row0=row0, finish=finish):
                finish(lambda tab_ref: _row_select(tab_ref, row0, ts))

    if side_cast:
        side_out_ref[...] = side_in_ref[...].astype(BF16)
    if side_mod_blocks:
        s = jax.nn.silu(ct_ref[...])
        w = wmod_ref[...]
        row_id = lax.broadcasted_iota(jnp.int32, (8, w.shape[1]), 0)
        out = jnp.broadcast_to(bmod_ref[...], (8, w.shape[1]))
        for r in range(BATCH + 1):
            out = out + jnp.where(row_id == r, jnp.sum(w * s[:, r:r + 1], axis=0, keepdims=True), 0.0)
        mod_ref[...] = out


def _matmul_ws(a_list, w, layer, *, tn, tm, epilogue="bf16", a_col_blocks=None, k_sizes=None, norm_w=None,
               res=None, mod=None, gate_idx=0, next_norm_w=None, n_sub=1, side_cast=None, side_mod=None,
               single_buffer_w=False, name="matmul"):
    assert BATCH == 2
    _, kdim, n = w.shape
    if k_sizes is None:
        k_sizes = [a.shape[1] for a in a_list]
    if a_col_blocks is None:
        a_col_blocks = [0] * len(a_list)
    assert sum(k_sizes) == kdim and n % tn == 0 and M_ROWS % tm == 0
    row_tiles = M_ROWS // tm
    n_steps = (n // tn) * row_tiles
    step = lambda j, i: j * row_tiles + i
    gated = epilogue in ("gated_residual", "gated_residual_norm")
    in_specs = [pl.BlockSpec((tm, ks), functools.partial(lambda j, i, cb: (i, cb), cb=cb))
                for ks, cb in zip(k_sizes, a_col_blocks)]
    w_mode = dict(pipeline_mode=pl.Buffered(1)) if (n == tn or single_buffer_w) else {}
    in_specs.append(pl.BlockSpec((None, kdim, tn), lambda j, i: (layer, 0, j), **w_mode))
    operands = list(a_list) + [w]
    if norm_w is not None:
        assert len(a_list) == 1
        in_specs.append(pl.BlockSpec((1, kdim), lambda j, i: (0, 0)))
        operands.append(norm_w.reshape(1, kdim))
    out_specs = [pl.BlockSpec((tm, tn), lambda j, i: (i, j))]
    out_shape = [jax.ShapeDtypeStruct((M_ROWS, n), F32 if gated else BF16)]
    if gated:
        blocks_per_vec = D_MODEL // tn
        in_specs.append(pl.BlockSpec((tm, tn), lambda j, i: (i, j)))
        in_specs.append(pl.BlockSpec((8, tn), lambda j, i: (0, gate_idx * blocks_per_vec + j)))
        operands += [res, mod]
    if epilogue == "gated_residual_norm":
        assert tn == n == D_MODEL
        in_specs.append(pl.BlockSpec((1, D_MODEL), lambda j, i: (0, 0)))
        in_specs.append(pl.BlockSpec((8, D_MODEL), lambda j, i: (0, gate_idx + 1)))
        in_specs.append(pl.BlockSpec((8, D_MODEL), lambda j, i: (0, gate_idx + 2)))
        operands += [next_norm_w.reshape(1, D_MODEL), mod, mod]
        out_specs.append(pl.BlockSpec((tm, tn), lambda j, i: (i, j)))
        out_shape.append(jax.ShapeDtypeStruct((M_ROWS, n), BF16))
    if side_cast is not None:
        side_w, side_layer = side_cast
        _, side_k, side_n = side_w.shape
        slab = side_k // n_steps
        assert slab * n_steps == side_k
        in_specs.append(pl.BlockSpec((None, slab, side_n), lambda j, i: (side_layer, step(j, i), 0)))
        operands.append(side_w)
        out_specs.append(pl.BlockSpec((slab, side_n), lambda j, i: (step(j, i), 0)))
        out_shape.append(jax.ShapeDtypeStruct((side_k, side_n), BF16))
    side_mod_blocks = 0
    if side_mod is not None:
        ct, w_mod, b_mod, mod_layer, first_blk, side_mod_blocks = side_mod
        assert side_mod_blocks <= n_steps
        rel = lambda j, i: jnp.minimum(step(j, i), side_mod_blocks - 1)
        in_specs.append(pl.BlockSpec((D_MODEL, 8), lambda j, i: (0, 0)))
        in_specs.append(pl.BlockSpec((None, D_MODEL, MOD_COLS), lambda j, i: (mod_layer, 0, first_blk + rel(j, i))))
        in_specs.append(pl.BlockSpec((None, 1, MOD_COLS), lambda j, i: (mod_layer, 0, first_blk + rel(j, i))))
        operands += [ct, w_mod, b_mod.reshape(DEPTH, 1, 6 * D_MODEL)]
        out_specs.append(pl.BlockSpec((8, MOD_COLS), lambda j, i: (0, rel(j, i))))
        out_shape.append(jax.ShapeDtypeStruct((8, side_mod_blocks * MOD_COLS), F32))
    cast = w.dtype != BF16
    outs = pl.pallas_call(
        functools.partial(_mm_ws_kernel, k_sizes=tuple(k_sizes), epilogue=epilogue, norm=norm_w is not None,
                          tm=tm, cast=cast, n_sub=n_sub, side_cast=side_cast is not None,
                          side_mod_blocks=side_mod_blocks),
        grid=(n // tn, row_tiles),
        in_specs=in_specs,
        out_specs=out_specs,
        out_shape=out_shape,
        scratch_shapes=[pltpu.VMEM((kdim, tn), BF16)] if cast else [],
        compiler_params=_params(("arbitrary", "arbitrary")),
        name=name,
    )(*operands)
    return outs[0] if len(outs) == 1 else outs


def _retention_kernel(q_ref, k_ref, v_ref, g_ref, cos_ref, sin_ref, ld_ref, gn_ref, o_ref, kb_ref, kv_ref, st_ref):
    c = RET_CHUNK
    ctx_chunks = CTX_LEN // c
    lat_chunks = SEQ // c
    n_chunks = ctx_chunks + lat_chunks
    sc = HEAD_DIM ** -0.5
    lg = jnp.log1p(-jnp.exp2(ld_ref[...]))
    lgf = lg[0:1, 0:1]
    lgb = lg[1:2, 0:1]
    ii = lax.broadcasted_iota(jnp.int32, (c, 1), 0).astype(F32)
    jj = lax.broadcasted_iota(jnp.int32, (1, c), 1).astype(F32)
    diff = ii - jj
    dmat = jnp.where(diff >= 0, jnp.exp(lgf * jnp.maximum(diff, 0.0)), jnp.exp(lgb * jnp.maximum(-diff, 0.0)))
    qf_dec = jnp.exp(lgf * (ii + 1.0))
    qb_dec = jnp.exp(lgb * (c - ii))
    kf_dec = jnp.exp(lgf * (c - 1.0 - ii))
    kb_dec = jnp.exp(lgb * ii)
    gcf = jnp.exp(lgf * c)
    gcb = jnp.exp(lgb * c)

    def rows(n):
        return pl.ds(pl.multiple_of(n * c, c), c)

    def roped(ref, n):
        x = ref[rows(n), :].astype(F32)
        return _rotate(x, cos_ref[rows(n), :], sin_ref[rows(n), :], HEAD_DIM // 2)

    def increments(n, carry):
        k = roped(k_ref, n)
        kb_ref[rows(n), :] = k.astype(BF16)
        kd = jnp.concatenate([(k * kf_dec).astype(BF16), (k * kb_dec).astype(BF16)], axis=1)
        kv_ref[n] = lax.dot_general(kd, v_ref[rows(n), :], TN_DIMS, preferred_element_type=F32)
        return carry

    def recurrence(n0, cnt, s_f, s_b):
        def body(t, carry):
            s_f, s_b = carry
            nf = n0 + t
            nb = n0 + cnt - 1 - t
            st_ref[nf, 0:c, :] = s_f.astype(BF16)
            st_ref[nb, c:2 * c, :] = s_b.astype(BF16)
            return gcf * s_f + kv_ref[nf, 0:c, :], gcb * s_b + kv_ref[nb, c:2 * c, :]
        return lax.fori_loop(0, cnt, body, (s_f, s_b))

    def outputs(n, carry):
        q = roped(q_ref, n) * sc
        v = v_ref[rows(n), :]
        scores = lax.dot_general(q.astype(BF16), kb_ref[rows(n), :], NT_DIMS, preferred_element_type=F32) * dmat
        qd = jnp.concatenate([(q * qf_dec).astype(BF16), (q * qb_dec).astype(BF16)], axis=1)
        o = (jnp.dot(scores.astype(BF16), v, preferred_element_type=F32)
             + jnp.dot(qd, st_ref[n], preferred_element_type=F32))
        mu = jnp.mean(o, axis=-1, keepdims=True)
        oc = o - mu
        var = jnp.mean(oc * oc, axis=-1, keepdims=True)
        on = oc * lax.rsqrt(var + EPS) * gn_ref[...]
        g = g_ref[rows(n), :].astype(F32)
        o_ref[rows(n), :] = (jax.nn.silu(g) * on).astype(BF16)
        return carry

    lax.fori_loop(0, n_chunks, increments, 0, unroll=RET_UNROLL_STATE)
    zero = jnp.zeros((c, c), F32)
    s_f, s_b = recurrence(0, ctx_chunks, zero, zero)
    recurrence(ctx_chunks, lat_chunks, s_f, s_b)
    lax.fori_loop(0, n_chunks, outputs, 0, unroll=RET_UNROLL_OUT)


def _retention(proj, log2_decay, gn_w, cos, sin):
    h = RET_HEADS
    ld = jnp.broadcast_to(log2_decay.T[:, :, None], (h, 2, LANES))
    seq_spec = lambda cb0: pl.BlockSpec((TB, HEAD_DIM), functools.partial(lambda b, hh, cb0: (b, cb0 + hh), cb0=cb0))
    tab_spec = pl.BlockSpec((TB, LANES), lambda b, hh: (0, 0))
    return pl.pallas_call(
        _retention_kernel,
        grid=(BATCH, h),
        in_specs=[seq_spec(0), seq_spec(h), seq_spec(2 * h), seq_spec(3 * h), tab_spec, tab_spec,
                  pl.BlockSpec((None, 2, LANES), lambda b, hh: (hh, 0, 0)),
                  pl.BlockSpec((1, HEAD_DIM), lambda b, hh: (0, hh))],
        out_specs=pl.BlockSpec((TB, HEAD_DIM), lambda b, hh: (b, hh)),
        out_shape=jax.ShapeDtypeStruct((M_ROWS, h * HEAD_DIM), BF16),
        scratch_shapes=[pltpu.VMEM((TB, HEAD_DIM), BF16),
                        pltpu.VMEM((TB // RET_CHUNK, 2 * RET_CHUNK, HEAD_DIM), F32),
                        pltpu.VMEM((TB // RET_CHUNK, 2 * RET_CHUNK, HEAD_DIM), BF16)],
        compiler_params=_params(("arbitrary", "arbitrary")),
        name="retention",
    )(proj, proj, proj, proj, cos, sin, ld, gn_w.reshape(1, h * HEAD_DIM))


def _window_kernel(sink_ref, q_ref, k_ref, v_ref, cos_ref, sin_ref, o_ref, kr_ref, vx_ref):
    blk = WIN_BLOCK
    ctx_chunks = CTX_LEN // blk
    n_chunks = TB // blk
    kvh = pl.program_id(1)
    sc = HEAD_DIM ** -0.5
    half = HEAD_DIM // 4
    rows_q = WIN_GROUP * blk

    kr_ref[...] = _rotate(k_ref[...].astype(F32), cos_ref[...], sin_ref[...], half).astype(BF16)
    vx_ref[:, 0:HEAD_DIM] = v_ref[...]
    vx_ref[:, HEAD_DIM:2 * HEAD_DIM] = jnp.ones((TB, HEAD_DIM), BF16)

    row_iota = lax.broadcasted_iota(jnp.int32, (rows_q, 1), 0)
    head = row_iota // blk
    sink = jnp.zeros((rows_q, 1), F32)
    for g in range(WIN_GROUP):
        sink = jnp.where(head == g, sink_ref[kvh * WIN_GROUP + g], sink)

    def chunk(n, carry):
        rq = pl.ds(pl.multiple_of(n * blk, blk), blk)
        cos = cos_ref[rq, :]
        sin = sin_ref[rq, :]
        q = jnp.concatenate(
            [(_rotate(q_ref[rq, g * HEAD_DIM:(g + 1) * HEAD_DIM].astype(F32), cos, sin, half) * sc).astype(BF16)
             for g in range(WIN_GROUP)], axis=0)

        cstart = jnp.clip(n - 1, ctx_chunks, n_chunks - 3)
        rw = pl.ds(pl.multiple_of(cstart * blk, blk), 3 * blk)
        s_c = lax.dot_general(q, kr_ref[0:CTX_LEN, :], NT_DIMS, preferred_element_type=F32)
        s_w = lax.dot_general(q, kr_ref[rw, :], NT_DIMS, preferred_element_type=F32)
        qpos = (n - ctx_chunks) * blk + lax.broadcasted_iota(jnp.int32, (blk, 1), 0)
        kpos = (cstart - ctx_chunks) * blk + lax.broadcasted_iota(jnp.int32, (1, 3 * blk), 1)
        valid = (jnp.abs(qpos - kpos) <= WIN_RADIUS) & (n >= ctx_chunks)
        s_w = jnp.concatenate([jnp.where(valid, s_w[g * blk:(g + 1) * blk, :], NEG_INF)
                               for g in range(WIN_GROUP)], axis=0)

        m = jnp.maximum(jnp.maximum(jnp.max(s_c, axis=-1, keepdims=True), jnp.max(s_w, axis=-1, keepdims=True)),
                        sink)
        p_c = jnp.exp(s_c - m).astype(BF16)
        p_w = jnp.exp(s_w - m).astype(BF16)
        ox = (jnp.dot(p_c, vx_ref[0:CTX_LEN, :], preferred_element_type=F32)
              + jnp.dot(p_w, vx_ref[rw, :], preferred_element_type=F32))
        o = ox[:, 0:HEAD_DIM] / (ox[:, HEAD_DIM:2 * HEAD_DIM] + jnp.exp(sink - m))
        for g in range(WIN_GROUP):
            o_ref[rq, g * HEAD_DIM:(g + 1) * HEAD_DIM] = o[g * blk:(g + 1) * blk, :].astype(BF16)
        return carry

    lax.fori_loop(0, n_chunks, chunk, 0, unroll=WIN_UNROLL)


def _window_attention(proj, sink, cos, sin):
    gw = WIN_GROUP * HEAD_DIM
    q_cb0 = 4 * RET_HEADS * HEAD_DIM // gw
    k_cb0 = (4 * RET_HEADS + WIN_HEADS) * HEAD_DIM // HEAD_DIM
    v_cb0 = k_cb0 + WIN_KV_HEADS
    tab_spec = pl.BlockSpec((TB, LANES), lambda b, kh: (0, 0))
    return pl.pallas_call(
        _window_kernel,
        grid=(BATCH, WIN_KV_HEADS),
        in_specs=[
            pl.BlockSpec(memory_space=pltpu.SMEM),
            pl.BlockSpec((TB, gw), lambda b, kh: (b, q_cb0 + kh)),
            pl.BlockSpec((TB, HEAD_DIM), lambda b, kh: (b, k_cb0 + kh)),
            pl.BlockSpec((TB, HEAD_DIM), lambda b, kh: (b, v_cb0 + kh)),
            tab_spec, tab_spec,
        ],
        out_specs=pl.BlockSpec((TB, gw), lambda b, kh: (b, kh)),
        out_shape=jax.ShapeDtypeStruct((M_ROWS, WIN_HEADS * HEAD_DIM), BF16),
        scratch_shapes=[pltpu.VMEM((TB, HEAD_DIM), BF16), pltpu.VMEM((TB, 2 * HEAD_DIM), BF16)],
        compiler_params=_params(("arbitrary", "arbitrary")),
        name="window_attention",
    )(sink, proj, proj, proj, cos, sin)


def _conv_kernel(b_ref, c_ref, x_ref, w_ref, o_ref):
    u = c_ref[...].astype(F32) * x_ref[...].astype(F32)
    row = lax.broadcasted_iota(jnp.int32, (TB, 1), 0)
    prev = jnp.where((row == 0) | (row == CTX_LEN), 0.0, pltpu.roll(u, 1, 0))
    nxt = jnp.where((row == CTX_LEN - 1) | (row == TB - 1), 0.0, pltpu.roll(u, TB - 1, 0))
    z = prev * w_ref[0:1, :] + u * w_ref[1:2, :] + nxt * w_ref[2:3, :]
    o_ref[...] = (b_ref[...].astype(F32) * z).astype(BF16)


def _short_conv(proj, conv_w):
    tc = 256
    nblk = CONV_WIDTH // tc
    spec = lambda part: pl.BlockSpec((TB, tc), functools.partial(lambda b, j, part: (b, part * nblk + j), part=part))
    return pl.pallas_call(
        _conv_kernel,
        grid=(BATCH, nblk),
        in_specs=[spec(0), spec(1), spec(2), pl.BlockSpec((3, tc), lambda b, j: (0, j))],
        out_specs=pl.BlockSpec((TB, tc), lambda b, j: (b, j)),
        out_shape=jax.ShapeDtypeStruct((M_ROWS, CONV_WIDTH), BF16),
        compiler_params=_params(("arbitrary", "arbitrary")),
        name="short_conv",
    )(proj, proj, proj, conv_w)


MLA_HEADS_PER_STEP = 2
MLA_Q_TILE = 512
MLA_KEY_CHUNK = 1024


def _mla_kernel(q_ref, kv_ref, kr_ref, cos_ref, sin_ref, o_ref, kc_ref, vx_ref):
    sc = (MLA_NOPE + MLA_ROPE) ** -0.5
    half = MLA_ROPE // 4
    kv_w = MLA_NOPE + MLA_V

    kr = _rotate(kr_ref[...].astype(F32), cos_ref[...], sin_ref[...], half).astype(BF16)
    for hh in range(MLA_HEADS_PER_STEP):
        kc_ref[hh, :, 0:MLA_NOPE] = kv_ref[:, hh * kv_w:hh * kv_w + MLA_NOPE]
        kc_ref[hh, :, MLA_NOPE:MLA_QK_PAD] = kr
        vx_ref[hh, :, 0:MLA_V] = kv_ref[:, hh * kv_w + MLA_NOPE:(hh + 1) * kv_w]
        vx_ref[hh, :, MLA_V:2 * MLA_V] = jnp.ones((TB, MLA_V), BF16)

    def attend(row0, n_rows, key_chunks):
        rq = pl.ds(row0, n_rows)
        cos = cos_ref[rq, :]
        sin = sin_ref[rq, :]
        for hh in range(MLA_HEADS_PER_STEP):
            c0 = hh * MLA_QK_PAD
            qn = (q_ref[rq, c0:c0 + MLA_NOPE].astype(F32) * sc).astype(BF16)
            qr = (_rotate(q_ref[rq, c0 + MLA_NOPE:c0 + MLA_QK_PAD].astype(F32), cos, sin, half) * sc).astype(BF16)
            q = jnp.concatenate([qn, qr], axis=1)
            m = None
            acc = None
            for k0, kn in key_chunks:
                s = lax.dot_general(q, kc_ref[hh, k0:k0 + kn, :], NT_DIMS, preferred_element_type=F32)
                m_chunk = jnp.max(s, axis=-1, keepdims=True)
                m_new = m_chunk if m is None else jnp.maximum(m, m_chunk)
                p = jnp.exp(s - m_new).astype(BF16)
                pv = jnp.dot(p, vx_ref[hh, k0:k0 + kn, :], preferred_element_type=F32)
                acc = pv if acc is None else acc * jnp.exp(m - m_new) + pv
                m = m_new
            o_ref[rq, hh * MLA_V:(hh + 1) * MLA_V] = (acc[:, 0:MLA_V] / acc[:, MLA_V:2 * MLA_V]).astype(BF16)

    ctx_keys = [(0, CTX_LEN)]
    all_keys = ctx_keys + [(CTX_LEN + k * MLA_KEY_CHUNK, MLA_KEY_CHUNK) for k in range(SEQ // MLA_KEY_CHUNK)]
    attend(0, CTX_LEN, ctx_keys)

    def latent_tile(t, carry):
        attend(pl.multiple_of(CTX_LEN + t * MLA_Q_TILE, ROW_TILE), MLA_Q_TILE, all_keys)
        return carry

    lax.fori_loop(0, SEQ // MLA_Q_TILE, latent_tile, 0)


def _mla_attention(qcat, kv, proj, cos, sin):
    hps = MLA_HEADS_PER_STEP
    kr_cb = (3 * CONV_WIDTH + MLA_Q_RANK + MLA_KV_RANK) // LANES
    tab_spec = pl.BlockSpec((TB, LANES), lambda b, h: (0, 0))
    return pl.pallas_call(
        _mla_kernel,
        grid=(BATCH, MLA_HEADS // hps),
        in_specs=[
            pl.BlockSpec((TB, hps * MLA_QK_PAD), lambda b, h: (b, h)),
            pl.BlockSpec((TB, hps * (MLA_NOPE + MLA_V)), lambda b, h: (b, h)),
            pl.BlockSpec((TB, LANES), lambda b, h: (b, kr_cb)),
            tab_spec, tab_spec,
        ],
        out_specs=pl.BlockSpec((TB, hps * MLA_V), lambda b, h: (b, h)),
        out_shape=jax.ShapeDtypeStruct((M_ROWS, MLA_HEADS * MLA_V), BF16),
        scratch_shapes=[pltpu.VMEM((hps, TB, MLA_QK_PAD), BF16), pltpu.VMEM((hps, TB, 2 * MLA_V), BF16)],
        compiler_params=_params(("arbitrary", "arbitrary")),
        name="mla_attention",
    )(qcat, kv, proj, cos, sin)


def kernel(x, c, ctx, c_ctx, w_mod, b_mod, norm1_w, norm2_w, mlp_w1, mlp_w2, ev_w_in, ev_ret_log2_decay,
           ev_ret_gn_w, ev_sink, ev_w_out, od_w_in, od_conv_w, od_q_norm_w, od_kv_norm_w, od_w_uq, od_w_ukv,
           od_w_out, norm_f):
    assert x.shape == (BATCH, SEQ, D_MODEL) and ctx.shape == (BATCH, CTX_LEN, D_MODEL)
    rows = SEQ // GRID_W
    row = np.repeat(np.arange(rows), GRID_W)
    col = np.tile(np.arange(GRID_W), rows)
    pos = np.arange(SEQ)
    ret_cos, ret_sin = _rope_tables([pos], HEAD_DIM)
    win_cos, win_sin = _rope_tables([row, col], HEAD_DIM)
    mla_cos, mla_sin = _rope_tables([row, col], MLA_ROPE)

    cvec = jnp.concatenate([c, c_ctx[None], jnp.zeros((8 - BATCH - 1, D_MODEL), F32)], axis=0)
    mod_head = _modulation(cvec, w_mod, b_mod, 2 * D_MODEL)
    c_cols = cvec.T
    head_blocks = 2 * D_MODEL // MOD_COLS
    all_blocks = 6 * D_MODEL // MOD_COLS

    n_odd = od_w_uq.shape[0]
    w_uq = jnp.pad(od_w_uq.astype(BF16).reshape(n_odd, MLA_Q_RANK, MLA_HEADS, MLA_NOPE + MLA_ROPE),
                   ((0, 0), (0, 0), (0, 0), (0, MLA_QK_PAD - MLA_NOPE - MLA_ROPE))
                   ).reshape(n_odd, MLA_Q_RANK, MLA_HEADS * MLA_QK_PAD)

    w_in_odd = jnp.pad(od_w_in.astype(BF16), ((0, 0), (0, 0), (0, ODD_IN_PAD - ODD_IN)))

    for layer in range(DEPTH):
        j = layer // 2
        if layer == 0:
            h, n1 = _embed_norm(x, ctx, norm1_w[layer], mod_head.reshape(8, 2, D_MODEL))
        else:
            n1 = _norm_mod(h, norm1_w[layer], mod2, 0)
        if layer % 2 == 0:
            side_mod = (c_cols, w_mod, b_mod, 0, head_blocks, all_blocks - head_blocks) if layer == 0 else None
            proj = _matmul_ws([n1], ev_w_in, j, tn=EVEN_IN // 4, tm=MM_TM_BIG, side_mod=side_mod,
                              name="even_in_proj")
            if layer == 0:
                proj, mod_tail = proj
                mod2 = jnp.concatenate([mod_head, mod_tail], axis=1)
            mix_a = _retention(proj, ev_ret_log2_decay[j], ev_ret_gn_w[j], ret_cos, ret_sin)
            mix_b = _window_attention(proj, ev_sink[j], win_cos, win_sin)
            w_out = ev_w_out
        else:
            proj = _matmul_ws([n1], w_in_odd, j, tn=ODD_IN_PAD, tm=MM_TM_SMALL, name="odd_in_proj")
            mix_a = _short_conv(proj, od_conv_w[j])
            qcat = _matmul_ws([proj], w_uq, j, tn=MLA_HEADS * MLA_QK_PAD, tm=MM_TM_BIG, k_sizes=[MLA_Q_RANK],
                              a_col_blocks=[3 * CONV_WIDTH // MLA_Q_RANK], norm_w=od_q_norm_w[j], name="mla_q_up")
            kv = _matmul_ws([proj], od_w_ukv, j, tn=MLA_HEADS * (MLA_NOPE + MLA_V), tm=MM_TM_BIG,
                            k_sizes=[MLA_KV_RANK], a_col_blocks=[(3 * CONV_WIDTH + MLA_Q_RANK) // MLA_KV_RANK],
                            norm_w=od_kv_norm_w[j], name="mla_kv_up")
            mix_b = _mla_attention(qcat, kv, proj, mla_cos, mla_sin)
            w_out = od_w_out
        h, n2 = _matmul_ws([mix_a, mix_b], w_out, j, tn=D_MODEL, tm=MM_TM_SMALL, epilogue="gated_residual_norm",
                           res=h, mod=mod2, gate_idx=2, next_norm_w=norm2_w[layer], n_sub=2, name="mixer_out_proj")
        side_mod = (c_cols, w_mod, b_mod, layer + 1, 0, all_blocks) if layer + 1 < DEPTH else None
        ff, w2_bf16, *mod_next = _matmul_ws([n2], mlp_w1, layer, tn=1024, tm=MM_TM_BIG, epilogue="relu2",
                                            side_cast=(mlp_w2, layer), side_mod=side_mod, name="mlp_up")
        h = _matmul_ws([ff], w2_bf16[None], 0, tn=1024, tm=MM_TM_DOWN, epilogue="gated_residual",
                       res=h, mod=mod2, gate_idx=5, single_buffer_w=True, name="mlp_down")
        if mod_next:
            mod2 = mod_next[0]
    return _final_norm(h, norm_f)
```

```python
import functools

import numpy as np
import jax
import jax.numpy as jnp
from jax import lax
from jax.experimental import pallas as pl
from jax.experimental.pallas import tpu as pltpu

F32 = jnp.float32
BF16 = jnp.bfloat16

D_MODEL = 2048
BATCH = 2
SEQ = 4096
DEPTH = 4
GRID_W = 64
CTX_LEN = 256
HEAD_DIM = 128
ROPE_BASE = 10000.0
EPS = 1e-6
NEG_INF = -1e30

RET_HEADS = 8
RET_CHUNK = 128
RET_UNROLL_STATE = 8
RET_UNROLL_OUT = 17
WIN_HEADS = 8
WIN_KV_HEADS = 2
WIN_GROUP = WIN_HEADS // WIN_KV_HEADS
WIN_RADIUS = 128
WIN_BLOCK = 128
WIN_UNROLL = 4
CONV_WIDTH = 1024
MLA_HEADS = 8
MLA_Q_RANK = 512
MLA_KV_RANK = 256
MLA_NOPE = 128
MLA_ROPE = 64
MLA_V = 128
D_FF = 4 * D_MODEL

EVEN_IN = 4 * RET_HEADS * HEAD_DIM + (WIN_HEADS + 2 * WIN_KV_HEADS) * HEAD_DIM
ODD_IN = 3 * CONV_WIDTH + MLA_Q_RANK + MLA_KV_RANK + MLA_ROPE
ODD_IN_PAD = 3968
MLA_QK_PAD = 256

TB = CTX_LEN + SEQ
M_ROWS = BATCH * TB
LANES = 128
ROW_TILE = 256
MM_TM_BIG = M_ROWS // 8
MM_TM_SMALL = M_ROWS // 16
MOD_COLS = 256
MM_TM_DOWN = 512
VMEM_LIMIT = 60 * 1024 * 1024

NT_DIMS = (((1,), (1,)), ((), ()))
TN_DIMS = (((0,), (0,)), ((), ()))


def _params(sem, vmem=VMEM_LIMIT):
    return pltpu.CompilerParams(dimension_semantics=sem, vmem_limit_bytes=vmem)


def _rope_tables(pos_list, d_rot):
    per = d_rot // len(pos_list)
    inv = ROPE_BASE ** (-np.arange(0, per, 2, dtype=np.float64) / per)
    cos_parts, sin_parts = [], []
    for pos in pos_list:
        ang = pos.astype(np.float64)[:, None] * inv[None, :]
        c, s = np.cos(ang), np.sin(ang)
        cos_parts += [c, c]
        sin_parts += [-s, s]
    pad = ((0, 0), (0, LANES - d_rot))
    cos = np.pad(np.concatenate(cos_parts, axis=-1), pad)
    sin = np.pad(np.concatenate(sin_parts, axis=-1), pad)
    ctx_cos = np.pad(np.ones((CTX_LEN, d_rot)), pad)
    ctx_sin = np.zeros((CTX_LEN, LANES))
    return (jnp.asarray(np.concatenate([ctx_cos, cos], axis=0), F32),
            jnp.asarray(np.concatenate([ctx_sin, sin], axis=0), F32))


def _rotate(x, cos, sin, half):
    if 2 * half == LANES:
        partner = pltpu.roll(x, half, 1)
    else:
        lane = lax.broadcasted_iota(jnp.int32, (1, LANES), 1)
        first = (lane % (2 * half)) < half
        partner = jnp.where(first, pltpu.roll(x, LANES - half, 1), pltpu.roll(x, half, 1))
    return x * cos + partner * sin


def _modulation_kernel(c_ref, w_ref, b_ref, o_ref):
    s = jax.nn.silu(c_ref[...]).astype(BF16)
    o_ref[...] = jnp.dot(s, w_ref[...].astype(BF16), preferred_element_type=F32) + b_ref[...]


def _modulation(cvec, w_mod, b_mod, n_cols):
    tn = 1024
    return pl.pallas_call(
        _modulation_kernel,
        grid=(n_cols // tn,),
        in_specs=[
            pl.BlockSpec((8, D_MODEL), lambda j: (0, 0)),
            pl.BlockSpec((None, D_MODEL, tn), lambda j: (0, 0, j)),
            pl.BlockSpec((None, 1, tn), lambda j: (0, 0, j)),
        ],
        out_specs=pl.BlockSpec((8, tn), lambda j: (0, j)),
        out_shape=jax.ShapeDtypeStruct((8, n_cols), F32),
        compiler_params=_params(("arbitrary",)),
        name="modulation",
    )(cvec, w_mod, b_mod.reshape(DEPTH, 1, 6 * D_MODEL))


def _segment_of_tile(i):
    tiles = TB // ROW_TILE
    return jnp.where(i % tiles == 0, BATCH, i // tiles)


def _embed_norm_kernel(x_ref, ctx_ref, w_ref, mod_ref, h_ref, n_ref):
    is_ctx = pl.program_id(0) % (TB // ROW_TILE) == 0
    x = jnp.where(is_ctx, ctx_ref[...], x_ref[...])
    h_ref[...] = x
    y = x * lax.rsqrt(jnp.mean(x * x, axis=-1, keepdims=True) + EPS) * w_ref[...]
    n_ref[...] = (y * (1.0 + mod_ref[1:2, :]) + mod_ref[0:1, :]).astype(BF16)


def _embed_norm(x, ctx, norm_w, mod3):
    assert CTX_LEN == ROW_TILE
    tiles = TB // ROW_TILE
    row_spec = pl.BlockSpec((ROW_TILE, D_MODEL), lambda i: (i, 0))
    return pl.pallas_call(
        _embed_norm_kernel,
        grid=(M_ROWS // ROW_TILE,),
        in_specs=[
            pl.BlockSpec((None, ROW_TILE, D_MODEL), lambda i: (i // tiles, jnp.maximum(i % tiles - 1, 0), 0)),
            pl.BlockSpec((None, ROW_TILE, D_MODEL), lambda i: (i // tiles, 0, 0)),
            pl.BlockSpec((1, D_MODEL), lambda i: (0, 0)),
            pl.BlockSpec((None, mod3.shape[1], D_MODEL), lambda i: (_segment_of_tile(i), 0, 0)),
        ],
        out_specs=[row_spec, row_spec],
        out_shape=[jax.ShapeDtypeStruct((M_ROWS, D_MODEL), F32), jax.ShapeDtypeStruct((M_ROWS, D_MODEL), BF16)],
        compiler_params=_params(("arbitrary",)),
        name="embed_norm",
    )(x, ctx, norm_w.reshape(1, D_MODEL), mod3)


def _norm_mod_kernel(x_ref, w_ref, shift_ref, scale_ref, o_ref, *, tm):
    row0 = pl.program_id(0) * tm
    second = row0 >= TB
    within = row0 - jnp.where(second, TB, 0)
    one_segment = (within >= CTX_LEN) & (within + tm <= TB)
    x = x_ref[...]
    y = x * lax.rsqrt(jnp.mean(x * x, axis=-1, keepdims=True) + EPS)

    def finish(pick):
        o_ref[...] = (y * (w_ref[...] * (1.0 + pick(scale_ref))) + pick(shift_ref)).astype(BF16)

    @pl.when(one_segment)
    def _():
        seg = second.astype(jnp.int32)
        finish(lambda tab_ref: tab_ref[pl.ds(seg, 1), :])

    @pl.when(jnp.logical_not(one_segment))
    def _():
        finish(lambda tab_ref: _row_select(tab_ref, row0, tm))


def _norm_mod(h, norm_w, mod, shift_idx):
    tm = MM_TM_BIG
    return pl.pallas_call(
        functools.partial(_norm_mod_kernel, tm=tm),
        grid=(M_ROWS // tm,),
        in_specs=[
            pl.BlockSpec((tm, D_MODEL), lambda i: (i, 0)),
            pl.BlockSpec((1, D_MODEL), lambda i: (0, 0)),
            pl.BlockSpec((8, D_MODEL), lambda i: (0, shift_idx)),
            pl.BlockSpec((8, D_MODEL), lambda i: (0, shift_idx + 1)),
        ],
        out_specs=pl.BlockSpec((tm, D_MODEL), lambda i: (i, 0)),
        out_shape=jax.ShapeDtypeStruct((M_ROWS, D_MODEL), BF16),
        compiler_params=_params(("arbitrary",)),
        name="norm_mod",
    )(h, norm_w.reshape(1, D_MODEL), mod, mod)


def _final_norm_kernel(x_ref, w_ref, o_ref):
    x = x_ref[...]
    o_ref[...] = x * lax.rsqrt(jnp.mean(x * x, axis=-1, keepdims=True) + EPS) * w_ref[...]


def _final_norm(h, norm_w):
    tiles = TB // ROW_TILE
    ctx_tiles = CTX_LEN // ROW_TILE
    return pl.pallas_call(
        _final_norm_kernel,
        grid=(BATCH, SEQ // ROW_TILE),
        in_specs=[
            pl.BlockSpec((ROW_TILE, D_MODEL), lambda b, t: (b * tiles + ctx_tiles + t, 0)),
            pl.BlockSpec((1, D_MODEL), lambda b, t: (0, 0)),
        ],
        out_specs=pl.BlockSpec((None, ROW_TILE, D_MODEL), lambda b, t: (b, t, 0)),
        out_shape=jax.ShapeDtypeStruct((BATCH, SEQ, D_MODEL), F32),
        compiler_params=_params(("arbitrary", "arbitrary")),
        name="final_norm",
    )(h, norm_w.reshape(1, D_MODEL))


def _row_select(tab_ref, row0, tm):
    row = row0 + lax.broadcasted_iota(jnp.int32, (tm, 1), 0)
    second = row >= TB
    is_ctx = (row - jnp.where(second, TB, 0)) < CTX_LEN
    return jnp.where(is_ctx, tab_ref[2:3, :], jnp.where(second, tab_ref[1:2, :], tab_ref[0:1, :]))


def _mm_ws_kernel(*refs, k_sizes, epilogue, norm, tm, cast, n_sub, side_cast, side_mod_blocks):
    n_a = len(k_sizes)
    a_refs = refs[:n_a]
    w_ref = refs[n_a]
    pos = n_a + 1
    if norm:
        nw_ref = refs[pos]
        pos += 1
    if epilogue in ("gated_residual", "gated_residual_norm"):
        res_ref, gate_ref = refs[pos], refs[pos + 1]
        pos += 2
    if epilogue == "gated_residual_norm":
        n2w_ref, shift_ref, scale_ref = refs[pos:pos + 3]
        pos += 3
    if side_cast:
        side_in_ref = refs[pos]
        pos += 1
    if side_mod_blocks:
        ct_ref, wmod_ref, bmod_ref = refs[pos:pos + 3]
        pos += 3
    o_ref = refs[pos]
    pos += 1
    if epilogue == "gated_residual_norm":
        n_ref = refs[pos]
        pos += 1
    i = pl.program_id(1)
    if side_cast:
        side_out_ref = refs[pos]
        pos += 1
    if side_mod_blocks:
        mod_ref = refs[pos]
        pos += 1

    if cast:
        wb_ref = refs[pos]

        @pl.when(i == 0)
        def _():
            wb_ref[...] = w_ref[...].astype(BF16)
    else:
        wb_ref = w_ref

    ts = tm // n_sub
    for sub in range(n_sub):
        rs = slice(sub * ts, (sub + 1) * ts)
        row0 = i * tm + sub * ts
        acc = None
        off = 0
        for a_ref, ks in zip(a_refs, k_sizes):
            a = a_ref[rs, :]
            if norm:
                af = a.astype(F32)
                af = af * lax.rsqrt(jnp.mean(af * af, axis=-1, keepdims=True) + EPS) * nw_ref[...]
                a = af.astype(BF16)
            part = jnp.dot(a, wb_ref[off:off + ks, :], preferred_element_type=F32)
            acc = part if acc is None else acc + part
            off += ks

        if epilogue == "bf16":
            o_ref[rs, :] = acc.astype(BF16)
        elif epilogue == "relu2":
            r = jnp.maximum(acc, 0.0)
            o_ref[rs, :] = (r * r).astype(BF16)
        else:
            second = row0 >= TB
            within = row0 - jnp.where(second, TB, 0)
            one_segment = (within >= CTX_LEN) & (within + ts <= TB)

            def finish(pick, acc=acc, rs=rs):
                h = res_ref[rs, :] + pick(gate_ref) * acc
                o_ref[rs, :] = h
                if epilogue == "gated_residual_norm":
                    y = h * lax.rsqrt(jnp.mean(h * h, axis=-1, keepdims=True) + EPS)
                    n_ref[rs, :] = (y * (n2w_ref[...] * (1.0 + pick(scale_ref))) + pick(shift_ref)).astype(BF16)

            @pl.when(one_segment)
            def _(second=second, finish=finish):
                seg = second.astype(jnp.int32)
                finish(lambda tab_ref: tab_ref[pl.ds(seg, 1), :])

            @pl.when(jnp.logical_not(one_segment))
            def _(row0=row0, finish=finish):
                finish(lambda tab_ref: _row_select(tab_ref, row0, ts))

    if side_cast:
        side_out_ref[...] = side_in_ref[...].astype(BF16)
    if side_mod_blocks:
        s = jax.nn.silu(ct_ref[...])
        w = wmod_ref[...]
        row_id = lax.broadcasted_iota(jnp.int32, (8, w.shape[1]), 0)
        out = jnp.broadcast_to(bmod_ref[...], (8, w.shape[1]))
        for r in range(BATCH + 1):
            out = out + jnp.where(row_id == r, jnp.sum(w * s[:, r:r + 1], axis=0, keepdims=True), 0.0)
        mod_ref[...] = out


def _matmul_ws(a_list, w, layer, *, tn, tm, epilogue="bf16", a_col_blocks=None, k_sizes=None, norm_w=None,
               res=None, mod=None, gate_idx=0, next_norm_w=None, n_sub=1, side_cast=None, side_mod=None,
               single_buffer_w=False, name="matmul"):
    assert BATCH == 2
    _, kdim, n = w.shape
    if k_sizes is None:
        k_sizes = [a.shape[1] for a in a_list]
    if a_col_blocks is None:
        a_col_blocks = [0] * len(a_list)
    assert sum(k_sizes) == kdim and n % tn == 0 and M_ROWS % tm == 0
    row_tiles = M_ROWS // tm
    n_steps = (n // tn) * row_tiles
    step = lambda j, i: j * row_tiles + i
    gated = epilogue in ("gated_residual", "gated_residual_norm")
    in_specs = [pl.BlockSpec((tm, ks), functools.partial(lambda j, i, cb: (i, cb), cb=cb))
                for ks, cb in zip(k_sizes, a_col_blocks)]
    w_mode = dict(pipeline_mode=pl.Buffered(1)) if (n == tn or single_buffer_w) else {}
    in_specs.append(pl.BlockSpec((None, kdim, tn), lambda j, i: (layer, 0, j), **w_mode))
    operands = list(a_list) + [w]
    if norm_w is not None:
        assert len(a_list) == 1
        in_specs.append(pl.BlockSpec((1, kdim), lambda j, i: (0, 0)))
        operands.append(norm_w.reshape(1, kdim))
    out_specs = [pl.BlockSpec((tm, tn), lambda j, i: (i, j))]
    out_shape = [jax.ShapeDtypeStruct((M_ROWS, n), F32 if gated else BF16)]
    if gated:
        blocks_per_vec = D_MODEL // tn
        in_specs.append(pl.BlockSpec((tm, tn), lambda j, i: (i, j)))
        in_specs.append(pl.BlockSpec((8, tn), lambda j, i: (0, gate_idx * blocks_per_vec + j)))
        operands += [res, mod]
    if epilogue == "gated_residual_norm":
        assert tn == n == D_MODEL
        in_specs.append(pl.BlockSpec((1, D_MODEL), lambda j, i: (0, 0)))
        in_specs.append(pl.BlockSpec((8, D_MODEL), lambda j, i: (0, gate_idx + 1)))
        in_specs.append(pl.BlockSpec((8, D_MODEL), lambda j, i: (0, gate_idx + 2)))
        operands += [next_norm_w.reshape(1, D_MODEL), mod, mod]
        out_specs.append(pl.BlockSpec((tm, tn), lambda j, i: (i, j)))
        out_shape.append(jax.ShapeDtypeStruct((M_ROWS, n), BF16))
    if side_cast is not None:
        side_w, side_layer = side_cast
        _, side_k, side_n = side_w.shape
        slab = side_k // n_steps
        assert slab * n_steps == side_k
        in_specs.append(pl.BlockSpec((None, slab, side_n), lambda j, i: (side_layer, step(j, i), 0)))
        operands.append(side_w)
        out_specs.append(pl.BlockSpec((slab, side_n), lambda j, i: (step(j, i), 0)))
        out_shape.append(jax.ShapeDtypeStruct((side_k, side_n), BF16))
    side_mod_blocks = 0
    if side_mod is not None:
        ct, w_mod, b_mod, mod_layer, first_blk, side_mod_blocks = side_mod
        assert side_mod_blocks <= n_steps
        rel = lambda j, i: jnp.minimum(step(j, i), side_mod_blocks - 1)
        in_specs.append(pl.BlockSpec((D_MODEL, 8), lambda j, i: (0, 0)))
        in_specs.append(pl.BlockSpec((None, D_MODEL, MOD_COLS), lambda j, i: (mod_layer, 0, first_blk + rel(j, i))))
        in_specs.append(pl.BlockSpec((None, 1, MOD_COLS), lambda j, i: (mod_layer, 0, first_blk + rel(j, i))))
        operands += [ct, w_mod, b_mod.reshape(DEPTH, 1, 6 * D_MODEL)]
        out_specs.append(pl.BlockSpec((8, MOD_COLS), lambda j, i: (0, rel(j, i))))
        out_shape.append(jax.ShapeDtypeStruct((8, side_mod_blocks * MOD_COLS), F32))
    cast = w.dtype != BF16
    outs = pl.pallas_call(
        functools.partial(_mm_ws_kernel, k_sizes=tuple(k_sizes), epilogue=epilogue, norm=norm_w is not None,
                          tm=tm, cast=cast, n_sub=n_sub, side_cast=side_cast is not None,
                          side_mod_blocks=side_mod_blocks),
        grid=(n // tn, row_tiles),
        in_specs=in_specs,
        out_specs=out_specs,
        out_shape=out_shape,
        scratch_shapes=[pltpu.VMEM((kdim, tn), BF16)] if cast else [],
        compiler_params=_params(("arbitrary", "arbitrary")),
        name=name,
    )(*operands)
    return outs[0] if len(outs) == 1 else outs


def _retention_kernel(q_ref, k_ref, v_ref, g_ref, cos_ref, sin_ref, ld_ref, gn_ref, o_ref, kb_ref, kv_ref, st_ref):
    c = RET_CHUNK
    ctx_chunks = CTX_LEN // c
    lat_chunks = SEQ // c
    n_chunks = ctx_chunks + lat_chunks
    sc = HEAD_DIM ** -0.5
    lg = jnp.log1p(-jnp.exp2(ld_ref[...]))
    lgf = lg[0:1, 0:1]
    lgb = lg[1:2, 0:1]
    ii = lax.broadcasted_iota(jnp.int32, (c, 1), 0).astype(F32)
    jj = lax.broadcasted_iota(jnp.int32, (1, c), 1).astype(F32)
    diff = ii - jj
    dmat = jnp.where(diff >= 0, jnp.exp(lgf * jnp.maximum(diff, 0.0)), jnp.exp(lgb * jnp.maximum(-diff, 0.0)))
    qf_dec = jnp.exp(lgf * (ii + 1.0))
    qb_dec = jnp.exp(lgb * (c - ii))
    kf_dec = jnp.exp(lgf * (c - 1.0 - ii))
    kb_dec = jnp.exp(lgb * ii)
    gcf = jnp.exp(lgf * c)
    gcb = jnp.exp(lgb * c)

    def rows(n):
        return pl.ds(pl.multiple_of(n * c, c), c)

    def roped(ref, n):
        x = ref[rows(n), :].astype(F32)
        return _rotate(x, cos_ref[rows(n), :], sin_ref[rows(n), :], HEAD_DIM // 2)

    def increments(n, carry):
        k = roped(k_ref, n)
        kb_ref[rows(n), :] = k.astype(BF16)
        kd = jnp.concatenate([(k * kf_dec).astype(BF16), (k * kb_dec).astype(BF16)], axis=1)
        kv_ref[n] = lax.dot_general(kd, v_ref[rows(n), :], TN_DIMS, preferred_element_type=F32)
        return carry

    def recurrence(n0, cnt, s_f, s_b):
        def body(t, carry):
            s_f, s_b = carry
            nf = n0 + t
            nb = n0 + cnt - 1 - t
            st_ref[nf, 0:c, :] = s_f.astype(BF16)
            st_ref[nb, c:2 * c, :] = s_b.astype(BF16)
            return gcf * s_f + kv_ref[nf, 0:c, :], gcb * s_b + kv_ref[nb, c:2 * c, :]
        return lax.fori_loop(0, cnt, body, (s_f, s_b))

    def outputs(n, carry):
        q = roped(q_ref, n) * sc
        v = v_ref[rows(n), :]
        scores = lax.dot_general(q.astype(BF16), kb_ref[rows(n), :], NT_DIMS, preferred_element_type=F32) * dmat
        qd = jnp.concatenate([(q * qf_dec).astype(BF16), (q * qb_dec).astype(BF16)], axis=1)
        o = (jnp.dot(scores.astype(BF16), v, preferred_element_type=F32)
             + jnp.dot(qd, st_ref[n], preferred_element_type=F32))
        mu = jnp.mean(o, axis=-1, keepdims=True)
        oc = o - mu
        var = jnp.mean(oc * oc, axis=-1, keepdims=True)
        on = oc * lax.rsqrt(var + EPS) * gn_ref[...]
        g = g_ref[rows(n), :].astype(F32)
        o_ref[rows(n), :] = (jax.nn.silu(g) * on).astype(BF16)
        return carry

    lax.fori_loop(0, n_chunks, increments, 0, unroll=RET_UNROLL_STATE)
    zero = jnp.zeros((c, c), F32)
    s_f, s_b = recurrence(0, ctx_chunks, zero, zero)
    recurrence(ctx_chunks, lat_chunks, s_f, s_b)
    lax.fori_loop(0, n_chunks, outputs, 0, unroll=RET_UNROLL_OUT)


def _retention(proj, log2_decay, gn_w, cos, sin):
    h = RET_HEADS
    ld = jnp.broadcast_to(log2_decay.T[:, :, None], (h, 2, LANES))
    seq_spec = lambda cb0: pl.BlockSpec((TB, HEAD_DIM), functools.partial(lambda b, hh, cb0: (b, cb0 + hh), cb0=cb0))
    tab_spec = pl.BlockSpec((TB, LANES), lambda b, hh: (0, 0))
    return pl.pallas_call(
        _retention_kernel,
        grid=(BATCH, h),
        in_specs=[seq_spec(0), seq_spec(h), seq_spec(2 * h), seq_spec(3 * h), tab_spec, tab_spec,
                  pl.BlockSpec((None, 2, LANES), lambda b, hh: (hh, 0, 0)),
                  pl.BlockSpec((1, HEAD_DIM), lambda b, hh: (0, hh))],
        out_specs=pl.BlockSpec((TB, HEAD_DIM), lambda b, hh: (b, hh)),
        out_shape=jax.ShapeDtypeStruct((M_ROWS, h * HEAD_DIM), BF16),
        scratch_shapes=[pltpu.VMEM((TB, HEAD_DIM), BF16),
                        pltpu.VMEM((TB // RET_CHUNK, 2 * RET_CHUNK, HEAD_DIM), F32),
                        pltpu.VMEM((TB // RET_CHUNK, 2 * RET_CHUNK, HEAD_DIM), BF16)],
        compiler_params=_params(("arbitrary", "arbitrary")),
        name="retention",
    )(proj, proj, proj, proj, cos, sin, ld, gn_w.reshape(1, h * HEAD_DIM))


def _window_kernel(sink_ref, q_ref, k_ref, v_ref, cos_ref, sin_ref, o_ref, kr_ref, vx_ref):
    blk = WIN_BLOCK
    ctx_chunks = CTX_LEN // blk
    n_chunks = TB // blk
    kvh = pl.program_id(1)
    sc = HEAD_DIM ** -0.5
    half = HEAD_DIM // 4
    rows_q = WIN_GROUP * blk

    kr_ref[...] = _rotate(k_ref[...].astype(F32), cos_ref[...], sin_ref[...], half).astype(BF16)
    vx_ref[:, 0:HEAD_DIM] = v_ref[...]
    vx_ref[:, HEAD_DIM:2 * HEAD_DIM] = jnp.ones((TB, HEAD_DIM), BF16)

    row_iota = lax.broadcasted_iota(jnp.int32, (rows_q, 1), 0)
    head = row_iota // blk
    sink = jnp.zeros((rows_q, 1), F32)
    for g in range(WIN_GROUP):
        sink = jnp.where(head == g, sink_ref[kvh * WIN_GROUP + g], sink)

    def chunk(n, carry):
        rq = pl.ds(pl.multiple_of(n * blk, blk), blk)
        cos = cos_ref[rq, :]
        sin = sin_ref[rq, :]
        q = jnp.concatenate(
            [(_rotate(q_ref[rq, g * HEAD_DIM:(g + 1) * HEAD_DIM].astype(F32), cos, sin, half) * sc).astype(BF16)
             for g in range(WIN_GROUP)], axis=0)

        cstart = jnp.clip(n - 1, ctx_chunks, n_chunks - 3)
        rw = pl.ds(pl.multiple_of(cstart * blk, blk), 3 * blk)
        s_c = lax.dot_general(q, kr_ref[0:CTX_LEN, :], NT_DIMS, preferred_element_type=F32)
        s_w = lax.dot_general(q, kr_ref[rw, :], NT_DIMS, preferred_element_type=F32)
        qpos = (n - ctx_chunks) * blk + lax.broadcasted_iota(jnp.int32, (blk, 1), 0)
        kpos = (cstart - ctx_chunks) * blk + lax.broadcasted_iota(jnp.int32, (1, 3 * blk), 1)
        valid = (jnp.abs(qpos - kpos) <= WIN_RADIUS) & (n >= ctx_chunks)
        s_w = jnp.concatenate([jnp.where(valid, s_w[g * blk:(g + 1) * blk, :], NEG_INF)
                               for g in range(WIN_GROUP)], axis=0)

        m = jnp.maximum(jnp.maximum(jnp.max(s_c, axis=-1, keepdims=True), jnp.max(s_w, axis=-1, keepdims=True)),
                        sink)
        p_c = jnp.exp(s_c - m).astype(BF16)
        p_w = jnp.exp(s_w - m).astype(BF16)
        ox = (jnp.dot(p_c, vx_ref[0:CTX_LEN, :], preferred_element_type=F32)
              + jnp.dot(p_w, vx_ref[rw, :], preferred_element_type=F32))
        o = ox[:, 0:HEAD_DIM] / (ox[:, HEAD_DIM:2 * HEAD_DIM] + jnp.exp(sink - m))
        for g in range(WIN_GROUP):
            o_ref[rq, g * HEAD_DIM:(g + 1) * HEAD_DIM] = o[g * blk:(g + 1) * blk, :].astype(BF16)
        return carry

    lax.fori_loop(0, n_chunks, chunk, 0, unroll=WIN_UNROLL)


def _window_attention(proj, sink, cos, sin):
    gw = WIN_GROUP * HEAD_DIM
    q_cb0 = 4 * RET_HEADS * HEAD_DIM // gw
    k_cb0 = (4 * RET_HEADS + WIN_HEADS) * HEAD_DIM // HEAD_DIM
    v_cb0 = k_cb0 + WIN_KV_HEADS
    tab_spec = pl.BlockSpec((TB, LANES), lambda b, kh: (0, 0))
    return pl.pallas_call(
        _window_kernel,
        grid=(BATCH, WIN_KV_HEADS),
        in_specs=[
            pl.BlockSpec(memory_space=pltpu.SMEM),
            pl.BlockSpec((TB, gw), lambda b, kh: (b, q_cb0 + kh)),
            pl.BlockSpec((TB, HEAD_DIM), lambda b, kh: (b, k_cb0 + kh)),
            pl.BlockSpec((TB, HEAD_DIM), lambda b, kh: (b, v_cb0 + kh)),
            tab_spec, tab_spec,
        ],
        out_specs=pl.BlockSpec((TB, gw), lambda b, kh: (b, kh)),
        out_shape=jax.ShapeDtypeStruct((M_ROWS, WIN_HEADS * HEAD_DIM), BF16),
        scratch_shapes=[pltpu.VMEM((TB, HEAD_DIM), BF16), pltpu.VMEM((TB, 2 * HEAD_DIM), BF16)],
        compiler_params=_params(("arbitrary", "arbitrary")),
        name="window_attention",
    )(sink, proj, proj, proj, cos, sin)


def _conv_kernel(b_ref, c_ref, x_ref, w_ref, o_ref):
    u = c_ref[...].astype(F32) * x_ref[...].astype(F32)
    row = lax.broadcasted_iota(jnp.int32, (TB, 1), 0)
    prev = jnp.where((row == 0) | (row == CTX_LEN), 0.0, pltpu.roll(u, 1, 0))
    nxt = jnp.where((row == CTX_LEN - 1) | (row == TB - 1), 0.0, pltpu.roll(u, TB - 1, 0))
    z = prev * w_ref[0:1, :] + u * w_ref[1:2, :] + nxt * w_ref[2:3, :]
    o_ref[...] = (b_ref[...].astype(F32) * z).astype(BF16)


def _short_conv(proj, conv_w):
    tc = 256
    nblk = CONV_WIDTH // tc
    spec = lambda part: pl.BlockSpec((TB, tc), functools.partial(lambda b, j, part: (b, part * nblk + j), part=part))
    return pl.pallas_call(
        _conv_kernel,
        grid=(BATCH, nblk),
        in_specs=[spec(0), spec(1), spec(2), pl.BlockSpec((3, tc), lambda b, j: (0, j))],
        out_specs=pl.BlockSpec((TB, tc), lambda b, j: (b, j)),
        out_shape=jax.ShapeDtypeStruct((M_ROWS, CONV_WIDTH), BF16),
        compiler_params=_params(("arbitrary", "arbitrary")),
        name="short_conv",
    )(proj, proj, proj, conv_w)


MLA_HEADS_PER_STEP = 2
MLA_Q_TILE = 1024
MLA_KEY_CHUNK = 1024


def _mla_kernel(q_ref, kv_ref, kr_ref, cos_ref, sin_ref, o_ref, kc_ref, vx_ref):
    sc = (MLA_NOPE + MLA_ROPE) ** -0.5
    half = MLA_ROPE // 4
    kv_w = MLA_NOPE + MLA_V

    kr = _rotate(kr_ref[...].astype(F32), cos_ref[...], sin_ref[...], half).astype(BF16)
    for hh in range(MLA_HEADS_PER_STEP):
        kc_ref[hh, :, 0:MLA_NOPE] = kv_ref[:, hh * kv_w:hh * kv_w + MLA_NOPE]
        kc_ref[hh, :, MLA_NOPE:MLA_QK_PAD] = kr
        vx_ref[hh, :, 0:MLA_V] = kv_ref[:, hh * kv_w + MLA_NOPE:(hh + 1) * kv_w]
        vx_ref[hh, :, MLA_V:2 * MLA_V] = jnp.ones((TB, MLA_V), BF16)

    def attend(row0, n_rows, key_chunks):
        rq = pl.ds(row0, n_rows)
        cos = cos_ref[rq, :]
        sin = sin_ref[rq, :]
        for hh in range(MLA_HEADS_PER_STEP):
            c0 = hh * MLA_QK_PAD
            qn = (q_ref[rq, c0:c0 + MLA_NOPE].astype(F32) * sc).astype(BF16)
            qr = (_rotate(q_ref[rq, c0 + MLA_NOPE:c0 + MLA_QK_PAD].astype(F32), cos, sin, half) * sc).astype(BF16)
            q = jnp.concatenate([qn, qr], axis=1)
            m = None
            acc = None
            for k0, kn in key_chunks:
                s = lax.dot_general(q, kc_ref[hh, k0:k0 + kn, :], NT_DIMS, preferred_element_type=F32)
                m_chunk = jnp.max(s, axis=-1, keepdims=True)
                m_new = m_chunk if m is None else jnp.maximum(m, m_chunk)
                p = jnp.exp(s - m_new).astype(BF16)
                pv = jnp.dot(p, vx_ref[hh, k0:k0 + kn, :], preferred_element_type=F32)
                acc = pv if acc is None else acc * jnp.exp(m - m_new) + pv
                m = m_new
            o_ref[rq, hh * MLA_V:(hh + 1) * MLA_V] = (acc[:, 0:MLA_V] / acc[:, MLA_V:2 * MLA_V]).astype(BF16)

    ctx_keys = [(0, CTX_LEN)]
    all_keys = ctx_keys + [(CTX_LEN + k * MLA_KEY_CHUNK, MLA_KEY_CHUNK) for k in range(SEQ // MLA_KEY_CHUNK)]
    attend(0, CTX_LEN, ctx_keys)

    def latent_tile(t, carry):
        attend(pl.multiple_of(CTX_LEN + t * MLA_Q_TILE, ROW_TILE), MLA_Q_TILE, all_keys)
        return carry

    lax.fori_loop(0, SEQ // MLA_Q_TILE, latent_tile, 0)


def _mla_attention(qcat, kv, proj, cos, sin):
    hps = MLA_HEADS_PER_STEP
    kr_cb = (3 * CONV_WIDTH + MLA_Q_RANK + MLA_KV_RANK) // LANES
    tab_spec = pl.BlockSpec((TB, LANES), lambda b, h: (0, 0))
    return pl.pallas_call(
        _mla_kernel,
        grid=(BATCH, MLA_HEADS // hps),
        in_specs=[
            pl.BlockSpec((TB, hps * MLA_QK_PAD), lambda b, h: (b, h)),
            pl.BlockSpec((TB, hps * (MLA_NOPE + MLA_V)), lambda b, h: (b, h)),
            pl.BlockSpec((TB, LANES), lambda b, h: (b, kr_cb)),
            tab_spec, tab_spec,
        ],
        out_specs=pl.BlockSpec((TB, hps * MLA_V), lambda b, h: (b, h)),
        out_shape=jax.ShapeDtypeStruct((M_ROWS, MLA_HEADS * MLA_V), BF16),
        scratch_shapes=[pltpu.VMEM((hps, TB, MLA_QK_PAD), BF16), pltpu.VMEM((hps, TB, 2 * MLA_V), BF16)],
        compiler_params=_params(("arbitrary", "arbitrary")),
        name="mla_attention",
    )(qcat, kv, proj, cos, sin)


def kernel(x, c, ctx, c_ctx, w_mod, b_mod, norm1_w, norm2_w, mlp_w1, mlp_w2, ev_w_in, ev_ret_log2_decay,
           ev_ret_gn_w, ev_sink, ev_w_out, od_w_in, od_conv_w, od_q_norm_w, od_kv_norm_w, od_w_uq, od_w_ukv,
           od_w_out, norm_f):
    assert x.shape == (BATCH, SEQ, D_MODEL) and ctx.shape == (BATCH, CTX_LEN, D_MODEL)
    rows = SEQ // GRID_W
    row = np.repeat(np.arange(rows), GRID_W)
    col = np.tile(np.arange(GRID_W), rows)
    pos = np.arange(SEQ)
    ret_cos, ret_sin = _rope_tables([pos], HEAD_DIM)
    win_cos, win_sin = _rope_tables([row, col], HEAD_DIM)
    mla_cos, mla_sin = _rope_tables([row, col], MLA_ROPE)

    cvec = jnp.concatenate([c, c_ctx[None], jnp.zeros((8 - BATCH - 1, D_MODEL), F32)], axis=0)
    mod_head = _modulation(cvec, w_mod, b_mod, 2 * D_MODEL)
    c_cols = cvec.T
    head_blocks = 2 * D_MODEL // MOD_COLS
    all_blocks = 6 * D_MODEL // MOD_COLS

    n_odd = od_w_uq.shape[0]
    w_uq = jnp.pad(od_w_uq.astype(BF16).reshape(n_odd, MLA_Q_RANK, MLA_HEADS, MLA_NOPE + MLA_ROPE),
                   ((0, 0), (0, 0), (0, 0), (0, MLA_QK_PAD - MLA_NOPE - MLA_ROPE))
                   ).reshape(n_odd, MLA_Q_RANK, MLA_HEADS * MLA_QK_PAD)

    w_in_odd = jnp.pad(od_w_in.astype(BF16), ((0, 0), (0, 0), (0, ODD_IN_PAD - ODD_IN)))

    for layer in range(DEPTH):
        j = layer // 2
        if layer == 0:
            h, n1 = _embed_norm(x, ctx, norm1_w[layer], mod_head.reshape(8, 2, D_MODEL))
        else:
            n1 = _norm_mod(h, norm1_w[layer], mod2, 0)
        if layer % 2 == 0:
            side_mod = (c_cols, w_mod, b_mod, 0, head_blocks, all_blocks - head_blocks) if layer == 0 else None
            proj = _matmul_ws([n1], ev_w_in, j, tn=EVEN_IN // 4, tm=MM_TM_BIG, side_mod=side_mod,
                              name="even_in_proj")
            if layer == 0:
                proj, mod_tail = proj
                mod2 = jnp.concatenate([mod_head, mod_tail], axis=1)
            mix_a = _retention(proj, ev_ret_log2_decay[j], ev_ret_gn_w[j], ret_cos, ret_sin)
            mix_b = _window_attention(proj, ev_sink[j], win_cos, win_sin)
            w_out = ev_w_out
        else:
            proj = _matmul_ws([n1], w_in_odd, j, tn=ODD_IN_PAD, tm=MM_TM_SMALL, name="odd_in_proj")
            mix_a = _short_conv(proj, od_conv_w[j])
            qcat = _matmul_ws([proj], w_uq, j, tn=MLA_HEADS * MLA_QK_PAD, tm=MM_TM_BIG, k_sizes=[MLA_Q_RANK],
                              a_col_blocks=[3 * CONV_WIDTH // MLA_Q_RANK], norm_w=od_q_norm_w[j], name="mla_q_up")
            kv = _matmul_ws([proj], od_w_ukv, j, tn=MLA_HEADS * (MLA_NOPE + MLA_V), tm=MM_TM_BIG,
                            k_sizes=[MLA_KV_RANK], a_col_blocks=[(3 * CONV_WIDTH + MLA_Q_RANK) // MLA_KV_RANK],
                            norm_w=od_kv_norm_w[j], name="mla_kv_up")
            mix_b = _mla_attention(qcat, kv, proj, mla_cos, mla_sin)
            w_out = od_w_out
        h, n2 = _matmul_ws([mix_a, mix_b], w_out, j, tn=D_MODEL, tm=MM_TM_SMALL, epilogue="gated_residual_norm",
                           res=h, mod=mod2, gate_idx=2, next_norm_w=norm2_w[layer], n_sub=2, name="mixer_out_proj")
        side_mod = (c_cols, w_mod, b_mod, layer + 1, 0, all_blocks) if layer + 1 < DEPTH else None
        ff, w2_bf16, *mod_next = _matmul_ws([n2], mlp_w1, layer, tn=1024, tm=MM_TM_BIG, epilogue="relu2",
                                            side_cast=(mlp_w2, layer), side_mod=side_mod, name="mlp_up")
        h = _matmul_ws([ff], w2_bf16[None], 0, tn=1024, tm=MM_TM_DOWN, epilogue="gated_residual",
                       res=h, mod=mod2, gate_idx=5, single_buffer_w=True, name="mlp_down")
        if mod_next:
            mod2 = mod_next[0]
    return _final_norm(h, norm_f)
```

```python
import functools

import numpy as np
import jax
import jax.numpy as jnp
from jax import lax
from jax.experimental import pallas as pl
from jax.experimental.pallas import tpu as pltpu

F32 = jnp.float32
BF16 = jnp.bfloat16

D_MODEL = 2048
BATCH = 2
SEQ = 4096
DEPTH = 4
GRID_W = 64
CTX_LEN = 256
HEAD_DIM = 128
ROPE_BASE = 10000.0
EPS = 1e-6
NEG_INF = -1e30

RET_HEADS = 8
RET_CHUNK = 128
RET_UNROLL_STATE = 8
RET_UNROLL_OUT = 17
WIN_HEADS = 8
WIN_KV_HEADS = 2
WIN_GROUP = WIN_HEADS // WIN_KV_HEADS
WIN_RADIUS = 128
WIN_BLOCK = 128
WIN_UNROLL = 4
CONV_WIDTH = 1024
MLA_HEADS = 8
MLA_Q_RANK = 512
MLA_KV_RANK = 256
MLA_NOPE = 128
MLA_ROPE = 64
MLA_V = 128
D_FF = 4 * D_MODEL

EVEN_IN = 4 * RET_HEADS * HEAD_DIM + (WIN_HEADS + 2 * WIN_KV_HEADS) * HEAD_DIM
ODD_IN = 3 * CONV_WIDTH + MLA_Q_RANK + MLA_KV_RANK + MLA_ROPE
ODD_IN_PAD = 3968
MLA_QK_PAD = 256

TB = CTX_LEN + SEQ
M_ROWS = BATCH * TB
LANES = 128
ROW_TILE = 256
MM_TM_BIG = M_ROWS // 8
MM_TM_SMALL = M_ROWS // 16
MOD_COLS = 256
MM_TM_DOWN = 512
VMEM_LIMIT = 60 * 1024 * 1024

NT_DIMS = (((1,), (1,)), ((), ()))
TN_DIMS = (((0,), (0,)), ((), ()))


def _params(sem, vmem=VMEM_LIMIT):
    return pltpu.CompilerParams(dimension_semantics=sem, vmem_limit_bytes=vmem)


def _rope_tables(pos_list, d_rot):
    per = d_rot // len(pos_list)
    inv = ROPE_BASE ** (-np.arange(0, per, 2, dtype=np.float64) / per)
    cos_parts, sin_parts = [], []
    for pos in pos_list:
        ang = pos.astype(np.float64)[:, None] * inv[None, :]
        c, s = np.cos(ang), np.sin(ang)
        cos_parts += [c, c]
        sin_parts += [-s, s]
    pad = ((0, 0), (0, LANES - d_rot))
    cos = np.pad(np.concatenate(cos_parts, axis=-1), pad)
    sin = np.pad(np.concatenate(sin_parts, axis=-1), pad)
    ctx_cos = np.pad(np.ones((CTX_LEN, d_rot)), pad)
    ctx_sin = np.zeros((CTX_LEN, LANES))
    return (jnp.asarray(np.concatenate([ctx_cos, cos], axis=0), F32),
            jnp.asarray(np.concatenate([ctx_sin, sin], axis=0), F32))


def _rotate(x, cos, sin, half):
    if 2 * half == LANES:
        partner = pltpu.roll(x, half, 1)
    else:
        lane = lax.broadcasted_iota(jnp.int32, (1, LANES), 1)
        first = (lane % (2 * half)) < half
        partner = jnp.where(first, pltpu.roll(x, LANES - half, 1), pltpu.roll(x, half, 1))
    return x * cos + partner * sin


def _modulation_kernel(c_ref, w_ref, b_ref, o_ref):
    s = jax.nn.silu(c_ref[...]).astype(BF16)
    o_ref[...] = jnp.dot(s, w_ref[...].astype(BF16), preferred_element_type=F32) + b_ref[...]


def _modulation(cvec, w_mod, b_mod, n_cols):
    tn = 1024
    return pl.pallas_call(
        _modulation_kernel,
        grid=(n_cols // tn,),
        in_specs=[
            pl.BlockSpec((8, D_MODEL), lambda j: (0, 0)),
            pl.BlockSpec((None, D_MODEL, tn), lambda j: (0, 0, j)),
            pl.BlockSpec((None, 1, tn), lambda j: (0, 0, j)),
        ],
        out_specs=pl.BlockSpec((8, tn), lambda j: (0, j)),
        out_shape=jax.ShapeDtypeStruct((8, n_cols), F32),
        compiler_params=_params(("arbitrary",)),
        name="modulation",
    )(cvec, w_mod, b_mod.reshape(DEPTH, 1, 6 * D_MODEL))


def _segment_of_tile(i):
    tiles = TB // ROW_TILE
    return jnp.where(i % tiles == 0, BATCH, i // tiles)


def _embed_norm_kernel(x_ref, ctx_ref, w_ref, mod_ref, h_ref, n_ref):
    is_ctx = pl.program_id(0) % (TB // ROW_TILE) == 0
    x = jnp.where(is_ctx, ctx_ref[...], x_ref[...])
    h_ref[...] = x
    y = x * lax.rsqrt(jnp.mean(x * x, axis=-1, keepdims=True) + EPS) * w_ref[...]
    n_ref[...] = (y * (1.0 + mod_ref[1:2, :]) + mod_ref[0:1, :]).astype(BF16)


def _embed_norm(x, ctx, norm_w, mod3):
    assert CTX_LEN == ROW_TILE
    tiles = TB // ROW_TILE
    row_spec = pl.BlockSpec((ROW_TILE, D_MODEL), lambda i: (i, 0))
    return pl.pallas_call(
        _embed_norm_kernel,
        grid=(M_ROWS // ROW_TILE,),
        in_specs=[
            pl.BlockSpec((None, ROW_TILE, D_MODEL), lambda i: (i // tiles, jnp.maximum(i % tiles - 1, 0), 0)),
            pl.BlockSpec((None, ROW_TILE, D_MODEL), lambda i: (i // tiles, 0, 0)),
            pl.BlockSpec((1, D_MODEL), lambda i: (0, 0)),
            pl.BlockSpec((None, mod3.shape[1], D_MODEL), lambda i: (_segment_of_tile(i), 0, 0)),
        ],
        out_specs=[row_spec, row_spec],
        out_shape=[jax.ShapeDtypeStruct((M_ROWS, D_MODEL), F32), jax.ShapeDtypeStruct((M_ROWS, D_MODEL), BF16)],
        compiler_params=_params(("arbitrary",)),
        name="embed_norm",
    )(x, ctx, norm_w.reshape(1, D_MODEL), mod3)


def _norm_mod_kernel(x_ref, w_ref, shift_ref, scale_ref, o_ref, *, tm):
    row0 = pl.program_id(0) * tm
    second = row0 >= TB
    within = row0 - jnp.where(second, TB, 0)
    one_segment = (within >= CTX_LEN) & (within + tm <= TB)
    x = x_ref[...]
    y = x * lax.rsqrt(jnp.mean(x * x, axis=-1, keepdims=True) + EPS)

    def finish(pick):
        o_ref[...] = (y * (w_ref[...] * (1.0 + pick(scale_ref))) + pick(shift_ref)).astype(BF16)

    @pl.when(one_segment)
    def _():
        seg = second.astype(jnp.int32)
        finish(lambda tab_ref: tab_ref[pl.ds(seg, 1), :])

    @pl.when(jnp.logical_not(one_segment))
    def _():
        finish(lambda tab_ref: _row_select(tab_ref, row0, tm))


def _norm_mod(h, norm_w, mod, shift_idx):
    tm = MM_TM_BIG
    return pl.pallas_call(
        functools.partial(_norm_mod_kernel, tm=tm),
        grid=(M_ROWS // tm,),
        in_specs=[
            pl.BlockSpec((tm, D_MODEL), lambda i: (i, 0)),
            pl.BlockSpec((1, D_MODEL), lambda i: (0, 0)),
            pl.BlockSpec((8, D_MODEL), lambda i: (0, shift_idx)),
            pl.BlockSpec((8, D_MODEL), lambda i: (0, shift_idx + 1)),
        ],
        out_specs=pl.BlockSpec((tm, D_MODEL), lambda i: (i, 0)),
        out_shape=jax.ShapeDtypeStruct((M_ROWS, D_MODEL), BF16),
        compiler_params=_params(("arbitrary",)),
        name="norm_mod",
    )(h, norm_w.reshape(1, D_MODEL), mod, mod)


FINAL_TILES_PER_STEP = 4


def _final_norm_kernel(*refs):
    x_refs, w_ref, o_ref = refs[:FINAL_TILES_PER_STEP], refs[-2], refs[-1]
    for k, x_ref in enumerate(x_refs):
        x = x_ref[...]
        o_ref[k * ROW_TILE:(k + 1) * ROW_TILE, :] = (
            x * lax.rsqrt(jnp.mean(x * x, axis=-1, keepdims=True) + EPS) * w_ref[...])


def _final_norm(h, norm_w):
    tiles = TB // ROW_TILE
    ctx_tiles = CTX_LEN // ROW_TILE
    per = FINAL_TILES_PER_STEP
    row_spec = lambda k: pl.BlockSpec(
        (ROW_TILE, D_MODEL), functools.partial(lambda b, t, k: (b * tiles + ctx_tiles + per * t + k, 0), k=k))
    return pl.pallas_call(
        _final_norm_kernel,
        grid=(BATCH, SEQ // (per * ROW_TILE)),
        in_specs=[row_spec(k) for k in range(per)] + [pl.BlockSpec((1, D_MODEL), lambda b, t: (0, 0))],
        out_specs=pl.BlockSpec((None, per * ROW_TILE, D_MODEL), lambda b, t: (b, t, 0)),
        out_shape=jax.ShapeDtypeStruct((BATCH, SEQ, D_MODEL), F32),
        compiler_params=_params(("arbitrary", "arbitrary")),
        name="final_norm",
    )(*([h] * per), norm_w.reshape(1, D_MODEL))


def _row_select(tab_ref, row0, tm):
    row = row0 + lax.broadcasted_iota(jnp.int32, (tm, 1), 0)
    second = row >= TB
    is_ctx = (row - jnp.where(second, TB, 0)) < CTX_LEN
    return jnp.where(is_ctx, tab_ref[2:3, :], jnp.where(second, tab_ref[1:2, :], tab_ref[0:1, :]))


def _mm_ws_kernel(*refs, k_sizes, epilogue, norm, tm, cast, n_sub, side_cast, side_mod_blocks):
    n_a = len(k_sizes)
    a_refs = refs[:n_a]
    w_ref = refs[n_a]
    pos = n_a + 1
    if norm:
        nw_ref = refs[pos]
        pos += 1
    if epilogue in ("gated_residual", "gated_residual_norm"):
        res_ref, gate_ref = refs[pos], refs[pos + 1]
        pos += 2
    if epilogue == "gated_residual_norm":
        n2w_ref, shift_ref, scale_ref = refs[pos:pos + 3]
        pos += 3
    if side_cast:
        side_in_ref = refs[pos]
        pos += 1
    if side_mod_blocks:
        ct_ref, wmod_ref, bmod_ref = refs[pos:pos + 3]
        pos += 3
    o_ref = refs[pos]
    pos += 1
    if epilogue == "gated_residual_norm":
        n_ref = refs[pos]
        pos += 1
    i = pl.program_id(1)
    if side_cast:
        side_out_ref = refs[pos]
        pos += 1
    if side_mod_blocks:
        mod_ref = refs[pos]
        pos += 1

    if cast:
        wb_ref = refs[pos]

        @pl.when(i == 0)
        def _():
            wb_ref[...] = w_ref[...].astype(BF16)
    else:
        wb_ref = w_ref

    ts = tm // n_sub
    for sub in range(n_sub):
        rs = slice(sub * ts, (sub + 1) * ts)
        row0 = i * tm + sub * ts
        acc = None
        off = 0
        for a_ref, ks in zip(a_refs, k_sizes):
            a = a_ref[rs, :]
            if norm:
                af = a.astype(F32)
                af = af * lax.rsqrt(jnp.mean(af * af, axis=-1, keepdims=True) + EPS) * nw_ref[...]
                a = af.astype(BF16)
            part = jnp.dot(a, wb_ref[off:off + ks, :], preferred_element_type=F32)
            acc = part if acc is None else acc + part
            off += ks

        if epilogue == "bf16":
            o_ref[rs, :] = acc.astype(BF16)
        elif epilogue == "relu2":
            r = jnp.maximum(acc, 0.0)
            o_ref[rs, :] = (r * r).astype(BF16)
        else:
            second = row0 >= TB
            within = row0 - jnp.where(second, TB, 0)
            one_segment = (within >= CTX_LEN) & (within + ts <= TB)

            def finish(pick, acc=acc, rs=rs):
                h = res_ref[rs, :] + pick(gate_ref) * acc
                o_ref[rs, :] = h
                if epilogue == "gated_residual_norm":
                    y = h * lax.rsqrt(jnp.mean(h * h, axis=-1, keepdims=True) + EPS)
                    n_ref[rs, :] = (y * (n2w_ref[...] * (1.0 + pick(scale_ref))) + pick(shift_ref)).astype(BF16)

            @pl.when(one_segment)
            def _(second=second, finish=finish):
                seg = second.astype(jnp.int32)
                finish(lambda tab_ref: tab_ref[pl.ds(seg, 1), :])

            @pl.when(jnp.logical_not(one_segment))
            def _(row0=row0, finish=finish):
                finish(lambda tab_ref: _row_select(tab_ref, row0, ts))

    if side_cast:
        side_out_ref[...] = side_in_ref[...].astype(BF16)
    if side_mod_blocks:
        s = jax.nn.silu(ct_ref[...])
        w = wmod_ref[...]
        row_id = lax.broadcasted_iota(jnp.int32, (8, w.shape[1]), 0)
        out = jnp.broadcast_to(bmod_ref[...], (8, w.shape[1]))
        for r in range(BATCH + 1):
            out = out + jnp.where(row_id == r, jnp.sum(w * s[:, r:r + 1], axis=0, keepdims=True), 0.0)
        mod_ref[...] = out


def _matmul_ws(a_list, w, layer, *, tn, tm, epilogue="bf16", a_col_blocks=None, k_sizes=None, norm_w=None,
               res=None, mod=None, gate_idx=0, next_norm_w=None, n_sub=1, side_cast=None, side_mod=None,
               single_buffer_w=False, name="matmul"):
    assert BATCH == 2
    _, kdim, n = w.shape
    if k_sizes is None:
        k_sizes = [a.shape[1] for a in a_list]
    if a_col_blocks is None:
        a_col_blocks = [0] * len(a_list)
    assert sum(k_sizes) == kdim and n % tn == 0 and M_ROWS % tm == 0
    row_tiles = M_ROWS // tm
    n_steps = (n // tn) * row_tiles
    step = lambda j, i: j * row_tiles + i
    gated = epilogue in ("gated_residual", "gated_residual_norm")
    in_specs = [pl.BlockSpec((tm, ks), functools.partial(lambda j, i, cb: (i, cb), cb=cb))
                for ks, cb in zip(k_sizes, a_col_blocks)]
    w_mode = dict(pipeline_mode=pl.Buffered(1)) if (n == tn or single_buffer_w) else {}
    in_specs.append(pl.BlockSpec((None, kdim, tn), lambda j, i: (layer, 0, j), **w_mode))
    operands = list(a_list) + [w]
    if norm_w is not None:
        assert len(a_list) == 1
        in_specs.append(pl.BlockSpec((1, kdim), lambda j, i: (0, 0)))
        operands.append(norm_w.reshape(1, kdim))
    out_specs = [pl.BlockSpec((tm, tn), lambda j, i: (i, j))]
    out_shape = [jax.ShapeDtypeStruct((M_ROWS, n), F32 if gated else BF16)]
    if gated:
        blocks_per_vec = D_MODEL // tn
        in_specs.append(pl.BlockSpec((tm, tn), lambda j, i: (i, j)))
        in_specs.append(pl.BlockSpec((8, tn), lambda j, i: (0, gate_idx * blocks_per_vec + j)))
        operands += [res, mod]
    if epilogue == "gated_residual_norm":
        assert tn == n == D_MODEL
        in_specs.append(pl.BlockSpec((1, D_MODEL), lambda j, i: (0, 0)))
        in_specs.append(pl.BlockSpec((8, D_MODEL), lambda j, i: (0, gate_idx + 1)))
        in_specs.append(pl.BlockSpec((8, D_MODEL), lambda j, i: (0, gate_idx + 2)))
        operands += [next_norm_w.reshape(1, D_MODEL), mod, mod]
        out_specs.append(pl.BlockSpec((tm, tn), lambda j, i: (i, j)))
        out_shape.append(jax.ShapeDtypeStruct((M_ROWS, n), BF16))
    if side_cast is not None:
        side_w, side_layer = side_cast
        _, side_k, side_n = side_w.shape
        slab = side_k // n_steps
        assert slab * n_steps == side_k
        in_specs.append(pl.BlockSpec((None, slab, side_n), lambda j, i: (side_layer, step(j, i), 0)))
        operands.append(side_w)
        out_specs.append(pl.BlockSpec((slab, side_n), lambda j, i: (step(j, i), 0)))
        out_shape.append(jax.ShapeDtypeStruct((side_k, side_n), BF16))
    side_mod_blocks = 0
    if side_mod is not None:
        ct, w_mod, b_mod, mod_layer, first_blk, side_mod_blocks = side_mod
        assert side_mod_blocks <= n_steps
        rel = lambda j, i: jnp.minimum(step(j, i), side_mod_blocks - 1)
        in_specs.append(pl.BlockSpec((D_MODEL, 8), lambda j, i: (0, 0)))
        in_specs.append(pl.BlockSpec((None, D_MODEL, MOD_COLS), lambda j, i: (mod_layer, 0, first_blk + rel(j, i))))
        in_specs.append(pl.BlockSpec((None, 1, MOD_COLS), lambda j, i: (mod_layer, 0, first_blk + rel(j, i))))
        operands += [ct, w_mod, b_mod.reshape(DEPTH, 1, 6 * D_MODEL)]
        out_specs.append(pl.BlockSpec((8, MOD_COLS), lambda j, i: (0, rel(j, i))))
        out_shape.append(jax.ShapeDtypeStruct((8, side_mod_blocks * MOD_COLS), F32))
    cast = w.dtype != BF16
    outs = pl.pallas_call(
        functools.partial(_mm_ws_kernel, k_sizes=tuple(k_sizes), epilogue=epilogue, norm=norm_w is not None,
                          tm=tm, cast=cast, n_sub=n_sub, side_cast=side_cast is not None,
                          side_mod_blocks=side_mod_blocks),
        grid=(n // tn, row_tiles),
        in_specs=in_specs,
        out_specs=out_specs,
        out_shape=out_shape,
        scratch_shapes=[pltpu.VMEM((kdim, tn), BF16)] if cast else [],
        compiler_params=_params(("arbitrary", "arbitrary")),
        name=name,
    )(*operands)
    return outs[0] if len(outs) == 1 else outs


def _retention_kernel(q_ref, k_ref, v_ref, g_ref, cos_ref, sin_ref, ld_ref, gn_ref, o_ref, kb_ref, kv_ref, st_ref):
    c = RET_CHUNK
    ctx_chunks = CTX_LEN // c
    lat_chunks = SEQ // c
    n_chunks = ctx_chunks + lat_chunks
    sc = HEAD_DIM ** -0.5
    lg = jnp.log1p(-jnp.exp2(ld_ref[...]))
    lgf = lg[0:1, 0:1]
    lgb = lg[1:2, 0:1]
    ii = lax.broadcasted_iota(jnp.int32, (c, 1), 0).astype(F32)
    jj = lax.broadcasted_iota(jnp.int32, (1, c), 1).astype(F32)
    diff = ii - jj
    dmat = jnp.where(diff >= 0, jnp.exp(lgf * jnp.maximum(diff, 0.0)), jnp.exp(lgb * jnp.maximum(-diff, 0.0)))
    qf_dec = jnp.exp(lgf * (ii + 1.0))
    qb_dec = jnp.exp(lgb * (c - ii))
    kf_dec = jnp.exp(lgf * (c - 1.0 - ii))
    kb_dec = jnp.exp(lgb * ii)
    gcf = jnp.exp(lgf * c)
    gcb = jnp.exp(lgb * c)

    def rows(n):
        return pl.ds(pl.multiple_of(n * c, c), c)

    def roped(ref, n):
        x = ref[rows(n), :].astype(F32)
        return _rotate(x, cos_ref[rows(n), :], sin_ref[rows(n), :], HEAD_DIM // 2)

    def increments(n, carry):
        k = roped(k_ref, n)
        kb_ref[rows(n), :] = k.astype(BF16)
        kd = jnp.concatenate([(k * kf_dec).astype(BF16), (k * kb_dec).astype(BF16)], axis=1)
        kv_ref[n] = lax.dot_general(kd, v_ref[rows(n), :], TN_DIMS, preferred_element_type=F32)
        return carry

    def recurrence(n0, cnt, s_f, s_b):
        def body(t, carry):
            s_f, s_b = carry
            nf = n0 + t
            nb = n0 + cnt - 1 - t
            st_ref[nf, 0:c, :] = s_f.astype(BF16)
            st_ref[nb, c:2 * c, :] = s_b.astype(BF16)
            return gcf * s_f + kv_ref[nf, 0:c, :], gcb * s_b + kv_ref[nb, c:2 * c, :]
        return lax.fori_loop(0, cnt, body, (s_f, s_b))

    def outputs(n, carry):
        q = roped(q_ref, n) * sc
        v = v_ref[rows(n), :]
        scores = lax.dot_general(q.astype(BF16), kb_ref[rows(n), :], NT_DIMS, preferred_element_type=F32) * dmat
        qd = jnp.concatenate([(q * qf_dec).astype(BF16), (q * qb_dec).astype(BF16)], axis=1)
        o = (jnp.dot(scores.astype(BF16), v, preferred_element_type=F32)
             + jnp.dot(qd, st_ref[n], preferred_element_type=F32))
        mu = jnp.mean(o, axis=-1, keepdims=True)
        oc = o - mu
        var = jnp.mean(oc * oc, axis=-1, keepdims=True)
        on = oc * lax.rsqrt(var + EPS) * gn_ref[...]
        g = g_ref[rows(n), :].astype(F32)
        o_ref[rows(n), :] = (jax.nn.silu(g) * on).astype(BF16)
        return carry

    lax.fori_loop(0, n_chunks, increments, 0, unroll=RET_UNROLL_STATE)
    zero = jnp.zeros((c, c), F32)
    s_f, s_b = recurrence(0, ctx_chunks, zero, zero)
    recurrence(ctx_chunks, lat_chunks, s_f, s_b)
    lax.fori_loop(0, n_chunks, outputs, 0, unroll=RET_UNROLL_OUT)


def _retention(proj, log2_decay, gn_w, cos, sin):
    h = RET_HEADS
    ld = jnp.broadcast_to(log2_decay.T[:, :, None], (h, 2, LANES))
    seq_spec = lambda cb0: pl.BlockSpec((TB, HEAD_DIM), functools.partial(lambda b, hh, cb0: (b, cb0 + hh), cb0=cb0))
    tab_spec = pl.BlockSpec((TB, LANES), lambda b, hh: (0, 0))
    return pl.pallas_call(
        _retention_kernel,
        grid=(BATCH, h),
        in_specs=[seq_spec(0), seq_spec(h), seq_spec(2 * h), seq_spec(3 * h), tab_spec, tab_spec,
                  pl.BlockSpec((None, 2, LANES), lambda b, hh: (hh, 0, 0)),
                  pl.BlockSpec((1, HEAD_DIM), lambda b, hh: (0, hh))],
        out_specs=pl.BlockSpec((TB, HEAD_DIM), lambda b, hh: (b, hh)),
        out_shape=jax.ShapeDtypeStruct((M_ROWS, h * HEAD_DIM), BF16),
        scratch_shapes=[pltpu.VMEM((TB, HEAD_DIM), BF16),
                        pltpu.VMEM((TB // RET_CHUNK, 2 * RET_CHUNK, HEAD_DIM), F32),
                        pltpu.VMEM((TB // RET_CHUNK, 2 * RET_CHUNK, HEAD_DIM), BF16)],
        compiler_params=_params(("arbitrary", "arbitrary")),
        name="retention",
    )(proj, proj, proj, proj, cos, sin, ld, gn_w.reshape(1, h * HEAD_DIM))


def _window_kernel(sink_ref, q_ref, k_ref, v_ref, cos_ref, sin_ref, o_ref, kr_ref, vx_ref):
    blk = WIN_BLOCK
    ctx_chunks = CTX_LEN // blk
    n_chunks = TB // blk
    kvh = pl.program_id(1)
    sc = HEAD_DIM ** -0.5
    half = HEAD_DIM // 4
    rows_q = WIN_GROUP * blk

    kr_ref[...] = _rotate(k_ref[...].astype(F32), cos_ref[...], sin_ref[...], half).astype(BF16)
    vx_ref[:, 0:HEAD_DIM] = v_ref[...]
    vx_ref[:, HEAD_DIM:2 * HEAD_DIM] = jnp.ones((TB, HEAD_DIM), BF16)

    row_iota = lax.broadcasted_iota(jnp.int32, (rows_q, 1), 0)
    head = row_iota // blk
    sink = jnp.zeros((rows_q, 1), F32)
    for g in range(WIN_GROUP):
        sink = jnp.where(head == g, sink_ref[kvh * WIN_GROUP + g], sink)

    def chunk(n, carry):
        rq = pl.ds(pl.multiple_of(n * blk, blk), blk)
        cos = cos_ref[rq, :]
        sin = sin_ref[rq, :]
        q = jnp.concatenate(
            [(_rotate(q_ref[rq, g * HEAD_DIM:(g + 1) * HEAD_DIM].astype(F32), cos, sin, half) * sc).astype(BF16)
             for g in range(WIN_GROUP)], axis=0)

        cstart = jnp.clip(n - 1, ctx_chunks, n_chunks - 3)
        rw = pl.ds(pl.multiple_of(cstart * blk, blk), 3 * blk)
        s_c = lax.dot_general(q, kr_ref[0:CTX_LEN, :], NT_DIMS, preferred_element_type=F32)
        s_w = lax.dot_general(q, kr_ref[rw, :], NT_DIMS, preferred_element_type=F32)
        qpos = (n - ctx_chunks) * blk + lax.broadcasted_iota(jnp.int32, (blk, 1), 0)
        kpos = (cstart - ctx_chunks) * blk + lax.broadcasted_iota(jnp.int32, (1, 3 * blk), 1)
        valid = (jnp.abs(qpos - kpos) <= WIN_RADIUS) & (n >= ctx_chunks)
        s_w = jnp.concatenate([jnp.where(valid, s_w[g * blk:(g + 1) * blk, :], NEG_INF)
                               for g in range(WIN_GROUP)], axis=0)

        m = jnp.maximum(jnp.maximum(jnp.max(s_c, axis=-1, keepdims=True), jnp.max(s_w, axis=-1, keepdims=True)),
                        sink)
        p_c = jnp.exp(s_c - m).astype(BF16)
        p_w = jnp.exp(s_w - m).astype(BF16)
        ox = (jnp.dot(p_c, vx_ref[0:CTX_LEN, :], preferred_element_type=F32)
              + jnp.dot(p_w, vx_ref[rw, :], preferred_element_type=F32))
        o = ox[:, 0:HEAD_DIM] / (ox[:, HEAD_DIM:2 * HEAD_DIM] + jnp.exp(sink - m))
        for g in range(WIN_GROUP):
            o_ref[rq, g * HEAD_DIM:(g + 1) * HEAD_DIM] = o[g * blk:(g + 1) * blk, :].astype(BF16)
        return carry

    lax.fori_loop(0, n_chunks, chunk, 0, unroll=WIN_UNROLL)


def _window_attention(proj, sink, cos, sin):
    gw = WIN_GROUP * HEAD_DIM
    q_cb0 = 4 * RET_HEADS * HEAD_DIM // gw
    k_cb0 = (4 * RET_HEADS + WIN_HEADS) * HEAD_DIM // HEAD_DIM
    v_cb0 = k_cb0 + WIN_KV_HEADS
    tab_spec = pl.BlockSpec((TB, LANES), lambda b, kh: (0, 0))
    return pl.pallas_call(
        _window_kernel,
        grid=(BATCH, WIN_KV_HEADS),
        in_specs=[
            pl.BlockSpec(memory_space=pltpu.SMEM),
            pl.BlockSpec((TB, gw), lambda b, kh: (b, q_cb0 + kh)),
            pl.BlockSpec((TB, HEAD_DIM), lambda b, kh: (b, k_cb0 + kh)),
            pl.BlockSpec((TB, HEAD_DIM), lambda b, kh: (b, v_cb0 + kh)),
            tab_spec, tab_spec,
        ],
        out_specs=pl.BlockSpec((TB, gw), lambda b, kh: (b, kh)),
        out_shape=jax.ShapeDtypeStruct((M_ROWS, WIN_HEADS * HEAD_DIM), BF16),
        scratch_shapes=[pltpu.VMEM((TB, HEAD_DIM), BF16), pltpu.VMEM((TB, 2 * HEAD_DIM), BF16)],
        compiler_params=_params(("arbitrary", "arbitrary")),
        name="window_attention",
    )(sink, proj, proj, proj, cos, sin)


def _conv_kernel(b_ref, c_ref, x_ref, w_ref, o_ref):
    u = c_ref[...].astype(F32) * x_ref[...].astype(F32)
    row = lax.broadcasted_iota(jnp.int32, (TB, 1), 0)
    prev = jnp.where((row == 0) | (row == CTX_LEN), 0.0, pltpu.roll(u, 1, 0))
    nxt = jnp.where((row == CTX_LEN - 1) | (row == TB - 1), 0.0, pltpu.roll(u, TB - 1, 0))
    z = prev * w_ref[0:1, :] + u * w_ref[1:2, :] + nxt * w_ref[2:3, :]
    o_ref[...] = (b_ref[...].astype(F32) * z).astype(BF16)


def _short_conv(proj, conv_w):
    tc = 256
    nblk = CONV_WIDTH // tc
    spec = lambda part: pl.BlockSpec((TB, tc), functools.partial(lambda b, j, part: (b, part * nblk + j), part=part))
    return pl.pallas_call(
        _conv_kernel,
        grid=(BATCH, nblk),
        in_specs=[spec(0), spec(1), spec(2), pl.BlockSpec((3, tc), lambda b, j: (0, j))],
        out_specs=pl.BlockSpec((TB, tc), lambda b, j: (b, j)),
        out_shape=jax.ShapeDtypeStruct((M_ROWS, CONV_WIDTH), BF16),
        compiler_params=_params(("arbitrary", "arbitrary")),
        name="short_conv",
    )(proj, proj, proj, conv_w)


MLA_HEADS_PER_STEP = 2
MLA_Q_TILE = 1024
MLA_KEY_CHUNK = 1024


def _mla_kernel(q_ref, kv_ref, kr_ref, cos_ref, sin_ref, o_ref, kc_ref, vx_ref):
    sc = (MLA_NOPE + MLA_ROPE) ** -0.5
    half = MLA_ROPE // 4
    kv_w = MLA_NOPE + MLA_V

    kr = _rotate(kr_ref[...].astype(F32), cos_ref[...], sin_ref[...], half).astype(BF16)
    for hh in range(MLA_HEADS_PER_STEP):
        kc_ref[hh, :, 0:MLA_NOPE] = kv_ref[:, hh * kv_w:hh * kv_w + MLA_NOPE]
        kc_ref[hh, :, MLA_NOPE:MLA_QK_PAD] = kr
        vx_ref[hh, :, 0:MLA_V] = kv_ref[:, hh * kv_w + MLA_NOPE:(hh + 1) * kv_w]
        vx_ref[hh, :, MLA_V:2 * MLA_V] = jnp.ones((TB, MLA_V), BF16)

    def attend(row0, n_rows, key_chunks):
        rq = pl.ds(row0, n_rows)
        cos = cos_ref[rq, :]
        sin = sin_ref[rq, :]
        for hh in range(MLA_HEADS_PER_STEP):
            c0 = hh * MLA_QK_PAD
            qn = (q_ref[rq, c0:c0 + MLA_NOPE].astype(F32) * sc).astype(BF16)
            qr = (_rotate(q_ref[rq, c0 + MLA_NOPE:c0 + MLA_QK_PAD].astype(F32), cos, sin, half) * sc).astype(BF16)
            q = jnp.concatenate([qn, qr], axis=1)
            m = None
            acc = None
            for k0, kn in key_chunks:
                s = lax.dot_general(q, kc_ref[hh, k0:k0 + kn, :], NT_DIMS, preferred_element_type=F32)
                m_chunk = jnp.max(s, axis=-1, keepdims=True)
                m_new = m_chunk if m is None else jnp.maximum(m, m_chunk)
                p = jnp.exp(s - m_new).astype(BF16)
                pv = jnp.dot(p, vx_ref[hh, k0:k0 + kn, :], preferred_element_type=F32)
                acc = pv if acc is None else acc * jnp.exp(m - m_new) + pv
                m = m_new
            o_ref[rq, hh * MLA_V:(hh + 1) * MLA_V] = (acc[:, 0:MLA_V] / acc[:, MLA_V:2 * MLA_V]).astype(BF16)

    ctx_keys = [(0, CTX_LEN)]
    all_keys = ctx_keys + [(CTX_LEN + k * MLA_KEY_CHUNK, MLA_KEY_CHUNK) for k in range(SEQ // MLA_KEY_CHUNK)]
    attend(0, CTX_LEN, ctx_keys)

    def latent_tile(t, carry):
        attend(pl.multiple_of(CTX_LEN + t * MLA_Q_TILE, ROW_TILE), MLA_Q_TILE, all_keys)
        return carry

    lax.fori_loop(0, SEQ // MLA_Q_TILE, latent_tile, 0)


def _mla_attention(qcat, kv, proj, cos, sin):
    hps = MLA_HEADS_PER_STEP
    kr_cb = (3 * CONV_WIDTH + MLA_Q_RANK + MLA_KV_RANK) // LANES
    tab_spec = pl.BlockSpec((TB, LANES), lambda b, h: (0, 0))
    return pl.pallas_call(
        _mla_kernel,
        grid=(BATCH, MLA_HEADS // hps),
        in_specs=[
            pl.BlockSpec((TB, hps * MLA_QK_PAD), lambda b, h: (b, h)),
            pl.BlockSpec((TB, hps * (MLA_NOPE + MLA_V)), lambda b, h: (b, h)),
            pl.BlockSpec((TB, LANES), lambda b, h: (b, kr_cb)),
            tab_spec, tab_spec,
        ],
        out_specs=pl.BlockSpec((TB, hps * MLA_V), lambda b, h: (b, h)),
        out_shape=jax.ShapeDtypeStruct((M_ROWS, MLA_HEADS * MLA_V), BF16),
        scratch_shapes=[pltpu.VMEM((hps, TB, MLA_QK_PAD), BF16), pltpu.VMEM((hps, TB, 2 * MLA_V), BF16)],
        compiler_params=_params(("arbitrary", "arbitrary")),
        name="mla_attention",
    )(qcat, kv, proj, cos, sin)


def kernel(x, c, ctx, c_ctx, w_mod, b_mod, norm1_w, norm2_w, mlp_w1, mlp_w2, ev_w_in, ev_ret_log2_decay,
           ev_ret_gn_w, ev_sink, ev_w_out, od_w_in, od_conv_w, od_q_norm_w, od_kv_norm_w, od_w_uq, od_w_ukv,
           od_w_out, norm_f):
    assert x.shape == (BATCH, SEQ, D_MODEL) and ctx.shape == (BATCH, CTX_LEN, D_MODEL)
    rows = SEQ // GRID_W
    row = np.repeat(np.arange(rows), GRID_W)
    col = np.tile(np.arange(GRID_W), rows)
    pos = np.arange(SEQ)
    ret_cos, ret_sin = _rope_tables([pos], HEAD_DIM)
    win_cos, win_sin = _rope_tables([row, col], HEAD_DIM)
    mla_cos, mla_sin = _rope_tables([row, col], MLA_ROPE)

    cvec = jnp.concatenate([c, c_ctx[None], jnp.zeros((8 - BATCH - 1, D_MODEL), F32)], axis=0)
    mod_head = _modulation(cvec, w_mod, b_mod, 2 * D_MODEL)
    c_cols = cvec.T
    head_blocks = 2 * D_MODEL // MOD_COLS
    all_blocks = 6 * D_MODEL // MOD_COLS

    n_odd = od_w_uq.shape[0]
    w_uq = jnp.pad(od_w_uq.astype(BF16).reshape(n_odd, MLA_Q_RANK, MLA_HEADS, MLA_NOPE + MLA_ROPE),
                   ((0, 0), (0, 0), (0, 0), (0, MLA_QK_PAD - MLA_NOPE - MLA_ROPE))
                   ).reshape(n_odd, MLA_Q_RANK, MLA_HEADS * MLA_QK_PAD)

    w_in_odd = jnp.pad(od_w_in.astype(BF16), ((0, 0), (0, 0), (0, ODD_IN_PAD - ODD_IN)))

    for layer in range(DEPTH):
        j = layer // 2
        if layer == 0:
            h, n1 = _embed_norm(x, ctx, norm1_w[layer], mod_head.reshape(8, 2, D_MODEL))
        else:
            n1 = _norm_mod(h, norm1_w[layer], mod2, 0)
        if layer % 2 == 0:
            side_mod = (c_cols, w_mod, b_mod, 0, head_blocks, all_blocks - head_blocks) if layer == 0 else None
            proj = _matmul_ws([n1], ev_w_in, j, tn=EVEN_IN // 4, tm=MM_TM_BIG, side_mod=side_mod,
                              name="even_in_proj")
            if layer == 0:
                proj, mod_tail = proj
                mod2 = jnp.concatenate([mod_head, mod_tail], axis=1)
            mix_a = _retention(proj, ev_ret_log2_decay[j], ev_ret_gn_w[j], ret_cos, ret_sin)
            mix_b = _window_attention(proj, ev_sink[j], win_cos, win_sin)
            w_out = ev_w_out
        else:
            proj = _matmul_ws([n1], w_in_odd, j, tn=ODD_IN_PAD, tm=MM_TM_SMALL, name="odd_in_proj")
            mix_a = _short_conv(proj, od_conv_w[j])
            qcat = _matmul_ws([proj], w_uq, j, tn=MLA_HEADS * MLA_QK_PAD, tm=MM_TM_BIG, k_sizes=[MLA_Q_RANK],
                              a_col_blocks=[3 * CONV_WIDTH // MLA_Q_RANK], norm_w=od_q_norm_w[j], name="mla_q_up")
            kv = _matmul_ws([proj], od_w_ukv, j, tn=MLA_HEADS * (MLA_NOPE + MLA_V), tm=MM_TM_BIG,
                            k_sizes=[MLA_KV_RANK], a_col_blocks=[(3 * CONV_WIDTH + MLA_Q_RANK) // MLA_KV_RANK],
                            norm_w=od_kv_norm_w[j], name="mla_kv_up")
            mix_b = _mla_attention(qcat, kv, proj, mla_cos, mla_sin)
            w_out = od_w_out
        h, n2 = _matmul_ws([mix_a, mix_b], w_out, j, tn=D_MODEL, tm=MM_TM_SMALL, epilogue="gated_residual_norm",
                           res=h, mod=mod2, gate_idx=2, next_norm_w=norm2_w[layer], n_sub=2, name="mixer_out_proj")
        side_mod = (c_cols, w_mod, b_mod, layer + 1, 0, all_blocks) if layer + 1 < DEPTH else None
        ff, w2_bf16, *mod_next = _matmul_ws([n2], mlp_w1, layer, tn=1024, tm=MM_TM_BIG, epilogue="relu2",
                                            side_cast=(mlp_w2, layer), side_mod=side_mod, name="mlp_up")
        h = _matmul_ws([ff], w2_bf16[None], 0, tn=1024, tm=MM_TM_DOWN, epilogue="gated_residual",
                       res=h, mod=mod2, gate_idx=5, single_buffer_w=True, name="mlp_down")
        if mod_next:
            mod2 = mod_next[0]
    return _final_norm(h, norm_f)
```
